```python
import jax, jax.numpy as jnp
from jax import lax
import numpy as np

D_MODEL = 2048
BATCH = 4
SEQ = 4096
DEPTH = 4

GRID_W = 64
CTX_LEN = 256
N_MIXERS = 2
N_RET_LAYERS = (DEPTH + N_MIXERS - 1) // N_MIXERS
N_HG_LAYERS = DEPTH // N_MIXERS
RET_HEADS = 8
RET_HEAD_DIM = D_MODEL // RET_HEADS
RET_CHUNK = 128
HG_EXPAND = 128
HG_HEADS = D_MODEL // HG_EXPAND
HG_CHUNK = 64
FFN_HIDDEN = -(-8 * D_MODEL // (3 * 256)) * 256
ROPE_BASE = 10000.0
EPS = 1e-6

kernel_name = "hybrid_retention_hgrn2_prefix_dit"


def rms_norm(x, gain):
    xf = x.astype(jnp.float32)
    y = xf * lax.rsqrt(jnp.mean(xf * xf, axis=-1, keepdims=True) + EPS)
    return (y * gain.astype(jnp.float32)).astype(x.dtype)


def modulate(h, shift, scale):
    return h * (1 + scale) + shift


def swiglu(h, w_gate_up, w_down):
    a, b = jnp.split(h @ w_gate_up, 2, axis=-1)
    return (jax.nn.silu(a) * b) @ w_down


def grid_positions(seq):
    rows_n = seq // GRID_W
    rows = jnp.repeat(jnp.arange(rows_n, dtype=jnp.float32), GRID_W)
    cols = jnp.tile(jnp.arange(GRID_W, dtype=jnp.float32), rows_n)
    return rows, cols


def rope_2d(t, rows, cols):
    half = t.shape[-1] // 2
    quarter = half // 2
    inv_freq = ROPE_BASE ** (-jnp.arange(quarter, dtype=jnp.float32) / quarter)
    ang = jnp.concatenate([rows[:, None] * inv_freq, cols[:, None] * inv_freq], axis=-1)
    cos, sin = jnp.cos(ang), jnp.sin(ang)
    t1, t2 = t[..., :half], t[..., half:]
    return jnp.concatenate([t1 * cos - t2 * sin, t1 * sin + t2 * cos], axis=-1)


def to_heads(t, n_heads):
    b, n, _ = t.shape
    return t.reshape(b, n, n_heads, -1).transpose(0, 2, 1, 3).astype(jnp.float32)


def from_heads(t):
    b, h, n, d = t.shape
    return t.transpose(0, 2, 1, 3).reshape(b, n, h * d)


def identity(t):
    return t


def flip_seq(t):
    return jnp.flip(t, axis=2)


def to_chunks(t, size):
    b, h, n, d = t.shape
    return t.reshape(b, h, n // size, size, d).transpose(2, 0, 1, 3, 4)


def from_chunks(t):
    n, b, h, size, d = t.shape
    return t.transpose(1, 2, 0, 3, 4).reshape(b, h, n * size, d)


def retention_chunks(q, k, v, log_gamma, s0):
    size = RET_CHUNK
    idx = jnp.arange(size, dtype=jnp.float32)
    lg = log_gamma[:, None]
    diff = idx[:, None] - idx[None, :]
    intra = jnp.where(diff >= 0, jnp.exp(lg[:, :, None] * jnp.maximum(diff, 0.0)), 0.0)
    q_dec = jnp.exp(lg * (idx + 1.0))[:, :, None]
    k_dec = jnp.exp(lg * (size - 1.0 - idx))[:, :, None]
    c_dec = jnp.exp(log_gamma * size)[:, None, None]

    def step(s, blk):
        qb, kb, vb = blk
        scores = jnp.einsum('bhnd,bhmd->bhnm', qb, kb) * intra
        o = jnp.einsum('bhnm,bhme->bhne', scores, vb) + jnp.einsum('bhnd,bhde->bhne', qb * q_dec, s)
        s = s * c_dec + jnp.einsum('bhmd,bhme->bhde', kb * k_dec, vb)
        return s, o

    s, o = lax.scan(step, s0, (to_chunks(q, size), to_chunks(k, size), to_chunks(v, size)))
    return from_chunks(o), s


def retention_state(k, v, log_gamma):
    n = k.shape[2]
    w = jnp.exp(log_gamma[:, None] * (n - 1.0 - jnp.arange(n, dtype=jnp.float32)))
    return jnp.einsum('bhtd,bhte->bhde', k * w[:, :, None], v)


def hgrn_chunks(q, k, v, log_f, s0):
    size = HG_CHUNK
    causal = jnp.tril(jnp.ones((size, size), dtype=bool))[:, :, None]

    def step(s, blk):
        qb, kb, vb, gb = blk
        cum = jnp.cumsum(gb, axis=2)
        diff = cum[:, :, :, None, :] - cum[:, :, None, :, :]
        pair = jnp.where(causal, jnp.exp(jnp.minimum(diff, 0.0)), 0.0)
        scores = jnp.einsum('bhnd,bhmd,bhnmd->bhnm', qb, kb, pair)
        o = jnp.einsum('bhnm,bhme->bhne', scores, vb) + jnp.einsum('bhnd,bhde->bhne', qb * jnp.exp(cum), s)
        cum_end = cum[:, :, -1:, :]
        s = jnp.exp(cum_end[:, :, 0, :, None]) * s + jnp.einsum('bhmd,bhme->bhde', kb * jnp.exp(cum_end - cum), vb)
        return s, o

    s, o = lax.scan(step, s0, (to_chunks(q, size), to_chunks(k, size), to_chunks(v, size), to_chunks(log_f, size)))
    return from_chunks(o), s


def hgrn_state(k, v, log_f):
    cum = jnp.cumsum(log_f, axis=2)
    return jnp.einsum('bhtd,bhte->bhde', k * jnp.exp(cum[:, :, -1:, :] - cum), v)


def retention_mixer(h, hc, w_in, w_out, decay_logits, ctx_out):
    rows, cols = grid_positions(h.shape[1])
    k_scale = RET_HEAD_DIM ** -0.5
    q, k, v, g = jnp.split(h @ w_in, 4, axis=-1)
    qc, kc, vc, gc = jnp.split(hc @ w_in, 4, axis=-1)
    q = rope_2d(to_heads(q, RET_HEADS), rows, cols)
    k = rope_2d(to_heads(k, RET_HEADS), rows, cols) * k_scale
    v = to_heads(v, RET_HEADS)
    qc = to_heads(qc, RET_HEADS)
    kc = to_heads(kc, RET_HEADS) * k_scale
    vc = to_heads(vc, RET_HEADS)
    log_gamma = jax.nn.log_sigmoid(decay_logits.astype(jnp.float32))
    s_zero = jnp.zeros(kc.shape[:2] + (kc.shape[-1], vc.shape[-1]), jnp.float32)
    o_lat, o_ctx = 0.0, 0.0
    for d, orient in enumerate((identity, flip_seq)):
        if ctx_out:
            oc, s_ctx = retention_chunks(orient(qc), orient(kc), orient(vc), log_gamma[d], s_zero)
            o_ctx = o_ctx + orient(oc)
        else:
            s_ctx = retention_state(orient(kc), orient(vc), log_gamma[d])
        ol, _ = retention_chunks(orient(q), orient(k), orient(v), log_gamma[d], s_ctx)
        o_lat = o_lat + orient(ol)

    def readout(o, gate):
        o = o * lax.rsqrt(jnp.mean(o * o, axis=-1, keepdims=True) + EPS)
        return (from_heads(o).astype(gate.dtype) * jax.nn.silu(gate)) @ w_out

    return readout(o_lat, g), (readout(o_ctx, gc) if ctx_out else None)


def hgrn_mixer(h, hc, w_in, w_out, norm_gain, lower_bound, ctx_out):
    lb = lower_bound.astype(jnp.float32).reshape(HG_HEADS, 1, HG_EXPAND)

    def project(t):
        q, f_fwd, f_bwd, i, g = jnp.split(t @ w_in, 5, axis=-1)
        return (jax.nn.silu(to_heads(q, HG_HEADS)),
                (to_heads(f_fwd, HG_HEADS), to_heads(f_bwd, HG_HEADS)),
                to_heads(i, HG_HEADS), g)

    def gates(z):
        f = lb + (1.0 - lb) * jax.nn.sigmoid(z)
        return jnp.log(f), (1.0 - lb) * jax.nn.sigmoid(-z)

    q, fz, i, g = project(h)
    qc, fzc, ic, gc = project(hc)
    s_zero = jnp.zeros(ic.shape[:2] + (HG_EXPAND, ic.shape[-1]), jnp.float32)
    o_lat, o_ctx = 0.0, 0.0
    for d, orient in enumerate((identity, flip_seq)):
        log_f, key = gates(fz[d])
        log_fc, keyc = gates(fzc[d])
        if ctx_out:
            oc, s_ctx = hgrn_chunks(orient(qc), orient(keyc), orient(ic), orient(log_fc), s_zero)
            o_ctx = o_ctx + orient(oc)
        else:
            s_ctx = hgrn_state(orient(keyc), orient(ic), orient(log_fc))
        ol, _ = hgrn_chunks(orient(q), orient(key), orient(i), orient(log_f), s_ctx)
        o_lat = o_lat + orient(ol)

    def readout(o, gate):
        return (rms_norm(from_heads(o), norm_gain).astype(gate.dtype) * jax.nn.sigmoid(gate)) @ w_out

    return readout(o_lat, g), (readout(o_ctx, gc) if ctx_out else None)


def setup_inputs(seed: int = 0) -> dict:
    key = jax.random.key(seed)
    ks = jax.random.split(key, 18)
    f32 = jnp.float32
    D = D_MODEL

    def nrm(k, shape, scale):
        return jax.random.normal(k, shape, f32) * scale

    base_logit = jnp.log(jnp.exp2(5.0 + jnp.arange(RET_HEADS, dtype=f32)) - 1.0)
    return {
        "x": nrm(ks[0], (BATCH, SEQ, D), 1.0),
        "c": nrm(ks[1], (BATCH, D), 1.0),
        "ctx": nrm(ks[2], (BATCH, CTX_LEN, D), 1.0),
        "c_ctx": nrm(ks[3], (D,), 1.0),
        "ada_w": nrm(ks[4], (DEPTH, D, 6 * D), 0.5 * D ** -0.5),
        "ada_b": nrm(ks[5], (DEPTH, 6 * D), 0.02),
        "norm1_g": 1.0 + nrm(ks[6], (DEPTH, D), 0.02),
        "norm2_g": 1.0 + nrm(ks[7], (DEPTH, D), 0.02),
        "ret_w_in": nrm(ks[8], (N_RET_LAYERS, D, 4 * D), D ** -0.5),
        "ret_w_out": nrm(ks[9], (N_RET_LAYERS, D, D), D ** -0.5),
        "ret_decay_logits": base_logit + nrm(ks[10], (N_RET_LAYERS, 2, RET_HEADS), 0.1),
        "hg_w_in": nrm(ks[11], (N_HG_LAYERS, D, 5 * D), D ** -0.5),
        "hg_w_out": nrm(ks[12], (N_HG_LAYERS, D, D), D ** -0.5),
        "hg_norm_g": 1.0 + nrm(ks[13], (N_HG_LAYERS, D), 0.02),
        "hg_lower_bounds": nrm(ks[14], (N_HG_LAYERS, D), 0.1),
        "ffn_w_gate_up": nrm(ks[15], (DEPTH, D, 2 * FFN_HIDDEN), D ** -0.5),
        "ffn_w_down": nrm(ks[16], (DEPTH, FFN_HIDDEN, D), FFN_HIDDEN ** -0.5),
        "final_norm_g": 1.0 + nrm(ks[17], (D,), 0.02),
    }


def reference(x, c, ctx, c_ctx, ada_w, ada_b, norm1_g, norm2_g, ret_w_in, ret_w_out,
              ret_decay_logits, hg_w_in, hg_w_out, hg_norm_g, hg_lower_bounds,
              ffn_w_gate_up, ffn_w_down, final_norm_g):
    lb_p = jax.nn.softmax(hg_lower_bounds.astype(jnp.float32), axis=0)
    lower_bounds = jnp.cumsum(lb_p, axis=0) - lb_p[0]
    c_act = jax.nn.silu(c)
    cc_act = jax.nn.silu(c_ctx)
    for layer in range(DEPTH):
        last = layer == DEPTH - 1
        j = layer // N_MIXERS
        mod = (c_act @ ada_w[layer] + ada_b[layer])[:, None, :]
        mod_c = cc_act @ ada_w[layer] + ada_b[layer]
        sh1, sc1, gt1, sh2, sc2, gt2 = jnp.split(mod, 6, axis=-1)
        csh1, csc1, cgt1, csh2, csc2, cgt2 = jnp.split(mod_c, 6, axis=-1)
        h = modulate(rms_norm(x, norm1_g[layer]), sh1, sc1)
        hc = modulate(rms_norm(ctx, norm1_g[layer]), csh1, csc1)
        if layer % N_MIXERS == 0:
            mix, mix_c = retention_mixer(h, hc, ret_w_in[j], ret_w_out[j], ret_decay_logits[j], not last)
        else:
            mix, mix_c = hgrn_mixer(h, hc, hg_w_in[j], hg_w_out[j], hg_norm_g[j], lower_bounds[j], not last)
        x = x + gt1 * mix
        x = x + gt2 * swiglu(modulate(rms_norm(x, norm2_g[layer]), sh2, sc2),
                             ffn_w_gate_up[layer], ffn_w_down[layer])
        if not last:
            ctx = ctx + cgt1 * mix_c
            ctx = ctx + cgt2 * swiglu(modulate(rms_norm(ctx, norm2_g[layer]), csh2, csc2),
                                      ffn_w_gate_up[layer], ffn_w_down[layer])
    return rms_norm(x, final_norm_g)
```

```python
import functools

import jax
import jax.numpy as jnp
from jax import lax
from jax.experimental import pallas as pl
from jax.experimental.pallas import tpu as pltpu

F32 = jnp.float32
BF16 = jnp.bfloat16

EPS = 1e-6
ROPE_BASE = 10000.0
GRID_W = 64
N_MIXERS = 2
RET_HEADS = 8
HG_EXPAND = 128

V7X_LANES = 128
V7X_SUBLANES = 8
V7X_VMEM_BYTES = 64 * 1024 * 1024

ROW_TILE = 512
IN_COL_TILE = 512
FFN_HID_TILE = 512
ADA_COL_TILE = 1024
RET_CHUNK = 256
HG_CHUNK = 128
ROW_STEP = 64
VMEM_LIMIT = 56 * 1024 * 1024


def _params(*semantics):
    return pltpu.CompilerParams(dimension_semantics=semantics, vmem_limit_bytes=VMEM_LIMIT)


def _sigmoid(x):
    return 1.0 / (1.0 + jnp.exp(-x))


def _dot(a, b):
    return jnp.dot(a, b, preferred_element_type=F32)


def _dot_nt(a, b):
    return lax.dot_general(a, b, (((1,), (1,)), ((), ())), preferred_element_type=F32)


def _dot_tn(a, b):
    return lax.dot_general(a, b, (((0,), (0,)), ((), ())), preferred_element_type=F32)


def _ada_kernel(c_ref, w_ref, b_ref, o_ref):
    c = c_ref[...]
    a = (c * _sigmoid(c)).astype(BF16)
    o_ref[...] = _dot(a, w_ref[...].astype(BF16)) + b_ref[...]


def _ada_mod(cond, ada_w, ada_b):
    depth, d, n = ada_w.shape
    rows = cond.shape[0]
    return pl.pallas_call(
        _ada_kernel,
        grid=(depth, n // ADA_COL_TILE),
        in_specs=[
            pl.BlockSpec((rows, d), lambda l, j: (0, 0)),
            pl.BlockSpec((None, d, ADA_COL_TILE), lambda l, j: (l, 0, j)),
            pl.BlockSpec((None, 1, ADA_COL_TILE), lambda l, j: (l, 0, j)),
        ],
        out_specs=pl.BlockSpec((None, rows, ADA_COL_TILE), lambda l, j: (l, 0, j)),
        out_shape=jax.ShapeDtypeStruct((depth, rows, n), F32),
        compiler_params=_params("parallel", "parallel"),
        name="ada_mod",
    )(cond, ada_w, ada_b.reshape(depth, 1, n))


def _norm_mod(x, gain, shift, scale):
    y = x * lax.rsqrt(jnp.mean(x * x, axis=-1, keepdims=True) + EPS) * gain
    return y * (1.0 + scale) + shift


def _row_loop(rows, body):
    def step(i, carry):
        body(pl.ds(pl.multiple_of(i * ROW_STEP, ROW_STEP), ROW_STEP))
        return carry
    lax.fori_loop(0, rows // ROW_STEP, step, 0)


def _mod_spec(d, slot, rows_per_batch, n_batch):
    def index(i, *_):
        return (jnp.minimum((i * ROW_TILE) // rows_per_batch, n_batch), 0, slot)
    return pl.BlockSpec((None, 1, d), index)


def _inproj_kernel(x_ref, g_ref, sh_ref, sc_ref, w_ref, o_ref, h_ref):
    @pl.when(pl.program_id(1) == 0)
    def _():
        def body(rows):
            h = _norm_mod(x_ref[rows, :], g_ref[...], sh_ref[...], sc_ref[...])
            h_ref[rows, :] = h.astype(BF16)
        _row_loop(ROW_TILE, body)

    o_ref[...] = _dot(h_ref[...], w_ref[...]).astype(o_ref.dtype)


def _in_proj(x, gain, mod, w, rows_per_batch, n_batch):
    r, d = x.shape
    n = w.shape[1]
    return pl.pallas_call(
        _inproj_kernel,
        grid=(r // ROW_TILE, n // IN_COL_TILE),
        in_specs=[
            pl.BlockSpec((ROW_TILE, d), lambda i, j: (i, 0)),
            pl.BlockSpec((1, d), lambda i, j: (0, 0)),
            _mod_spec(d, 0, rows_per_batch, n_batch),
            _mod_spec(d, 1, rows_per_batch, n_batch),
            pl.BlockSpec((d, IN_COL_TILE), lambda i, j: (0, j)),
        ],
        out_specs=pl.BlockSpec((ROW_TILE, IN_COL_TILE), lambda i, j: (i, j)),
        out_shape=jax.ShapeDtypeStruct((r, n), BF16),
        scratch_shapes=[pltpu.VMEM((ROW_TILE, d), BF16)],
        compiler_params=_params("parallel", "arbitrary"),
        name="in_proj",
    )(x, gain.reshape(1, d), mod, mod, w)


def _outproj_kernel(o_ref, g_ref, ng_ref, w_ref, x_ref, gt_ref, n2_ref, sh_ref, sc_ref,
                    xo_ref, h_ref, y_ref, *, hgrn):
    def gate(rows):
        o = o_ref[rows, :].astype(F32)
        g = g_ref[rows, :].astype(F32)
        if hgrn:
            o = o * lax.rsqrt(jnp.mean(o * o, axis=-1, keepdims=True) + EPS) * ng_ref[...]
            y = o * _sigmoid(g)
        else:
            y = o * (g * _sigmoid(g))
        y_ref[rows, :] = y.astype(BF16)
    _row_loop(ROW_TILE, gate)

    xo_ref[...] = _dot(y_ref[...], w_ref[...])

    def resid(rows):
        xn = x_ref[rows, :] + gt_ref[...] * xo_ref[rows, :]
        xo_ref[rows, :] = xn
        h_ref[rows, :] = _norm_mod(xn, n2_ref[...], sh_ref[...], sc_ref[...]).astype(BF16)
    _row_loop(ROW_TILE, resid)


def _out_proj(o, proj, gate_block, norm_gain, w, x, mod, gain2, rows_per_batch, n_batch, n_rows, hgrn):
    d = x.shape[1]
    row = lambda i: (i, 0)
    const = lambda i: (0, 0)
    return pl.pallas_call(
        functools.partial(_outproj_kernel, hgrn=hgrn),
        grid=(n_rows // ROW_TILE,),
        in_specs=[
            pl.BlockSpec((ROW_TILE, d), row),
            pl.BlockSpec((ROW_TILE, d), lambda i: (i, gate_block)),
            pl.BlockSpec((1, d), const),
            pl.BlockSpec((d, d), const),
            pl.BlockSpec((ROW_TILE, d), row),
            _mod_spec(d, 2, rows_per_batch, n_batch),
            pl.BlockSpec((1, d), const),
            _mod_spec(d, 3, rows_per_batch, n_batch),
            _mod_spec(d, 4, rows_per_batch, n_batch),
        ],
        out_specs=[pl.BlockSpec((ROW_TILE, d), row), pl.BlockSpec((ROW_TILE, d), row)],
        out_shape=[jax.ShapeDtypeStruct((n_rows, d), F32), jax.ShapeDtypeStruct((n_rows, d), BF16)],
        scratch_shapes=[pltpu.VMEM((ROW_TILE, d), BF16)],
        compiler_params=_params("parallel"),
        name="out_proj",
    )(o, proj, norm_gain.reshape(1, d), w, x, mod, gain2.reshape(1, d), mod, mod)


def _ffn_kernel(h_ref, wg_ref, wu_ref, wd_ref, x_ref, gt_ref, fg_ref, o_ref, acc_ref, *, final_norm):
    j = pl.program_id(1)
    h = h_ref[...]
    a = _dot(h, wg_ref[...])
    b = _dot(h, wu_ref[...])
    p = (a * _sigmoid(a) * b).astype(BF16)
    part = _dot(p, wd_ref[...])

    @pl.when(j == 0)
    def _():
        acc_ref[...] = part

    @pl.when(j > 0)
    def _():
        acc_ref[...] += part

    @pl.when(j == pl.num_programs(1) - 1)
    def _():
        def body(rows):
            xn = x_ref[rows, :] + gt_ref[...] * acc_ref[rows, :]
            if final_norm:
                xn = xn * lax.rsqrt(jnp.mean(xn * xn, axis=-1, keepdims=True) + EPS) * fg_ref[...]
            o_ref[rows, :] = xn
        _row_loop(ROW_TILE, body)


def _ffn(h2, w_gate_up, w_down, x, mod, final_gain, rows_per_batch, n_batch, n_rows, final_norm):
    d = x.shape[1]
    hidden = w_down.shape[0]
    n_hid = hidden // FFN_HID_TILE
    row = lambda i, j: (i, 0)
    return pl.pallas_call(
        functools.partial(_ffn_kernel, final_norm=final_norm),
        grid=(n_rows // ROW_TILE, n_hid),
        in_specs=[
            pl.BlockSpec((ROW_TILE, d), row),
            pl.BlockSpec((d, FFN_HID_TILE), lambda i, j: (0, j)),
            pl.BlockSpec((d, FFN_HID_TILE), lambda i, j: (0, j + n_hid)),
            pl.BlockSpec((FFN_HID_TILE, d), lambda i, j: (j, 0)),
            pl.BlockSpec((ROW_TILE, d), row),
            _mod_spec(d, 5, rows_per_batch, n_batch),
            pl.BlockSpec((1, d), lambda i, j: (0, 0)),
        ],
        out_specs=pl.BlockSpec((ROW_TILE, d), row),
        out_shape=jax.ShapeDtypeStruct((n_rows, d), F32),
        scratch_shapes=[pltpu.VMEM((ROW_TILE, d), F32)],
        compiler_params=_params("parallel", "arbitrary"),
        name="ffn",
    )(h2, w_gate_up, w_gate_up, w_down, x, mod, final_gain.reshape(1, d))


def _ret_kernel(lg_ref, q_ref, k_ref, v_ref, qc_ref, kc_ref, vc_ref, cos_ref, sin_ref,
                o_ref, oc_ref, dec_ref, kr_ref, sbs_ref, sf_ref, sb_ref, *, k_scale):
    c = RET_CHUNK
    dk = q_ref.shape[1]
    half = dk // 2
    n_chunks = q_ref.shape[0] // c
    head = pl.program_id(1)
    lgf = lg_ref[0, head]
    lgb = lg_ref[1, head]

    n_i = lax.broadcasted_iota(jnp.int32, (c, c), 0).astype(F32)
    m_i = lax.broadcasted_iota(jnp.int32, (c, c), 1).astype(F32)
    diff = n_i - m_i
    dec_ref[0] = (jnp.where(diff >= 0, jnp.exp(lgf * jnp.maximum(diff, 0.0)), 0.0)
                  + jnp.where(diff <= 0, jnp.exp(lgb * jnp.maximum(-diff, 0.0)), 0.0))
    t_i = lax.broadcasted_iota(jnp.int32, (c, dk), 0).astype(F32)
    dec_ref[1] = jnp.exp(lgf * (t_i + 1.0))
    dec_ref[2] = jnp.exp(lgb * (c - t_i))
    dec_ref[3] = jnp.exp(lgf * (c - 1.0 - t_i))
    dec_ref[4] = jnp.exp(lgb * t_i)
    cf = jnp.exp(lgf * c)
    cb = jnp.exp(lgb * c)

    def head_norm(o):
        return o * lax.rsqrt(jnp.mean(o * o, axis=-1, keepdims=True) + EPS)

    def rope(t, rows):
        cos = cos_ref[rows, :]
        sin = sin_ref[rows, :]
        t1 = t[:, :half]
        t2 = t[:, half:]
        return jnp.concatenate([t1 * cos - t2 * sin, t1 * sin + t2 * cos], axis=-1)

    qc = qc_ref[...]
    kc = (kc_ref[...].astype(F32) * k_scale)
    vc = vc_ref[...]
    sc = _dot_nt(qc, kc.astype(BF16)) * dec_ref[0]
    oc_ref[...] = head_norm(_dot(sc.astype(BF16), vc)).astype(oc_ref.dtype)
    sf_ref[...] = _dot_tn((kc * dec_ref[3]).astype(BF16), vc)
    sb_ref[...] = _dot_tn((kc * dec_ref[4]).astype(BF16), vc)

    def bwd(i, carry):
        ci = n_chunks - 1 - i
        rows = pl.ds(pl.multiple_of(ci * c, c), c)
        kr = rope(k_ref[rows, :].astype(F32), rows) * k_scale
        kr_ref[rows, :] = kr.astype(BF16)
        s = sb_ref[...]
        sbs_ref[ci] = s.astype(BF16)
        sb_ref[...] = s * cb + _dot_tn((kr * dec_ref[4]).astype(BF16), v_ref[rows, :])
        return carry
    lax.fori_loop(0, n_chunks, bwd, 0)

    def fwd(ci, carry):
        rows = pl.ds(pl.multiple_of(ci * c, c), c)
        qr = rope(q_ref[rows, :].astype(F32), rows).astype(BF16)
        kr = kr_ref[rows, :]
        v = v_ref[rows, :]
        s = sf_ref[...]
        sc = _dot_nt(qr, kr) * dec_ref[0]
        o = _dot(sc.astype(BF16), v)
        o += dec_ref[1] * _dot(qr, s.astype(BF16))
        o += dec_ref[2] * _dot(qr, sbs_ref[ci])
        sf_ref[...] = s * cf + _dot_tn((kr.astype(F32) * dec_ref[3]).astype(BF16), v)
        o_ref[rows, :] = head_norm(o).astype(o_ref.dtype)
        return carry
    lax.fori_loop(0, n_chunks, fwd, 0)


def _retention(proj, log_gamma, cos, sin, n_batch, seq, ctx_len):
    heads = RET_HEADS
    dk = proj.shape[1] // (4 * heads)
    d = heads * dk
    assert dk == RET_CHUNK and ctx_len == RET_CHUNK and seq % RET_CHUNK == 0
    ctx_row0 = (n_batch * seq) // ctx_len
    lat = lambda part: pl.BlockSpec((seq, dk), lambda b, h: (b, part * heads + h))
    ctx = lambda part: pl.BlockSpec((ctx_len, dk), lambda b, h: (ctx_row0 + b, part * heads + h))
    tab =pl.BlockSpec((seq, dk // 2), lambda b, h: (0, 0))
    n_chunks = seq // RET_CHUNK
    return pl.pallas_call(
        functools.partial(_ret_kernel, k_scale=dk ** -0.5),
        grid=(n_batch, heads),
        in_specs=[pl.BlockSpec(memory_space=pltpu.SMEM),
                  lat(0), lat(1), lat(2), ctx(0), ctx(1), ctx(2), tab, tab],
        out_specs=[pl.BlockSpec((seq, dk), lambda b, h: (b, h)),
                   pl.BlockSpec((ctx_len, dk), lambda b, h: (b, h))],
        out_shape=[jax.ShapeDtypeStruct((n_batch * seq, d), BF16),
                   jax.ShapeDtypeStruct((n_batch * ctx_len, d), BF16)],
        scratch_shapes=[
            pltpu.VMEM((5, RET_CHUNK, RET_CHUNK), F32),
            pltpu.VMEM((seq, dk), BF16),
            pltpu.VMEM((n_chunks, dk, dk), BF16),
            pltpu.VMEM((dk, dk), F32),
            pltpu.VMEM((dk, dk), F32),
        ],
        compiler_params=_params("parallel", "arbitrary"),
        name="retention",
    )(log_gamma, proj, proj, proj, proj, proj, proj, cos, sin)


def _hg_gates(z, lb):
    en = jnp.exp(-jnp.abs(z))
    r = 1.0 / (1.0 + en)
    pos = z >= 0
    f = lb + (1.0 - lb) * jnp.where(pos, r, en * r)
    key = (1.0 - lb) * jnp.where(pos, en * r, r)
    return f, jnp.log(f), key


def _hg_cumsum(g):
    sub = lax.broadcasted_iota(jnp.int32, g.shape, 0) & (V7X_SUBLANES - 1)
    x = g
    for s in (1, 2, 4):
        x = x + jnp.where(sub >= s, pltpu.roll(x, s, axis=0), 0.0)
    blocks, lasts = [], []
    carry = None
    for j in range(g.shape[0] // V7X_SUBLANES):
        blk = x[j * V7X_SUBLANES:(j + 1) * V7X_SUBLANES]
        if carry is not None:
            blk = blk + carry
        carry = jnp.broadcast_to(blk[V7X_SUBLANES - 1:, :], blk.shape)
        blocks.append(blk)
        lasts.append(carry)
    return blocks, lasts


def _tiles(x):
    return [x[j * V7X_SUBLANES:(j + 1) * V7X_SUBLANES] for j in range(x.shape[0] // V7X_SUBLANES)]


def _hg_levels(up, lo, cv, cin, lasts, pair_even, pair_odd):
    nt = len(cv)
    sub = lax.broadcasted_iota(jnp.int32, cv[0].shape, 0)
    out = []
    bt = nt
    while bt >= 2:
        ht = bt // 2
        rows = []
        for j in range(nt):
            b0 = (j // bt) * bt
            ref = lasts[b0 + ht - 1]
            if j - b0 >= ht:
                rows.append(jnp.exp(cv[j] - ref) * up[j])
            else:
                rows.append(jnp.exp(ref - cv[j]) * lo[j])
        out.append(jnp.concatenate(rows, axis=0).astype(BF16))
        bt = ht
    for size in (8, 4):
        hs = size // 2
        upper = (sub & hs) != 0
        rows = []
        for j in range(nt):
            if size == 8:
                ref = jnp.broadcast_to(cin[j][3:4, :], cin[j].shape)
            else:
                ref = jnp.where(sub < 4, jnp.broadcast_to(cin[j][1:2, :], cin[j].shape),
                                jnp.broadcast_to(cin[j][5:6, :], cin[j].shape))
            e = jnp.where(upper, cv[j] - ref, ref - cv[j])
            rows.append(jnp.exp(e) * jnp.where(upper, up[j], lo[j]))
        out.append(jnp.concatenate(rows, axis=0).astype(BF16))
    odd = (sub & 1) != 0
    rows = [jnp.where(odd, pair_odd[j], pair_even[j]) for j in range(nt)]
    out.append(jnp.concatenate(rows, axis=0).astype(BF16))
    return out


def _hg_kernel(lb_ref, q_ref, zf_ref, zb_ref, v_ref, qc_ref, zfc_ref, zbc_ref, vc_ref,
               o_ref, oc_ref, sbs_ref, sf_ref, sb_ref):
    c = HG_CHUNK
    lb = lb_ref[...]

    n_i = lax.broadcasted_iota(jnp.int32, (c, c), 0)
    m_i = lax.broadcasted_iota(jnp.int32, (c, c), 1)

    def bwd_state_step(z_ref, val_ref, ci):
        rows = pl.ds(pl.multiple_of(ci * c, c), c)
        _, logf, key = _hg_gates(z_ref[rows, :].astype(F32), lb)
        cin, lasts = _hg_cumsum(logf)
        total = lasts[-1][0:1, :]
        cum = jnp.concatenate(cin, axis=0)
        kdec = (key * jnp.exp(cum - logf)).astype(BF16)
        s = sb_ref[...]
        sbs_ref[ci] = s.astype(BF16)
        sb_ref[...] = s * jnp.exp(total) + _dot_tn(val_ref[rows, :], kdec)

    def chunk_out(qr_ref, zfr_ref, zbr_ref, vr_ref, out_ref, ci):
        rows = pl.ds(pl.multiple_of(ci * c, c), c)
        qraw = qr_ref[rows, :].astype(F32)
        q = qraw * _sigmoid(qraw)
        v = vr_ref[rows, :]
        qt = _tiles(q)

        ff, logf, keyf = _hg_gates(zfr_ref[rows, :].astype(F32), lb)
        cinf, lastf = _hg_cumsum(logf)
        kft = _tiles(keyf)
        qff = _tiles(q * ff)
        zf_levels = _hg_levels(qt, kft, cinf, cinf, lastf, kft, qff)
        cumf = jnp.concatenate(cinf, axis=0)
        totf = lastf[-1][0:1, :]

        fb, logb, keyb = _hg_gates(zbr_ref[rows, :].astype(F32), lb)
        cinb, lastb = _hg_cumsum(logb)
        cumb = jnp.concatenate(cinb, axis=0)
        cumxb = cumb - logb
        kbt = _tiles(keyb)
        qfb = _tiles(q * fb)
        zb_levels = _hg_levels(kbt, qt, _tiles(cumxb), cinb, lastb, qfb, kbt)
        totb = lastb[-1][0:1, :]

        x = n_i ^ m_i
        lower = n_i > m_i
        a = None
        size = c
        for zf_l, zb_l in zip(zf_levels, zb_levels):
            p = jnp.where(lower, _dot_nt(zf_l, zf_l), _dot_nt(zb_l, zb_l))
            a = p if a is None else jnp.where(x < size, p, a)
            size //= 2
        qb = q.astype(BF16)
        diag = _dot_nt(qb, (keyf + keyb).astype(BF16))
        a = jnp.where(x == 0, diag, a)

        o = _dot(a.astype(BF16), v)
        sf = sf_ref[...]
        o += _dot_nt((q * jnp.exp(cumf)).astype(BF16), sf.astype(BF16))
        o += _dot_nt((q * jnp.exp(totb - cumxb)).astype(BF16), sbs_ref[ci])
        out_ref[rows, :] = o.astype(out_ref.dtype)
        kdec = (keyf * jnp.exp(totf - cumf)).astype(BF16)
        sf_ref[...] = sf * jnp.exp(totf) + _dot_tn(v, kdec)

    def run(qr_ref, zfr_ref, zbr_ref, vr_ref, out_ref):
        n_chunks = qr_ref.shape[0] // c

        def bwd(i, carry):
            bwd_state_step(zbr_ref, vr_ref, n_chunks - 1 - i)
            return carry
        lax.fori_loop(0, n_chunks, bwd, 0)

        def fwd(ci, carry):
            chunk_out(qr_ref, zfr_ref, zbr_ref, vr_ref, out_ref, ci)
            return carry
        lax.fori_loop(0, n_chunks, fwd, 0)

    sf_ref[...] = jnp.zeros_like(sf_ref)
    sb_ref[...] = jnp.zeros_like(sb_ref)
    run(qc_ref, zfc_ref, zbc_ref, vc_ref, oc_ref)
    run(q_ref, zf_ref, zb_ref, v_ref, o_ref)


def _hgrn(proj, lower_bound, n_batch, seq, ctx_len):
    dk = HG_EXPAND
    heads = proj.shape[1] // (5 * dk)
    d = heads * dk
    ctx_row0 = (n_batch * seq) // ctx_len
    lat = lambda part: pl.BlockSpec((seq, dk), lambda b, h: (b, part * heads + h))
    ctx = lambda part: pl.BlockSpec((ctx_len, dk), lambda b, h: (ctx_row0 + b, part * heads + h))
    n_chunks = seq // HG_CHUNK
    return pl.pallas_call(
        _hg_kernel,
        grid=(n_batch, heads),
        in_specs=[pl.BlockSpec((None, 1, dk), lambda b, h: (h, 0, 0)),
                  lat(0), lat(1), lat(2), lat(3), ctx(0), ctx(1), ctx(2), ctx(3)],
        out_specs=[pl.BlockSpec((seq, dk), lambda b, h: (b, h)),
                   pl.BlockSpec((ctx_len, dk), lambda b, h: (b, h))],
        out_shape=[jax.ShapeDtypeStruct((n_batch * seq, d), BF16),
                   jax.ShapeDtypeStruct((n_batch * ctx_len, d), BF16)],
        scratch_shapes=[
            pltpu.VMEM((n_chunks, dk, dk), BF16),
            pltpu.VMEM((dk, dk), F32),
            pltpu.VMEM((dk, dk), F32),
        ],
        compiler_params=_params("parallel", "arbitrary"),
        name="hgrn",
    )(lower_bound.reshape(heads, 1, dk), proj, proj, proj, proj, proj, proj, proj, proj)


def _rope_tables(seq, head_dim):
    quarter = head_dim // 4
    rows = jnp.repeat(jnp.arange(seq // GRID_W, dtype=F32), GRID_W)
    cols = jnp.tile(jnp.arange(GRID_W, dtype=F32), seq // GRID_W)
    inv_freq = ROPE_BASE ** (-jnp.arange(quarter, dtype=F32) / quarter)
    ang = jnp.concatenate([rows[:, None] * inv_freq, cols[:, None] * inv_freq], axis=-1)
    return jnp.cos(ang), jnp.sin(ang)


def kernel(x, c, ctx, c_ctx, ada_w, ada_b, norm1_g, norm2_g, ret_w_in, ret_w_out, ret_decay_logits,
           hg_w_in, hg_w_out, hg_norm_g, hg_lower_bounds, ffn_w_gate_up, ffn_w_down, final_norm_g):
    n_batch, seq, d = x.shape
    ctx_len = ctx.shape[1]
    depth = ada_w.shape[0]
    n_lat = n_batch * seq
    n_all = n_lat + n_batch * ctx_len
    assert seq % ROW_TILE == 0 and (n_batch * ctx_len) % ROW_TILE == 0

    xs = jnp.concatenate([x.reshape(n_lat, d), ctx.reshape(n_batch * ctx_len, d)], axis=0)

    cond_rows = -(-(n_batch + 1) // V7X_SUBLANES) * V7X_SUBLANES
    cond = jnp.zeros((cond_rows, d), F32).at[:n_batch].set(c).at[n_batch].set(c_ctx)
    mods = _ada_mod(cond, ada_w, ada_b).reshape(depth, cond_rows, 1, 6 * d)

    lb_p = jax.nn.softmax(hg_lower_bounds.astype(F32), axis=0)
    lower_bounds = jnp.cumsum(lb_p, axis=0) - lb_p[0]
    log_gamma = jax.nn.log_sigmoid(ret_decay_logits.astype(F32))
    cos, sin = _rope_tables(seq, d // RET_HEADS)

    out = None
    for layer in range(depth):
        last = layer == depth - 1
        j = layer // N_MIXERS
        mod = mods[layer]
        n_rows = n_lat if last else n_all
        retention = layer % N_MIXERS == 0
        w_in = (ret_w_in if retention else hg_w_in)[j].astype(BF16)
        w_out = (ret_w_out if retention else hg_w_out)[j].astype(BF16)
        proj = _in_proj(xs, norm1_g[layer], mod, w_in, seq, n_batch)
        if retention:
            o_lat, o_ctx = _retention(proj, log_gamma[j], cos, sin, n_batch, seq, ctx_len)
            norm_gain = jnp.ones((d,), F32)
        else:
            o_lat, o_ctx = _hgrn(proj, lower_bounds[j], n_batch, seq, ctx_len)
            norm_gain = hg_norm_g[j]
        o = o_lat if last else jnp.concatenate([o_lat, o_ctx], axis=0)
        gate_block = proj.shape[1] // d - 1
        xs, h2 = _out_proj(o, proj, gate_block, norm_gain, w_out, xs, mod, norm2_g[layer],
                           seq, n_batch, n_rows, not retention)
        xs = _ffn(h2, ffn_w_gate_up[layer].astype(BF16), ffn_w_down[layer].astype(BF16), xs, mod,
                  final_norm_g, seq, n_batch, n_rows, last)
    return xs.reshape(n_batch, seq, d)
```

```python
import functools

import jax
import jax.numpy as jnp
from jax import lax
from jax.experimental import pallas as pl
from jax.experimental.pallas import tpu as pltpu

F32 = jnp.float32
BF16 = jnp.bfloat16

EPS = 1e-6
LOG2E = 1.4426950408889634
ROPE_BASE = 10000.0
GRID_W = 64
N_MIXERS = 2
RET_HEADS = 8
HG_EXPAND = 128

V7X_LANES = 128
V7X_SUBLANES = 8
V7X_VMEM_BYTES = 64 * 1024 * 1024

ROW_TILE = 512
BIG_ROW_TILE = 1024
IN_COL_TILE = 1024
FFN_HID_TILE = 512
FFN_OUT_TILE = 512
MXU_COLS = 256
ADA_COL_TILE = 1024
RET_CHUNK = 256
HG_CHUNK = 128
ROW_STEP = 64
VMEM_LIMIT = 56 * 1024 * 1024


def _params(*semantics):
    return pltpu.CompilerParams(dimension_semantics=semantics, vmem_limit_bytes=VMEM_LIMIT)


def _sigmoid(x):
    return 1.0 / (1.0 + jnp.exp(-x))


def _dot(a, b):
    return jnp.dot(a, b, preferred_element_type=F32)


def _dot_nt(a, b):
    return lax.dot_general(a, b, (((1,), (1,)), ((), ())), preferred_element_type=F32)


def _dot_tn(a, b):
    return lax.dot_general(a, b, (((0,), (0,)), ((), ())), preferred_element_type=F32)


def _ada_kernel(c_ref, w_ref, b_ref, o_ref):
    c = c_ref[...]
    a = (c * _sigmoid(c)).astype(BF16)
    o_ref[...] = _dot(a, w_ref[...].astype(BF16)) + b_ref[...]


def _ada_mod(cond, ada_w, ada_b):
    depth, d, n = ada_w.shape
    rows = cond.shape[0]
    return pl.pallas_call(
        _ada_kernel,
        grid=(depth, n // ADA_COL_TILE),
        in_specs=[
            pl.BlockSpec((rows, d), lambda l, j: (0, 0)),
            pl.BlockSpec((None, d, ADA_COL_TILE), lambda l, j: (l, 0, j)),
            pl.BlockSpec((None, 1, ADA_COL_TILE), lambda l, j: (l, 0, j)),
        ],
        out_specs=pl.BlockSpec((None, rows, ADA_COL_TILE), lambda l, j: (l, 0, j)),
        out_shape=jax.ShapeDtypeStruct((depth, rows, n), F32),
        compiler_params=_params("parallel", "parallel"),
        name="ada_mod",
    )(cond, ada_w, ada_b.reshape(depth, 1, n))


def _norm_mod(x, gain, shift, scale):
    y = x * lax.rsqrt(jnp.mean(x * x, axis=-1, keepdims=True) + EPS) * gain
    return y * (1.0 + scale) + shift


def _row_loop(rows, body):
    def step(i, carry):
        body(pl.ds(pl.multiple_of(i * ROW_STEP, ROW_STEP), ROW_STEP))
        return carry
    lax.fori_loop(0, rows // ROW_STEP, step, 0)


def _mod_spec(width, slot, row_tile, rows_per_batch, n_batch, col_axis=None):
    def index(*ids):
        col = slot if col_axis is None else slot + ids[col_axis]
        return (jnp.minimum((ids[0] * row_tile) // rows_per_batch, n_batch), 0, col)
    return pl.BlockSpec((None, 1, width), index)


def _inproj_kernel(x_ref, g_ref, sh_ref, sc_ref, w_ref, o_ref, h_ref):
    @pl.when(pl.program_id(1) == 0)
    def _():
        def body(rows):
            h = _norm_mod(x_ref[rows, :], g_ref[...], sh_ref[...], sc_ref[...])
            h_ref[rows, :] = h.astype(BF16)
        _row_loop(BIG_ROW_TILE, body)

    o_ref[...] = _dot(h_ref[...], w_ref[...]).astype(o_ref.dtype)


def _in_proj(x, gain, mod, w, rows_per_batch, n_batch):
    r, d = x.shape
    n = w.shape[1]
    tm = BIG_ROW_TILE
    return pl.pallas_call(
        _inproj_kernel,
        grid=(r // tm, n // IN_COL_TILE),
        in_specs=[
            pl.BlockSpec((tm, d), lambda i, j: (i, 0)),
            pl.BlockSpec((1, d), lambda i, j: (0, 0)),
            _mod_spec(d, 0, tm, rows_per_batch, n_batch),
            _mod_spec(d, 1, tm, rows_per_batch, n_batch),
            pl.BlockSpec((d, IN_COL_TILE), lambda i, j: (0, j)),
        ],
        out_specs=pl.BlockSpec((tm, IN_COL_TILE), lambda i, j: (i, j)),
        out_shape=jax.ShapeDtypeStruct((r, n), BF16),
        scratch_shapes=[pltpu.VMEM((tm, d), BF16)],
        compiler_params=_params("parallel", "arbitrary"),
        name="in_proj",
    )(x, gain.reshape(1, d), mod, mod, w)


def _outproj_kernel(o_ref, g_ref, ng_ref, w_ref, x_ref, gt_ref, n2_ref, sh_ref, sc_ref,
                    xo_ref, h_ref, y_ref, *, hgrn):
    def gate(rows):
        o = o_ref[rows, :].astype(F32)
        g = g_ref[rows, :].astype(F32)
        if hgrn:
            o = o * lax.rsqrt(jnp.mean(o * o, axis=-1, keepdims=True) + EPS) * ng_ref[...]
            y = o * _sigmoid(g)
        else:
            y = o * (g * _sigmoid(g))
        y_ref[rows, :] = y.astype(BF16)
    _row_loop(ROW_TILE, gate)

    xo_ref[...] = _dot(y_ref[...], w_ref[...])

    def resid(rows):
        xn = x_ref[rows, :] + gt_ref[...] * xo_ref[rows, :]
        xo_ref[rows, :] = xn
        h_ref[rows, :] = _norm_mod(xn, n2_ref[...], sh_ref[...], sc_ref[...]).astype(BF16)
    _row_loop(ROW_TILE, resid)


def _out_proj(o, proj, gate_block, norm_gain, w, x, mod, gain2, rows_per_batch, n_batch, n_rows, hgrn):
    d = x.shape[1]
    tm = ROW_TILE
    row = lambda i: (i, 0)
    const = lambda i: (0, 0)
    return pl.pallas_call(
        functools.partial(_outproj_kernel, hgrn=hgrn),
        grid=(n_rows // tm,),
        in_specs=[
            pl.BlockSpec((tm, d), row),
            pl.BlockSpec((tm, d), lambda i: (i, gate_block)),
            pl.BlockSpec((1, d), const),
            pl.BlockSpec((d, d), const),
            pl.BlockSpec((tm, d), row),
            _mod_spec(d, 2, tm, rows_per_batch, n_batch),
            pl.BlockSpec((1, d), const),
            _mod_spec(d, 3, tm, rows_per_batch, n_batch),
            _mod_spec(d, 4, tm, rows_per_batch, n_batch),
        ],
        out_specs=[pl.BlockSpec((tm, d), row), pl.BlockSpec((tm, d), row)],
        out_shape=[jax.ShapeDtypeStruct((n_rows, d), F32), jax.ShapeDtypeStruct((n_rows, d), BF16)],
        scratch_shapes=[pltpu.VMEM((tm, d), BF16)],
        compiler_params=_params("parallel"),
        name="out_proj",
    )(o, proj, norm_gain.reshape(1, d), w, x, mod, gain2.reshape(1, d), mod, mod)


def _gateup_kernel(h_ref, wg_ref, wu_ref, p_ref):
    h = h_ref[...]
    for t in range(FFN_HID_TILE // MXU_COLS):
        cols = slice(t * MXU_COLS, (t + 1) * MXU_COLS)
        a = _dot(h, wg_ref[:, cols])
        b = _dot(h, wu_ref[:, cols])
        p_ref[:, cols] = (a * _sigmoid(a) * b).astype(BF16)


def _down_kernel(p_ref, wd_ref, x_ref, gt_ref, o_ref):
    o_ref[...] = x_ref[...] + gt_ref[...] * _dot(p_ref[...], wd_ref[...])


def _ffn(h2, w_gate_up, w_down, x, mod, rows_per_batch, n_batch, n_rows):
    d = x.shape[1]
    hidden = w_down.shape[0]
    tm = BIG_ROW_TILE
    n_hid = hidden // FFN_HID_TILE
    p = pl.pallas_call(
        _gateup_kernel,
        grid=(n_rows // tm, n_hid),
        in_specs=[
            pl.BlockSpec((tm, d), lambda i, j: (i, 0)),
            pl.BlockSpec((d, FFN_HID_TILE), lambda i, j: (0, j)),
            pl.BlockSpec((d, FFN_HID_TILE), lambda i, j: (0, j + n_hid)),
        ],
        out_specs=pl.BlockSpec((tm, FFN_HID_TILE), lambda i, j: (i, j)),
        out_shape=jax.ShapeDtypeStruct((n_rows, hidden), BF16),
        compiler_params=_params("parallel", "arbitrary"),
        name="ffn_gate_up",
    )(h2, w_gate_up, w_gate_up)
    tn = FFN_OUT_TILE
    return pl.pallas_call(
        _down_kernel,
        grid=(n_rows // tm, d // tn),
        in_specs=[
            pl.BlockSpec((tm, hidden), lambda i, j: (i, 0)),
            pl.BlockSpec((hidden, tn), lambda i, j: (0, j)),
            pl.BlockSpec((tm, tn), lambda i, j: (i, j)),
            _mod_spec(tn, 5 * (d // tn), tm, rows_per_batch, n_batch, col_axis=1),
        ],
        out_specs=pl.BlockSpec((tm, tn), lambda i, j: (i, j)),
        out_shape=jax.ShapeDtypeStruct((n_rows, d), F32),
        compiler_params=_params("parallel", "arbitrary"),
        name="ffn_down",
    )(p, w_down, x, mod)


def _final_norm_kernel(x_ref, g_ref, o_ref):
    def body(rows):
        x = x_ref[rows, :]
        o_ref[rows, :] = x * lax.rsqrt(jnp.mean(x * x, axis=-1, keepdims=True) + EPS) * g_ref[...]
    _row_loop(ROW_TILE, body)


def _final_norm(x, gain):
    r, d = x.shape
    return pl.pallas_call(
        _final_norm_kernel,
        grid=(r // ROW_TILE,),
        in_specs=[pl.BlockSpec((ROW_TILE, d), lambda i: (i, 0)), pl.BlockSpec((1, d), lambda i: (0, 0))],
        out_specs=pl.BlockSpec((ROW_TILE, d), lambda i: (i, 0)),
        out_shape=jax.ShapeDtypeStruct((r, d), F32),
        compiler_params=_params("parallel"),
        name="final_norm",
    )(x, gain.reshape(1, d))


def _ret_kernel(lg_ref, q_ref, k_ref, v_ref, qc_ref, kc_ref, vc_ref, cos_ref, sin_ref,
                o_ref, oc_ref, dec_ref, kr_ref, sbs_ref, sf_ref, sb_ref, *, k_scale):
    c = RET_CHUNK
    dk = q_ref.shape[1]
    half = dk // 2
    n_chunks = q_ref.shape[0] // c
    head = pl.program_id(1)
    lgf = lg_ref[0, head]
    lgb = lg_ref[1, head]

    n_i = lax.broadcasted_iota(jnp.int32, (c, c), 0).astype(F32)
    m_i = lax.broadcasted_iota(jnp.int32, (c, c), 1).astype(F32)
    diff = n_i - m_i
    dec_ref[0] = (jnp.where(diff >= 0, jnp.exp(lgf * jnp.maximum(diff, 0.0)), 0.0)
                  + jnp.where(diff <= 0, jnp.exp(lgb * jnp.maximum(-diff, 0.0)), 0.0))
    t_i = lax.broadcasted_iota(jnp.int32, (c, dk), 0).astype(F32)
    dec_ref[1] = jnp.exp(lgf * (t_i + 1.0))
    dec_ref[2] = jnp.exp(lgb * (c - t_i))
    dec_ref[3] = jnp.exp(lgf * (c - 1.0 - t_i))
    dec_ref[4] = jnp.exp(lgb * t_i)
    cf = jnp.exp(lgf * c)
    cb = jnp.exp(lgb * c)

    def head_norm(o):
        return o * lax.rsqrt(jnp.mean(o * o, axis=-1, keepdims=True) + EPS)

    def rope(t, rows):
        cos = cos_ref[rows, :]
        sin = sin_ref[rows, :]
        t1 = t[:, :half]
        t2 = t[:, half:]
        return jnp.concatenate([t1 * cos - t2 * sin, t1 * sin + t2 * cos], axis=-1)

    qc = qc_ref[...]
    kc = (kc_ref[...].astype(F32) * k_scale)
    vc = vc_ref[...]
    sc = _dot_nt(qc, kc.astype(BF16)) * dec_ref[0]
    oc_ref[...] = head_norm(_dot(sc.astype(BF16), vc)).astype(oc_ref.dtype)
    sf_ref[...] = _dot_tn((kc * dec_ref[3]).astype(BF16), vc)
    sb_ref[...] = _dot_tn((kc * dec_ref[4]).astype(BF16), vc)

    def bwd(i, carry):
        ci = n_chunks - 1 - i
        rows = pl.ds(pl.multiple_of(ci * c, c), c)
        kr = rope(k_ref[rows, :].astype(F32), rows) * k_scale
        kr_ref[rows, :] = kr.astype(BF16)
        s = sb_ref[...]
        sbs_ref[ci] = s.astype(BF16)
        sb_ref[...] = s * cb + _dot_tn((kr * dec_ref[4]).astype(BF16), v_ref[rows, :])
        return carry
    lax.fori_loop(0, n_chunks, bwd, 0)

    def fwd(ci, carry):
        rows = pl.ds(pl.multiple_of(ci * c, c), c)
        qr = rope(q_ref[rows, :].astype(F32), rows).astype(BF16)
        kr = kr_ref[rows, :]
        v = v_ref[rows, :]
        s = sf_ref[...]
        sc = _dot_nt(qr, kr) * dec_ref[0]
        o = _dot(sc.astype(BF16), v)
        o += dec_ref[1] * _dot(qr, s.astype(BF16))
        o += dec_ref[2] * _dot(qr, sbs_ref[ci])
        sf_ref[...] = s * cf + _dot_tn((kr.astype(F32) * dec_ref[3]).astype(BF16), v)
        o_ref[rows, :] = head_norm(o).astype(o_ref.dtype)
        return carry
    lax.fori_loop(0, n_chunks, fwd, 0)


def _retention(proj, log_gamma, cos, sin, n_batch, seq, ctx_len):
    heads = RET_HEADS
    dk = proj.shape[1] // (4 * heads)
    d = heads * dk
    assert dk == RET_CHUNK and ctx_len == RET_CHUNK and seq % RET_CHUNK == 0
    ctx_row0 = (n_batch * seq) // ctx_len
    lat = lambda part: pl.BlockSpec((seq, dk), lambda b, h: (b, part * heads + h))
    ctx = lambda part: pl.BlockSpec((ctx_len, dk), lambda b, h: (ctx_row0 + b, part * heads + h))
    tab = pl.BlockSpec((seq, dk // 2), lambda b, h: (0, 0))
    n_chunks = seq // RET_CHUNK
    return pl.pallas_call(
        functools.partial(_ret_kernel, k_scale=dk ** -0.5),
        grid=(n_batch, heads),
        in_specs=[pl.BlockSpec(memory_space=pltpu.SMEM),
                  lat(0), lat(1), lat(2), ctx(0), ctx(1), ctx(2), tab, tab],
        out_specs=[pl.BlockSpec((seq, dk), lambda b, h: (b, h)),
                   pl.BlockSpec((ctx_len, dk), lambda b, h: (b, h))],
        out_shape=[jax.ShapeDtypeStruct((n_batch * seq, d), BF16),
                   jax.ShapeDtypeStruct((n_batch * ctx_len, d), BF16)],
        scratch_shapes=[
            pltpu.VMEM((5, RET_CHUNK, RET_CHUNK), F32),
            pltpu.VMEM((seq, dk), BF16),
            pltpu.VMEM((n_chunks, dk, dk), BF16),
            pltpu.VMEM((dk, dk), F32),
            pltpu.VMEM((dk, dk), F32),
        ],
        compiler_params=_params("parallel", "arbitrary"),
        name="retention",
    )(log_gamma, proj, proj, proj, proj, proj, proj, cos, sin)


def _hg_gates(z, lb):
    en = jnp.exp(-jnp.abs(z))
    r = 1.0 / (1.0 + en)
    pos = z >= 0
    f = lb + (1.0 - lb) * jnp.where(pos, r, en * r)
    key = (1.0 - lb) * jnp.where(pos, en * r, r)
    return f, jnp.log(f) * LOG2E, key


def _hg_cumsum(g):
    sub = lax.broadcasted_iota(jnp.int32, g.shape, 0) & (V7X_SUBLANES - 1)
    x = g
    for s in (1, 2, 4):
        x = x + jnp.where(sub >= s, pltpu.roll(x, s, axis=0), 0.0)
    blocks, lasts = [], []
    carry = None
    for j in range(g.shape[0] // V7X_SUBLANES):
        blk = x[j * V7X_SUBLANES:(j + 1) * V7X_SUBLANES]
        if carry is not None:
            blk = blk + carry
        carry = jnp.broadcast_to(blk[V7X_SUBLANES - 1:, :], blk.shape)
        blocks.append(blk)
        lasts.append(carry)
    return blocks, lasts


def _tiles(x):
    return [x[j * V7X_SUBLANES:(j + 1) * V7X_SUBLANES] for j in range(x.shape[0] // V7X_SUBLANES)]


def _hg_levels(up, lo, cv, cin, lasts, pair_even, pair_odd):
    nt = len(cv)
    sub = lax.broadcasted_iota(jnp.int32, cv[0].shape, 0)
    out = []
    bt = nt
    while bt >= 2:
        ht = bt // 2
        rows = []
        for j in range(nt):
            b0 = (j // bt) * bt
            ref = lasts[b0 + ht - 1]
            if j - b0 >= ht:
                rows.append(jnp.exp2(cv[j] - ref) * up[j])
            else:
                rows.append(jnp.exp2(ref - cv[j]) * lo[j])
        out.append(jnp.concatenate(rows, axis=0).astype(BF16))
        bt = ht
    for size in (8, 4):
        upper = (sub & (size // 2)) != 0
        sign = jnp.where(upper, 1.0, -1.0)
        rows = []
        for j in range(nt):
            if size == 8:
                ref = jnp.broadcast_to(cin[j][3:4, :], cin[j].shape)
            else:
                ref = jnp.where(sub < 4, jnp.broadcast_to(cin[j][1:2, :], cin[j].shape),
                                jnp.broadcast_to(cin[j][5:6, :], cin[j].shape))
            rows.append(jnp.exp2((cv[j] - ref) * sign) * jnp.where(upper, up[j], lo[j]))
        out.append(jnp.concatenate(rows, axis=0).astype(BF16))
    odd = (sub & 1) != 0
    rows = [jnp.where(odd, pair_odd[j], pair_even[j]) for j in range(nt)]
    out.append(jnp.concatenate(rows, axis=0).astype(BF16))
    return out


def _hg_chunk(forward, rows, q_ref, z_ref, v_ref, lb, s_ref, pair_xor, triangle):
    qraw = q_ref[rows, :].astype(F32)
    q = qraw * _sigmoid(qraw)
    v = v_ref[rows, :]
    f, g, key = _hg_gates(z_ref[rows, :].astype(F32), lb)
    cin, lasts = _hg_cumsum(g)
    cum = jnp.concatenate(cin, axis=0)
    tot = lasts[-1][0:1, :]
    qt, kt, qft = _tiles(q), _tiles(key), _tiles(q * f)
    if forward:
        levels = _hg_levels(qt, kt, cin, cin, lasts, kt, qft)
        q_dec = jnp.exp2(cum)
        k_dec = jnp.exp2(tot - cum)
    else:
        cumx = cum - g
        levels = _hg_levels(kt, qt, _tiles(cumx), cin, lasts, qft, kt)
        q_dec = jnp.exp2(tot - cumx)
        k_dec = jnp.exp2(cumx)
    a = None
    size = q.shape[0]
    for z_l in levels:
        p = _dot_nt(z_l, z_l)
        a = p if a is None else jnp.where(pair_xor < size, p, a)
        size //= 2
    a = jnp.where(triangle, a, 0.0)
    s = s_ref[...]
    o = _dot(a.astype(BF16), v)
    o += jnp.sum(q * key, axis=-1, keepdims=True) * v.astype(F32)
    o += _dot_nt((q * q_dec).astype(BF16), s.astype(BF16))
    s_ref[...] = s * jnp.exp2(tot) + _dot_tn(v, (key * k_dec).astype(BF16))
    return o


def _hg_kernel(lb_ref, q_ref, zf_ref, zb_ref, v_ref, qc_ref, zfc_ref, zbc_ref, vc_ref,
               o_ref, oc_ref, of_ref, ob_ref, sf_ref, sb_ref):
    c = HG_CHUNK
    lb = lb_ref[...]
    n_i = lax.broadcasted_iota(jnp.int32, (c, c), 0)
    m_i = lax.broadcasted_iota(jnp.int32, (c, c), 1)
    pair_xor = n_i ^ m_i

    def run(qr_ref, zfr_ref, zbr_ref, vr_ref, out_ref):
        n_chunks = qr_ref.shape[0] // c

        def step(i, carry):
            rf = pl.ds(pl.multiple_of(i * c, c), c)
            rb = pl.ds(pl.multiple_of((n_chunks - 1 - i) * c, c), c)
            of_ref[rf, :] = _hg_chunk(True, rf, qr_ref, zfr_ref, vr_ref, lb, sf_ref, pair_xor, n_i > m_i)
            ob_ref[rb, :] = _hg_chunk(False, rb, qr_ref, zbr_ref, vr_ref, lb, sb_ref, pair_xor, n_i < m_i)
            return carry
        lax.fori_loop(0, n_chunks, step, 0)

        def combine(i, carry):
            rows = pl.ds(pl.multiple_of(i * c, c), c)
            out_ref[rows, :] = (of_ref[rows, :] + ob_ref[rows, :]).astype(out_ref.dtype)
            return carry
        lax.fori_loop(0, n_chunks, combine, 0)

    sf_ref[...] = jnp.zeros_like(sf_ref)
    sb_ref[...] = jnp.zeros_like(sb_ref)
    run(qc_ref, zfc_ref, zbc_ref, vc_ref, oc_ref)
    run(q_ref, zf_ref, zb_ref, v_ref, o_ref)


def _hgrn(proj, lower_bound, n_batch, seq, ctx_len):
    dk = HG_EXPAND
    heads = proj.shape[1] // (5 * dk)
    d = heads * dk
    assert seq % HG_CHUNK == 0 and ctx_len % HG_CHUNK == 0 and ctx_len <= seq
    ctx_row0 = (n_batch * seq) // ctx_len
    lat = lambda part: pl.BlockSpec((seq, dk), lambda b, h: (b, part * heads + h))
    ctx = lambda part: pl.BlockSpec((ctx_len, dk), lambda b, h: (ctx_row0 + b, part * heads + h))
    return pl.pallas_call(
        _hg_kernel,
        grid=(n_batch, heads),
        in_specs=[pl.BlockSpec((None, 1, dk), lambda b, h: (h, 0, 0)),
                  lat(0), lat(1), lat(2), lat(3), ctx(0), ctx(1), ctx(2), ctx(3)],
        out_specs=[pl.BlockSpec((seq, dk), lambda b, h: (b, h)),
                   pl.BlockSpec((ctx_len, dk), lambda b, h: (b, h))],
        out_shape=[jax.ShapeDtypeStruct((n_batch * seq, d), BF16),
                   jax.ShapeDtypeStruct((n_batch * ctx_len, d), BF16)],
        scratch_shapes=[
            pltpu.VMEM((seq, dk), F32),
            pltpu.VMEM((seq, dk), F32),
            pltpu.VMEM((dk, dk), F32),
            pltpu.VMEM((dk, dk), F32),
        ],
        compiler_params=_params("parallel", "arbitrary"),
        name="hgrn",
    )(lower_bound.reshape(heads, 1, dk), proj, proj, proj, proj, proj, proj, proj, proj)


def _rope_tables(seq, head_dim):
    quarter = head_dim // 4
    rows = jnp.repeat(jnp.arange(seq // GRID_W, dtype=F32), GRID_W)
    cols = jnp.tile(jnp.arange(GRID_W, dtype=F32), seq // GRID_W)
    inv_freq = ROPE_BASE ** (-jnp.arange(quarter, dtype=F32) / quarter)
    ang = jnp.concatenate([rows[:, None] * inv_freq, cols[:, None] * inv_freq], axis=-1)
    return jnp.cos(ang), jnp.sin(ang)


def kernel(x, c, ctx, c_ctx, ada_w, ada_b, norm1_g, norm2_g, ret_w_in, ret_w_out, ret_decay_logits,
           hg_w_in, hg_w_out, hg_norm_g, hg_lower_bounds, ffn_w_gate_up, ffn_w_down, final_norm_g):
    n_batch, seq, d = x.shape
    ctx_len = ctx.shape[1]
    depth = ada_w.shape[0]
    n_lat = n_batch * seq
    n_all = n_lat + n_batch * ctx_len
    assert seq % BIG_ROW_TILE == 0 and (n_batch * ctx_len) % BIG_ROW_TILE == 0

    xs = jnp.concatenate([x.reshape(n_lat, d), ctx.reshape(n_batch * ctx_len, d)], axis=0)

    cond_rows = -(-(n_batch + 1) // V7X_SUBLANES) * V7X_SUBLANES
    cond = jnp.zeros((cond_rows, d), F32).at[:n_batch].set(c).at[n_batch].set(c_ctx)
    mods = _ada_mod(cond, ada_w, ada_b).reshape(depth, cond_rows, 1, 6 * d)

    lb_p = jax.nn.softmax(hg_lower_bounds.astype(F32), axis=0)
    lower_bounds = jnp.cumsum(lb_p, axis=0) - lb_p[0]
    log_gamma = jax.nn.log_sigmoid(ret_decay_logits.astype(F32))
    cos, sin = _rope_tables(seq, d // RET_HEADS)

    for layer in range(depth):
        last = layer == depth - 1
        j = layer // N_MIXERS
        mod = mods[layer]
        n_rows = n_lat if last else n_all
        retention = layer % N_MIXERS == 0
        w_in = (ret_w_in if retention else hg_w_in)[j].astype(BF16)
        w_out = (ret_w_out if retention else hg_w_out)[j].astype(BF16)
        proj = _in_proj(xs, norm1_g[layer], mod, w_in, seq, n_batch)
        if retention:
            o_lat, o_ctx = _retention(proj, log_gamma[j], cos, sin, n_batch, seq, ctx_len)
            norm_gain = jnp.ones((d,), F32)
        else:
            o_lat, o_ctx = _hgrn(proj, lower_bounds[j], n_batch, seq, ctx_len)
            norm_gain = hg_norm_g[j]
        o = o_lat if last else jnp.concatenate([o_lat, o_ctx], axis=0)
        gate_block = proj.shape[1] // d - 1
        xs, h2 = _out_proj(o, proj, gate_block, norm_gain, w_out, xs, mod, norm2_g[layer],
                           seq, n_batch, n_rows, not retention)
        xs = _ffn(h2, ffn_w_gate_up[layer].astype(BF16), ffn_w_down[layer].astype(BF16), xs, mod,
                  seq, n_batch, n_rows)
    return _final_norm(xs, final_norm_g).reshape(n_batch, seq, d)
```

```python
import functools
import math

import jax
import jax.numpy as jnp
from jax import lax
from jax.experimental import pallas as pl
from jax.experimental.pallas import tpu as pltpu

F32 = jnp.float32
BF16 = jnp.bfloat16

EPS = 1e-6
LOG2E = 1.4426950408889634
ROPE_BASE = 10000.0
GRID_W = 64
N_MIXERS = 2
RET_HEADS = 8
HG_EXPAND = 128

V7X_LANES = 128
V7X_SUBLANES = 8
V7X_VMEM_BYTES = 64 * 1024 * 1024

ROW_TILE = 512
BIG_ROW_TILE = 1024
IN_COL_TILE = 1024
FFN_HID_TILE = 512
FFN_OUT_TILE = 512
MXU_COLS = 256
ADA_COL_TILE = 1024
RET_CHUNK = 256
HG_CHUNK = 128
HG_UNROLL = 4
ROW_STEP = 64
VMEM_LIMIT = 56 * 1024 * 1024


def _params(*semantics):
    return pltpu.CompilerParams(dimension_semantics=semantics, vmem_limit_bytes=VMEM_LIMIT)


def _sigmoid(x):
    return 1.0 / (1.0 + jnp.exp(-x))


def _dot(a, b):
    return jnp.dot(a, b, preferred_element_type=F32)


def _dot_nt(a, b):
    return lax.dot_general(a, b, (((1,), (1,)), ((), ())), preferred_element_type=F32)


def _dot_tn(a, b):
    return lax.dot_general(a, b, (((0,), (0,)), ((), ())), preferred_element_type=F32)


def _ada_kernel(c_ref, w_ref, b_ref, o_ref):
    c = c_ref[...]
    a = (c * _sigmoid(c)).astype(BF16)
    o_ref[...] = _dot(a, w_ref[...].astype(BF16)) + b_ref[...]


def _ada_mod(cond, ada_w, ada_b):
    depth, d, n = ada_w.shape
    rows = cond.shape[0]
    return pl.pallas_call(
        _ada_kernel,
        grid=(depth, n // ADA_COL_TILE),
        in_specs=[
            pl.BlockSpec((rows, d), lambda l, j: (0, 0)),
            pl.BlockSpec((None, d, ADA_COL_TILE), lambda l, j: (l, 0, j)),
            pl.BlockSpec((None, 1, ADA_COL_TILE), lambda l, j: (l, 0, j)),
        ],
        out_specs=pl.BlockSpec((None, rows, ADA_COL_TILE), lambda l, j: (l, 0, j)),
        out_shape=jax.ShapeDtypeStruct((depth, rows, n), F32),
        compiler_params=_params("parallel", "parallel"),
        name="ada_mod",
    )(cond, ada_w, ada_b.reshape(depth, 1, n))


def _norm_mod(x, gain, shift, scale):
    y = x * lax.rsqrt(jnp.mean(x * x, axis=-1, keepdims=True) + EPS) * gain
    return y * (1.0 + scale) + shift


def _row_loop(rows, body):
    def step(i, carry):
        body(pl.ds(pl.multiple_of(i * ROW_STEP, ROW_STEP), ROW_STEP))
        return carry
    lax.fori_loop(0, rows // ROW_STEP, step, 0)


def _mod_spec(width, slot, row_tile, rows_per_batch, n_batch, col_axis=None):
    def index(*ids):
        col = slot if col_axis is None else slot + ids[col_axis]
        return (jnp.minimum((ids[0] * row_tile) // rows_per_batch, n_batch), 0, col)
    return pl.BlockSpec((None, 1, width), index)


def _prenorm_kernel(x_ref, g_ref, sh_ref, sc_ref, h_ref):
    def body(rows):
        h = _norm_mod(x_ref[rows, :], g_ref[...], sh_ref[...], sc_ref[...])
        h_ref[rows, :] = h.astype(BF16)
    _row_loop(ROW_TILE, body)


def _prenorm(x, gain, mod, rows_per_batch, n_batch):
    r, d = x.shape
    tm = ROW_TILE
    return pl.pallas_call(
        _prenorm_kernel,
        grid=(r // tm,),
        in_specs=[
            pl.BlockSpec((tm, d), lambda i: (i, 0)),
            pl.BlockSpec((1, d), lambda i: (0, 0)),
            _mod_spec(d, 0, tm, rows_per_batch, n_batch),
            _mod_spec(d, 1, tm, rows_per_batch, n_batch),
        ],
        out_specs=pl.BlockSpec((tm, d), lambda i: (i, 0)),
        out_shape=jax.ShapeDtypeStruct((r, d), BF16),
        compiler_params=_params("parallel"),
        name="prenorm",
    )(x, gain.reshape(1, d), mod, mod)


def _col_tiles():
    return [slice(t * MXU_COLS, (t + 1) * MXU_COLS) for t in range(IN_COL_TILE // MXU_COLS)]


def _proj_kernel(h_ref, w_ref, o_ref, *, n_silu):
    h = h_ref[...]

    def run(silu):
        for cols in _col_tiles():
            a = _dot(h, w_ref[:, cols])
            if silu:
                a = a * _sigmoid(a)
            o_ref[:, cols] = a.astype(o_ref.dtype)

    if n_silu == 0:
        run(False)
    else:
        j = pl.program_id(1)
        pl.when(j < n_silu)(functools.partial(run, True))
        pl.when(j >= n_silu)(functools.partial(run, False))


def _proj(h, w, col_tiles, n_silu):
    r, d = h.shape
    tm, tn = BIG_ROW_TILE, IN_COL_TILE
    first, skip_from, skip = col_tiles
    n_tiles = w.shape[1] // tn - first - skip
    wcol = lambda i, j: (0, first + j + jnp.where(j >= skip_from, skip, 0))
    return pl.pallas_call(
        functools.partial(_proj_kernel, n_silu=n_silu),
        grid=(r // tm, n_tiles),
        in_specs=[pl.BlockSpec((tm, d), lambda i, j: (i, 0)), pl.BlockSpec((d, tn), wcol)],
        out_specs=pl.BlockSpec((tm, tn), lambda i, j: (i, j)),
        out_shape=jax.ShapeDtypeStruct((r, n_tiles * tn), BF16),
        compiler_params=_params("parallel", "arbitrary"),
        name="in_proj",
    )(h, w)


def _proj_gates_kernel(h_ref, w_ref, lb_ref, g_ref, key_ref):
    h = h_ref[...]
    for cols in _col_tiles():
        _, g, key = _hg_gates(_dot(h, w_ref[:, cols]), lb_ref[:, cols])
        g_ref[:, cols] = g
        key_ref[:, cols] = key.astype(key_ref.dtype)


def _proj_gates(h, w, lower_bound, first_tile, n_tiles):
    r, d = h.shape
    tm, tn = BIG_ROW_TILE, IN_COL_TILE
    lb_tiles = lower_bound.shape[0] // tn
    out = pl.BlockSpec((tm, tn), lambda i, j: (i, j))
    return pl.pallas_call(
        _proj_gates_kernel,
        grid=(r // tm, n_tiles),
        in_specs=[pl.BlockSpec((tm, d), lambda i, j: (i, 0)),
                  pl.BlockSpec((d, tn), lambda i, j: (0, first_tile + j)),
                  pl.BlockSpec((1, tn), lambda i, j: (0, j % lb_tiles))],
        out_specs=[out, out],
        out_shape=[jax.ShapeDtypeStruct((r, n_tiles * tn), F32),
                   jax.ShapeDtypeStruct((r, n_tiles * tn), BF16)],
        compiler_params=_params("parallel", "arbitrary"),
        name="in_proj_gates",
    )(h, w, lower_bound.reshape(1, -1))


def _outproj_kernel(o_ref, g_ref, ng_ref, w_ref, x_ref, gt_ref, n2_ref, sh_ref, sc_ref,
                    xo_ref, h_ref, y_ref, *, hgrn):
    def gate(rows):
        o = o_ref[rows, :].astype(F32)
        g = g_ref[rows, :].astype(F32)
        if hgrn:
            o = o * lax.rsqrt(jnp.mean(o * o, axis=-1, keepdims=True) + EPS) * ng_ref[...]
            y = o * _sigmoid(g)
        else:
            y = o * (g * _sigmoid(g))
        y_ref[rows, :] = y.astype(BF16)
    _row_loop(ROW_TILE, gate)

    xo_ref[...] = _dot(y_ref[...], w_ref[...])

    def resid(rows):
        xn = x_ref[rows, :] + gt_ref[...] * xo_ref[rows, :]
        xo_ref[rows, :] = xn
        h_ref[rows, :] = _norm_mod(xn, n2_ref[...], sh_ref[...], sc_ref[...]).astype(BF16)
    _row_loop(ROW_TILE, resid)


def _out_proj(o, proj, gate_block, norm_gain, w, x, mod, gain2, rows_per_batch, n_batch, n_rows, hgrn):
    d = x.shape[1]
    tm = ROW_TILE
    row = lambda i: (i, 0)
    const = lambda i: (0, 0)
    return pl.pallas_call(
        functools.partial(_outproj_kernel, hgrn=hgrn),
        grid=(n_rows // tm,),
        in_specs=[
            pl.BlockSpec((tm, d), row),
            pl.BlockSpec((tm, d), lambda i: (i, gate_block)),
            pl.BlockSpec((1, d), const),
            pl.BlockSpec((d, d), const),
            pl.BlockSpec((tm, d), row),
            _mod_spec(d, 2, tm, rows_per_batch, n_batch),
            pl.BlockSpec((1, d), const),
            _mod_spec(d, 3, tm, rows_per_batch, n_batch),
            _mod_spec(d, 4, tm, rows_per_batch, n_batch),
        ],
        out_specs=[pl.BlockSpec((tm, d), row), pl.BlockSpec((tm, d), row)],
        out_shape=[jax.ShapeDtypeStruct((n_rows, d), F32), jax.ShapeDtypeStruct((n_rows, d), BF16)],
        scratch_shapes=[pltpu.VMEM((tm, d), BF16)],
        compiler_params=_params("parallel"),
        name="out_proj",
    )(o, proj, norm_gain.reshape(1, d), w, x, mod, gain2.reshape(1, d), mod, mod)


def _gateup_kernel(h_ref, wg_ref, wu_ref, p_ref):
    h = h_ref[...]
    for t in range(FFN_HID_TILE // MXU_COLS):
        cols = slice(t * MXU_COLS, (t + 1) * MXU_COLS)
        a = _dot(h, wg_ref[:, cols])
        b = _dot(h, wu_ref[:, cols])
        p_ref[:, cols] = (a * _sigmoid(a) * b).astype(BF16)


def _down_kernel(p_ref, wd_ref, x_ref, gt_ref, o_ref):
    o_ref[...] = x_ref[...] + gt_ref[...] * _dot(p_ref[...], wd_ref[...])


def _ffn(h2, w_gate_up, w_down, x, mod, rows_per_batch, n_batch, n_rows):
    d = x.shape[1]
    hidden = w_down.shape[0]
    tm = BIG_ROW_TILE
    n_hid = hidden // FFN_HID_TILE
    p = pl.pallas_call(
        _gateup_kernel,
        grid=(n_rows // tm, n_hid),
        in_specs=[
            pl.BlockSpec((tm, d), lambda i, j: (i, 0)),
            pl.BlockSpec((d, FFN_HID_TILE), lambda i, j: (0, j)),
            pl.BlockSpec((d, FFN_HID_TILE), lambda i, j: (0, j + n_hid)),
        ],
        out_specs=pl.BlockSpec((tm, FFN_HID_TILE), lambda i, j: (i, j)),
        out_shape=jax.ShapeDtypeStruct((n_rows, hidden), BF16),
        compiler_params=_params("parallel", "arbitrary"),
        name="ffn_gate_up",
    )(h2, w_gate_up, w_gate_up)
    tn = FFN_OUT_TILE
    return pl.pallas_call(
        _down_kernel,
        grid=(n_rows // tm, d // tn),
        in_specs=[
            pl.BlockSpec((tm, hidden), lambda i, j: (i, 0)),
            pl.BlockSpec((hidden, tn), lambda i, j: (0, j)),
            pl.BlockSpec((tm, tn), lambda i, j: (i, j)),
            _mod_spec(tn, 5 * (d // tn), tm, rows_per_batch, n_batch, col_axis=1),
        ],
        out_specs=pl.BlockSpec((tm, tn), lambda i, j: (i, j)),
        out_shape=jax.ShapeDtypeStruct((n_rows, d), F32),
        compiler_params=_params("parallel", "arbitrary"),
        name="ffn_down",
    )(p, w_down, x, mod)


def _final_norm_kernel(x_ref, g_ref, o_ref):
    def body(rows):
        x = x_ref[rows, :]
        o_ref[rows, :] = x * lax.rsqrt(jnp.mean(x * x, axis=-1, keepdims=True) + EPS) * g_ref[...]
    _row_loop(ROW_TILE, body)


def _final_norm(x, gain):
    r, d = x.shape
    return pl.pallas_call(
        _final_norm_kernel,
        grid=(r // ROW_TILE,),
        in_specs=[pl.BlockSpec((ROW_TILE, d), lambda i: (i, 0)), pl.BlockSpec((1, d), lambda i: (0, 0))],
        out_specs=pl.BlockSpec((ROW_TILE, d), lambda i: (i, 0)),
        out_shape=jax.ShapeDtypeStruct((r, d), F32),
        compiler_params=_params("parallel"),
        name="final_norm",
    )(x, gain.reshape(1, d))


def _ret_kernel(lg_ref, q_ref, k_ref, v_ref, qc_ref, kc_ref, vc_ref, cos_ref, sin_ref,
                o_ref, oc_ref, dec_ref, kr_ref, sbs_ref, sf_ref, sb_ref, *, k_scale):
    c = RET_CHUNK
    dk = q_ref.shape[1]
    half = dk // 2
    n_chunks = q_ref.shape[0] // c
    head = pl.program_id(1)
    lgf = lg_ref[0, head]
    lgb = lg_ref[1, head]

    n_i = lax.broadcasted_iota(jnp.int32, (c, c), 0).astype(F32)
    m_i = lax.broadcasted_iota(jnp.int32, (c, c), 1).astype(F32)
    diff = n_i - m_i
    dec_ref[0] = (jnp.where(diff >= 0, jnp.exp(lgf * jnp.maximum(diff, 0.0)), 0.0)
                  + jnp.where(diff <= 0, jnp.exp(lgb * jnp.maximum(-diff, 0.0)), 0.0))
    t_i = lax.broadcasted_iota(jnp.int32, (c, dk), 0).astype(F32)
    dec_ref[1] = jnp.exp(lgf * (t_i + 1.0))
    dec_ref[2] = jnp.exp(lgb * (c - t_i))
    dec_ref[3] = jnp.exp(lgf * (c - 1.0 - t_i))
    dec_ref[4] = jnp.exp(lgb * t_i)
    cf = jnp.exp(lgf * c)
    cb = jnp.exp(lgb * c)

    def head_norm(o):
        return o * lax.rsqrt(jnp.mean(o * o, axis=-1, keepdims=True) + EPS)

    def rope(t, rows):
        cos = cos_ref[rows, :]
        sin = sin_ref[rows, :]
        t1 = t[:, :half]
        t2 = t[:, half:]
        return jnp.concatenate([t1 * cos - t2 * sin, t1 * sin + t2 * cos], axis=-1)

    qc = qc_ref[...]
    kc = (kc_ref[...].astype(F32) * k_scale)
    vc = vc_ref[...]
    sc = _dot_nt(qc, kc.astype(BF16)) * dec_ref[0]
    oc_ref[...] = head_norm(_dot(sc.astype(BF16), vc)).astype(oc_ref.dtype)
    sf_ref[...] = _dot_tn((kc * dec_ref[3]).astype(BF16), vc)
    sb_ref[...] = _dot_tn((kc * dec_ref[4]).astype(BF16), vc)

    def bwd(i, carry):
        ci = n_chunks - 1 - i
        rows = pl.ds(pl.multiple_of(ci * c, c), c)
        kr = rope(k_ref[rows, :].astype(F32), rows) * k_scale
        kr_ref[rows, :] = kr.astype(BF16)
        s = sb_ref[...]
        sbs_ref[ci] = s.astype(BF16)
        sb_ref[...] = s * cb + _dot_tn((kr * dec_ref[4]).astype(BF16), v_ref[rows, :])
        return carry
    lax.fori_loop(0, n_chunks, bwd, 0)

    def fwd(ci, carry):
        rows = pl.ds(pl.multiple_of(ci * c, c), c)
        qr = rope(q_ref[rows, :].astype(F32), rows).astype(BF16)
        kr = kr_ref[rows, :]
        v = v_ref[rows, :]
        s = sf_ref[...]
        sc = _dot_nt(qr, kr) * dec_ref[0]
        o = _dot(sc.astype(BF16), v)
        o += dec_ref[1] * _dot(qr, s.astype(BF16))
        o += dec_ref[2] * _dot(qr, sbs_ref[ci])
        sf_ref[...] = s * cf + _dot_tn((kr.astype(F32) * dec_ref[3]).astype(BF16), v)
        o_ref[rows, :] = head_norm(o).astype(o_ref.dtype)
        return carry
    lax.fori_loop(0, n_chunks, fwd, 0)


def _retention(proj, log_gamma, cos, sin, n_batch, seq, ctx_len):
    heads = RET_HEADS
    dk = proj.shape[1] // (4 * heads)
    d = heads * dk
    assert dk == RET_CHUNK and ctx_len == RET_CHUNK and seq % RET_CHUNK == 0
    ctx_row0 = (n_batch * seq) // ctx_len
    lat = lambda part: pl.BlockSpec((seq, dk), lambda b, h: (b, part * heads + h))
    ctx = lambda part: pl.BlockSpec((ctx_len, dk), lambda b, h: (ctx_row0 + b, part * heads + h))
    tab = pl.BlockSpec((seq, dk // 2), lambda b, h: (0, 0))
    n_chunks = seq // RET_CHUNK
    return pl.pallas_call(
        functools.partial(_ret_kernel, k_scale=dk ** -0.5),
        grid=(n_batch, heads),
        in_specs=[pl.BlockSpec(memory_space=pltpu.SMEM),
                  lat(0), lat(1), lat(2), ctx(0), ctx(1), ctx(2), tab, tab],
        out_specs=[pl.BlockSpec((seq, dk), lambda b, h: (b, h)),
                   pl.BlockSpec((ctx_len, dk), lambda b, h: (b, h))],
        out_shape=[jax.ShapeDtypeStruct((n_batch * seq, d), BF16),
                   jax.ShapeDtypeStruct((n_batch * ctx_len, d), BF16)],
        scratch_shapes=[
            pltpu.VMEM((5, RET_CHUNK, RET_CHUNK), F32),
            pltpu.VMEM((seq, dk), BF16),
            pltpu.VMEM((n_chunks, dk, dk), BF16),
            pltpu.VMEM((dk, dk), F32),
            pltpu.VMEM((dk, dk), F32),
        ],
        compiler_params=_params("parallel", "arbitrary"),
        name="retention",
    )(log_gamma, proj, proj, proj, proj, proj, proj, cos, sin)


def _hg_gates(z, lb):
    en = jnp.exp(-jnp.abs(z))
    r = 1.0 / (1.0 + en)
    pos = z >= 0
    f = lb + (1.0 - lb) * jnp.where(pos, r, en * r)
    key = (1.0 - lb) * jnp.where(pos, en * r, r)
    return f, jnp.log(f) * LOG2E, key


def _hg_cumsum(tri_ref, g):
    dk = g.shape[1]
    hi = g.astype(BF16)
    lo = (g - hi.astype(F32)).astype(BF16)
    r = _dot(tri_ref[...], jnp.concatenate([hi, lo], axis=1))
    return r[:, :dk] + r[:, dk:]


def _tiles(x):
    return [x[j * V7X_SUBLANES:(j + 1) * V7X_SUBLANES] for j in range(x.shape[0] // V7X_SUBLANES)]


def _row_of_tile(tile, r):
    return jnp.broadcast_to(tile[r:r + 1, :], tile.shape)


def _hg_levels(up, lo, cv, cin):
    upt, lot, cvt, cint = _tiles(up), _tiles(lo), _tiles(cv), _tiles(cin)
    nt = len(cvt)
    zero = jnp.zeros_like(cvt[0])
    lastt = [_row_of_tile(t, V7X_SUBLANES - 1) for t in cint]
    out = []
    bt = nt
    while bt >= 2:
        ht = bt // 2
        hi_rows, lo_rows = [], []
        for j in range(nt):
            b0 = (j // bt) * bt
            ref = lastt[b0 + ht - 1]
            if j - b0 >= ht:
                hi_rows.append(jnp.exp2(cvt[j] - ref) * upt[j])
                lo_rows.append(zero)
            else:
                hi_rows.append(zero)
                lo_rows.append(jnp.exp2(ref - cvt[j]) * lot[j])
        out.append((jnp.concatenate(hi_rows, axis=0), jnp.concatenate(lo_rows, axis=0)))
        bt = ht
    sub = lax.broadcasted_iota(jnp.int32, zero.shape, 0)
    for size in (8, 4):
        upper = (sub & (size // 2)) != 0
        sign = jnp.where(upper, 1.0, -1.0)
        hi_rows, lo_rows = [], []
        for j in range(nt):
            if size == 8:
                ref = _row_of_tile(cint[j], 3)
            else:
                ref = jnp.where(sub < 4, _row_of_tile(cint[j], 1), _row_of_tile(cint[j], 5))
            z = jnp.exp2((cvt[j] - ref) * sign) * jnp.where(upper, upt[j], lot[j])
            hi_rows.append(jnp.where(upper, z, 0.0))
            lo_rows.append(jnp.where(upper, 0.0, z))
        out.append((jnp.concatenate(hi_rows, axis=0), jnp.concatenate(lo_rows, axis=0)))
    return out


def _hg_bwd_local(rows, ci, g_ref, key_ref, v_ref, tri_ref, cumb_ref, kv_ref):
    g = g_ref[rows, :]
    cum = _hg_cumsum(tri_ref, g)
    cumb_ref[rows, :] = cum
    k_dec = key_ref[rows, :].astype(F32) * jnp.exp2(cum - g)
    kv_ref[ci] = _dot_tn(v_ref[rows, :], k_dec.astype(BF16))


def _hg_step_out(rows_list, cis, q_ref, gf_ref, kf_ref, gb_ref, kb_ref, v_ref, tri_ref, cumb_ref,
                 sbs_ref, sf_ref, out_ref, pair_xor):
    n = len(cis)
    q = [q_ref[r, :].astype(F32) for r in rows_list]
    kf = [kf_ref[r, :].astype(F32) for r in rows_list]
    kb = [kb_ref[r, :].astype(F32) for r in rows_list]
    v = [v_ref[r, :] for r in rows_list]
    gf = [gf_ref[r, :] for r in rows_list]
    gb = [gb_ref[r, :] for r in rows_list]
    c = q[0].shape[0]

    cumf = [_hg_cumsum(tri_ref, g) for g in gf]
    cumb = [cumb_ref[r, :] for r in rows_list]
    cumxb = [cb - g for cb, g in zip(cumb, gb)]
    totf = [cf[c - 1:, :] for cf in cumf]
    totb = [cb[c - 1:, :] for cb in cumb]

    kvs = [_dot_tn(v[u], (kf[u] * jnp.exp2(totf[u] - cumf[u])).astype(BF16)) for u in range(n)]
    sf = [sf_ref[...]]
    for u in range(n):
        sf.append(sf[u] * jnp.exp2(totf[u]) + kvs[u])
    sf_ref[...] = sf[n]

    odd = (lax.broadcasted_iota(jnp.int32, q[0].shape, 0) & 1) != 0
    scores = []
    for u in range(n):
        lev_f = _hg_levels(q[u], kf[u], cumf[u], cumf[u])
        lev_b = _hg_levels(kb[u], q[u], cumxb[u], cumb[u])
        lev_f.append((jnp.where(odd, q[u] * jnp.exp2(gf[u]), 0.0), jnp.where(odd, 0.0, kf[u])))
        lev_b.append((jnp.where(odd, kb[u], 0.0), jnp.where(odd, 0.0, q[u] * jnp.exp2(gb[u]))))
        a = None
        size = c
        for (xf, yf), (yb, xb) in zip(lev_f, lev_b):
            x = jnp.concatenate([xf.astype(BF16), xb.astype(BF16)], axis=1)
            y = jnp.concatenate([yf.astype(BF16), yb.astype(BF16)], axis=1)
            p = _dot_nt(x, y)
            a = p if a is None else jnp.where(pair_xor < size, p, a)
            size //= 2
        scores.append(a.astype(BF16))

    inter = []
    for u in range(n):
        q_dec = jnp.concatenate([(q[u] * jnp.exp2(cumf[u])).astype(BF16),
                                 (q[u] * jnp.exp2(totb[u] - cumxb[u])).astype(BF16)], axis=1)
        states = jnp.concatenate([sf[u].astype(BF16), sbs_ref[cis[u]].astype(BF16)], axis=1)
        inter.append(_dot_nt(q_dec, states))

    for u in range(n):
        o = _dot(scores[u], v[u]) + inter[u]
        o += jnp.sum(q[u] * (kf[u] + kb[u]), axis=-1, keepdims=True) * v[u].astype(F32)
        out_ref[rows_list[u], :] = o.astype(out_ref.dtype)


def _hg_kernel(q_ref, gf_ref, kf_ref, gb_ref, kb_ref, v_ref,
               qc_ref, gfc_ref, kfc_ref, gbc_ref, kbc_ref, vc_ref,
               o_ref, oc_ref, tri_ref, cumb_ref, sbs_ref, sf_ref, sb_ref):
    c = HG_CHUNK
    n_i = lax.broadcasted_iota(jnp.int32, (c, c), 0)
    m_i = lax.broadcasted_iota(jnp.int32, (c, c), 1)
    pair_xor = n_i ^ m_i
    tri_ref[...] = jnp.where(m_i <= n_i, 1.0, 0.0).astype(BF16)

    def run(qr_ref, gfr_ref, kfr_ref, gbr_ref, kbr_ref, vr_ref, out_ref):
        n_chunks = qr_ref.shape[0] // c
        unroll = math.gcd(HG_UNROLL, n_chunks)
        n_steps = n_chunks // unroll

        def chunk_rows(ci):
            return pl.ds(pl.multiple_of(ci * c, c), c)

        def bwd_local(i, carry):
            cis = [i * unroll + u for u in range(unroll)]
            gs = [gbr_ref[chunk_rows(ci), :] for ci in cis]
            cums = [_hg_cumsum(tri_ref, g) for g in gs]
            k_decs = [(kbr_ref[chunk_rows(ci), :].astype(F32) * jnp.exp2(cum - g)).astype(BF16)
                      for ci, g, cum in zip(cis, gs, cums)]
            kvs = [_dot_tn(vr_ref[chunk_rows(ci), :], k_dec) for ci, k_dec in zip(cis, k_decs)]
            for ci, cum, kv in zip(cis, cums, kvs):
                cumb_ref[chunk_rows(ci), :] = cum
                sbs_ref[ci] = kv
            return carry
        lax.fori_loop(0, n_steps, bwd_local, 0)

        def bwd_state(i, carry):
            ci = n_chunks - 1 - i
            rows = chunk_rows(ci)
            s = sb_ref[...]
            kv = sbs_ref[ci]
            sbs_ref[ci] = s
            last_tile = cumb_ref[pl.ds(pl.multiple_of(ci * c + c - V7X_SUBLANES, V7X_SUBLANES), V7X_SUBLANES), :]
            sb_ref[...] = s * jnp.exp2(last_tile[V7X_SUBLANES - 1:, :]) + kv
            return carry
        lax.fori_loop(0, n_chunks, bwd_state, 0)

        def fwd(i, carry):
            cis = [i * unroll + u for u in range(unroll)]
            _hg_step_out([chunk_rows(ci) for ci in cis], cis, qr_ref, gfr_ref, kfr_ref, gbr_ref, kbr_ref,
                         vr_ref, tri_ref, cumb_ref, sbs_ref, sf_ref, out_ref, pair_xor)
            return carry
        lax.fori_loop(0, n_steps, fwd, 0)

    sf_ref[...] = jnp.zeros_like(sf_ref)
    sb_ref[...] = jnp.zeros_like(sb_ref)
    run(qc_ref, gfc_ref, kfc_ref, gbc_ref, kbc_ref, vc_ref, oc_ref)
    run(q_ref, gf_ref, kf_ref, gb_ref, kb_ref, v_ref, o_ref)


def _hgrn(qiv, log_f, key, n_batch, seq, ctx_len):
    dk = HG_EXPAND
    heads = log_f.shape[1] // (2 * dk)
    d = heads * dk
    assert seq % HG_CHUNK == 0 and ctx_len % HG_CHUNK == 0 and ctx_len <= seq
    ctx_row0 = (n_batch * seq) // ctx_len
    lat = lambda part: pl.BlockSpec((seq, dk), lambda b, h: (b, part * heads + h))
    ctx = lambda part: pl.BlockSpec((ctx_len, dk), lambda b, h: (ctx_row0 + b, part * heads + h))
    return pl.pallas_call(
        _hg_kernel,
        grid=(n_batch, heads),
        in_specs=[lat(0), lat(0), lat(0), lat(1), lat(1), lat(1),
                  ctx(0), ctx(0), ctx(0), ctx(1), ctx(1), ctx(1)],
        out_specs=[pl.BlockSpec((seq, dk), lambda b, h: (b, h)),
                   pl.BlockSpec((ctx_len, dk), lambda b, h: (b, h))],
        out_shape=[jax.ShapeDtypeStruct((n_batch * seq, d), BF16),
                   jax.ShapeDtypeStruct((n_batch * ctx_len, d), BF16)],
        scratch_shapes=[
            pltpu.VMEM((HG_CHUNK, HG_CHUNK), BF16),
            pltpu.VMEM((seq, dk), F32),
            pltpu.VMEM((seq // HG_CHUNK, dk, dk), F32),
            pltpu.VMEM((dk, dk), F32),
            pltpu.VMEM((dk, dk), F32),
        ],
        compiler_params=_params("parallel", "arbitrary"),
        name="hgrn",
    )(qiv, log_f, key, log_f, key, qiv, qiv, log_f, key, log_f, key, qiv)


def _rope_tables(seq, head_dim):
    quarter = head_dim // 4
    rows = jnp.repeat(jnp.arange(seq // GRID_W, dtype=F32), GRID_W)
    cols = jnp.tile(jnp.arange(GRID_W, dtype=F32), seq // GRID_W)
    inv_freq = ROPE_BASE ** (-jnp.arange(quarter, dtype=F32) / quarter)
    ang = jnp.concatenate([rows[:, None] * inv_freq, cols[:, None] * inv_freq], axis=-1)
    return jnp.cos(ang), jnp.sin(ang)


def kernel(x, c, ctx, c_ctx, ada_w, ada_b, norm1_g, norm2_g, ret_w_in, ret_w_out, ret_decay_logits,
           hg_w_in, hg_w_out, hg_norm_g, hg_lower_bounds, ffn_w_gate_up, ffn_w_down, final_norm_g):
    n_batch, seq, d = x.shape
    ctx_len = ctx.shape[1]
    depth = ada_w.shape[0]
    n_lat = n_batch * seq
    n_all = n_lat + n_batch * ctx_len
    assert seq % BIG_ROW_TILE == 0 and (n_batch * ctx_len) % BIG_ROW_TILE == 0

    xs = jnp.concatenate([x.reshape(n_lat, d), ctx.reshape(n_batch * ctx_len, d)], axis=0)

    cond_rows = -(-(n_batch + 1) // V7X_SUBLANES) * V7X_SUBLANES
    cond = jnp.zeros((cond_rows, d), F32).at[:n_batch].set(c).at[n_batch].set(c_ctx)
    mods = _ada_mod(cond, ada_w, ada_b).reshape(depth, cond_rows, 1, 6 * d)

    lb_p = jax.nn.softmax(hg_lower_bounds.astype(F32), axis=0)
    lower_bounds = jnp.cumsum(lb_p, axis=0) - lb_p[0]
    log_gamma = jax.nn.log_sigmoid(ret_decay_logits.astype(F32))
    cos, sin = _rope_tables(seq, d // RET_HEADS)

    for layer in range(depth):
        last = layer == depth - 1
        j = layer // N_MIXERS
        mod = mods[layer]
        n_rows = n_lat if last else n_all
        retention = layer % N_MIXERS == 0
        w_in = (ret_w_in if retention else hg_w_in)[j].astype(BF16)
        w_out = (ret_w_out if retention else hg_w_out)[j].astype(BF16)
        h1 = _prenorm(xs, norm1_g[layer], mod, seq, n_batch)
        tiles_per_part = d // IN_COL_TILE
        if retention:
            proj = _proj(h1, w_in, (0, w_in.shape[1], 0), 0)
            o_lat, o_ctx = _retention(proj, log_gamma[j], cos, sin, n_batch, seq, ctx_len)
            norm_gain = jnp.ones((d,), F32)
        else:
            proj = _proj(h1, w_in, (0, tiles_per_part, 2 * tiles_per_part), tiles_per_part)
            log_f, key = _proj_gates(h1, w_in, lower_bounds[j], tiles_per_part, 2 * tiles_per_part)
            o_lat, o_ctx = _hgrn(proj, log_f, key, n_batch, seq, ctx_len)
            norm_gain = hg_norm_g[j]
        o = o_lat if last else jnp.concatenate([o_lat, o_ctx], axis=0)
        gate_block = proj.shape[1] // d - 1
        xs, h2 = _out_proj(o, proj, gate_block, norm_gain, w_out, xs, mod, norm2_g[layer],
                           seq, n_batch, n_rows, not retention)
        xs = _ffn(h2, ffn_w_gate_up[layer].astype(BF16), ffn_w_down[layer].astype(BF16), xs, mod,
                  seq, n_batch, n_rows)
    return _final_norm(xs, final_norm_g).reshape(n_batch, seq, d)
```

```python
import functools
import math

import jax
import jax.numpy as jnp
from jax import lax
from jax.experimental import pallas as pl
from jax.experimental.pallas import tpu as pltpu

F32 = jnp.float32
BF16 = jnp.bfloat16

EPS = 1e-6
LOG2E = 1.4426950408889634
ROPE_BASE = 10000.0
GRID_W = 64
N_MIXERS = 2
RET_HEADS = 8
HG_EXPAND = 128

V7X_LANES = 128
V7X_SUBLANES = 8
V7X_VMEM_BYTES = 64 * 1024 * 1024

ROW_TILE = 512
BIG_ROW_TILE = 1024
IN_COL_TILE = 1024
FFN_HID_TILE = 512
FFN_OUT_TILE = 512
MXU_COLS = 256
ADA_COL_TILE = 1024
RET_CHUNK = 256
RET_UNROLL = 4
HG_CHUNK = 128
HG_UNROLL = 4
ROW_STEP = 64
NORM_STEP = 128
VMEM_LIMIT = 56 * 1024 * 1024


def _params(*semantics):
    return pltpu.CompilerParams(dimension_semantics=semantics, vmem_limit_bytes=VMEM_LIMIT)


def _sigmoid(x):
    return 1.0 / (1.0 + jnp.exp(-x))


def _dot(a, b):
    return jnp.dot(a, b, preferred_element_type=F32)


def _dot_nt(a, b):
    return lax.dot_general(a, b, (((1,), (1,)), ((), ())), preferred_element_type=F32)


def _dot_tn(a, b):
    return lax.dot_general(a, b, (((0,), (0,)), ((), ())), preferred_element_type=F32)


def _ada_kernel(c_ref, w_ref, b_ref, o_ref):
    c = c_ref[...]
    a = (c * _sigmoid(c)).astype(BF16)
    o_ref[...] = _dot(a, w_ref[...].astype(BF16)) + b_ref[...]


def _ada_mod(cond, ada_w, ada_b):
    depth, d, n = ada_w.shape
    rows = cond.shape[0]
    return pl.pallas_call(
        _ada_kernel,
        grid=(depth, n // ADA_COL_TILE),
        in_specs=[
            pl.BlockSpec((rows, d), lambda l, j: (0, 0)),
            pl.BlockSpec((None, d, ADA_COL_TILE), lambda l, j: (l, 0, j)),
            pl.BlockSpec((None, 1, ADA_COL_TILE), lambda l, j: (l, 0, j)),
        ],
        out_specs=pl.BlockSpec((None, rows, ADA_COL_TILE), lambda l, j: (l, 0, j)),
        out_shape=jax.ShapeDtypeStruct((depth, rows, n), F32),
        compiler_params=_params("parallel", "parallel"),
        name="ada_mod",
    )(cond, ada_w, ada_b.reshape(depth, 1, n))


def _norm_mod(x, gain, shift, scale):
    return x * lax.rsqrt(jnp.mean(x * x, axis=-1, keepdims=True) + EPS) * (gain * (1.0 + scale)) + shift


def _row_loop(rows, body, step_rows=ROW_STEP):
    def step(i, carry):
        body(pl.ds(pl.multiple_of(i * step_rows, step_rows), step_rows))
        return carry
    lax.fori_loop(0, rows // step_rows, step, 0)


def _mod_spec(width, slot, row_tile, rows_per_batch, n_batch, col_axis=None):
    def index(*ids):
        col = slot if col_axis is None else slot + ids[col_axis]
        return (jnp.minimum((ids[0] * row_tile) // rows_per_batch, n_batch), 0, col)
    return pl.BlockSpec((None, 1, width), index)


def _prenorm_kernel(x_ref, g_ref, sh_ref, sc_ref, h_ref):
    def body(rows):
        h = _norm_mod(x_ref[rows, :], g_ref[...], sh_ref[...], sc_ref[...])
        h_ref[rows, :] = h.astype(BF16)
    _row_loop(ROW_TILE, body, NORM_STEP)


def _prenorm(x, gain, mod, rows_per_batch, n_batch):
    r, d = x.shape
    tm = ROW_TILE
    return pl.pallas_call(
        _prenorm_kernel,
        grid=(r // tm,),
        in_specs=[
            pl.BlockSpec((tm, d), lambda i: (i, 0)),
            pl.BlockSpec((1, d), lambda i: (0, 0)),
            _mod_spec(d, 0, tm, rows_per_batch, n_batch),
            _mod_spec(d, 1, tm, rows_per_batch, n_batch),
        ],
        out_specs=pl.BlockSpec((tm, d), lambda i: (i, 0)),
        out_shape=jax.ShapeDtypeStruct((r, d), BF16),
        compiler_params=_params("parallel"),
        name="prenorm",
    )(x, gain.reshape(1, d), mod, mod)


def _col_tiles():
    return [slice(t * MXU_COLS, (t + 1) * MXU_COLS) for t in range(IN_COL_TILE // MXU_COLS)]


def _proj_kernel(h_ref, w_ref, o_ref, *, n_silu):
    h = h_ref[...]

    def run(silu):
        for cols in _col_tiles():
            a = _dot(h, w_ref[:, cols])
            if silu:
                a = a * _sigmoid(a)
            o_ref[:, cols] = a.astype(o_ref.dtype)

    if n_silu == 0:
        run(False)
    else:
        j = pl.program_id(1)
        pl.when(j < n_silu)(functools.partial(run, True))
        pl.when(j >= n_silu)(functools.partial(run, False))


def _proj(h, w, col_tiles, n_silu):
    r, d = h.shape
    tm, tn = BIG_ROW_TILE, IN_COL_TILE
    first, skip_from, skip = col_tiles
    n_tiles = w.shape[1] // tn - first - skip
    wcol = lambda i, j: (0, first + j + jnp.where(j >= skip_from, skip, 0))
    return pl.pallas_call(
        functools.partial(_proj_kernel, n_silu=n_silu),
        grid=(r // tm, n_tiles),
        in_specs=[pl.BlockSpec((tm, d), lambda i, j: (i, 0)), pl.BlockSpec((d, tn), wcol)],
        out_specs=pl.BlockSpec((tm, tn), lambda i, j: (i, j)),
        out_shape=jax.ShapeDtypeStruct((r, n_tiles * tn), BF16),
        compiler_params=_params("parallel", "arbitrary"),
        name="in_proj",
    )(h, w)


def _proj_gates_kernel(h_ref, w_ref, lb_ref, g_ref, key_ref):
    h = h_ref[...]
    for cols in _col_tiles():
        _, g, key = _hg_gates(_dot(h, w_ref[:, cols]), lb_ref[:, cols])
        g_ref[:, cols] = g
        key_ref[:, cols] = key.astype(key_ref.dtype)


def _proj_gates(h, w, lower_bound, first_tile, n_tiles):
    r, d = h.shape
    tm, tn = BIG_ROW_TILE, IN_COL_TILE
    lb_tiles = lower_bound.shape[0] // tn
    out = pl.BlockSpec((tm, tn), lambda i, j: (i, j))
    return pl.pallas_call(
        _proj_gates_kernel,
        grid=(r // tm, n_tiles),
        in_specs=[pl.BlockSpec((tm, d), lambda i, j: (i, 0)),
                  pl.BlockSpec((d, tn), lambda i, j: (0, first_tile + j)),
                  pl.BlockSpec((1, tn), lambda i, j: (0, j % lb_tiles))],
        out_specs=[out, out],
        out_shape=[jax.ShapeDtypeStruct((r, n_tiles * tn), F32),
                   jax.ShapeDtypeStruct((r, n_tiles * tn), BF16)],
        compiler_params=_params("parallel", "arbitrary"),
        name="in_proj_gates",
    )(h, w, lower_bound.reshape(1, -1))


def _outproj_kernel(o_ref, g_ref, ng_ref, w_ref, x_ref, gt_ref, n2_ref, sh_ref, sc_ref,
                    xo_ref, h_ref, y_ref, *, hgrn):
    def gate(rows):
        o = o_ref[rows, :].astype(F32)
        g = g_ref[rows, :].astype(F32)
        if hgrn:
            o = o * lax.rsqrt(jnp.mean(o * o, axis=-1, keepdims=True) + EPS) * ng_ref[...]
            y = o * _sigmoid(g)
        else:
            y = o * (g * _sigmoid(g))
        y_ref[rows, :] = y.astype(BF16)
    _row_loop(ROW_TILE, gate)

    y = y_ref[...]
    for t in range(w_ref.shape[1] // MXU_COLS):
        cols = slice(t * MXU_COLS, (t + 1) * MXU_COLS)
        xo_ref[:, cols] = x_ref[:, cols] + gt_ref[:, cols] * _dot(y, w_ref[:, cols])

    def norm(rows):
        h_ref[rows, :] = _norm_mod(xo_ref[rows, :], n2_ref[...], sh_ref[...], sc_ref[...]).astype(BF16)
    _row_loop(ROW_TILE, norm, NORM_STEP)


def _out_proj(o, proj, gate_block, norm_gain, w, x, mod, gain2, rows_per_batch, n_batch, n_rows, hgrn):
    d = x.shape[1]
    tm = ROW_TILE
    row = lambda i: (i, 0)
    const = lambda i: (0, 0)
    return pl.pallas_call(
        functools.partial(_outproj_kernel, hgrn=hgrn),
        grid=(n_rows // tm,),
        in_specs=[
            pl.BlockSpec((tm, d), row),
            pl.BlockSpec((tm, d), lambda i: (i, gate_block)),
            pl.BlockSpec((1, d), const),
            pl.BlockSpec((d, d), const),
            pl.BlockSpec((tm, d), row),
            _mod_spec(d, 2, tm, rows_per_batch, n_batch),
            pl.BlockSpec((1, d), const),
            _mod_spec(d, 3, tm, rows_per_batch, n_batch),
            _mod_spec(d, 4, tm, rows_per_batch, n_batch),
        ],
        out_specs=[pl.BlockSpec((tm, d), row), pl.BlockSpec((tm, d), row)],
        out_shape=[jax.ShapeDtypeStruct((n_rows, d), F32), jax.ShapeDtypeStruct((n_rows, d), BF16)],
        scratch_shapes=[pltpu.VMEM((tm, d), BF16)],
        compiler_params=_params("parallel"),
        name="out_proj",
    )(o, proj, norm_gain.reshape(1, d), w, x, mod, gain2.reshape(1, d), mod, mod)


def _gateup_kernel(h_ref, wg_ref, wu_ref, p_ref):
    h = h_ref[...]
    for t in range(FFN_HID_TILE // MXU_COLS):
        cols = slice(t * MXU_COLS, (t + 1) * MXU_COLS)
        a = _dot(h, wg_ref[:, cols])
        b = _dot(h, wu_ref[:, cols])
        p_ref[:, cols] = (a * _sigmoid(a) * b).astype(BF16)


def _down_kernel(p_ref, wd_ref, x_ref, gt_ref, o_ref):
    o_ref[...] = x_ref[...] + gt_ref[...] * _dot(p_ref[...], wd_ref[...])


def _ffn(h2, w_gate_up, w_down, x, mod, rows_per_batch, n_batch, n_rows):
    d = x.shape[1]
    hidden = w_down.shape[0]
    tm = BIG_ROW_TILE
    n_hid = hidden // FFN_HID_TILE
    p = pl.pallas_call(
        _gateup_kernel,
        grid=(n_rows // tm, n_hid),
        in_specs=[
            pl.BlockSpec((tm, d), lambda i, j: (i, 0)),
            pl.BlockSpec((d, FFN_HID_TILE), lambda i, j: (0, j)),
            pl.BlockSpec((d, FFN_HID_TILE), lambda i, j: (0, j + n_hid)),
        ],
        out_specs=pl.BlockSpec((tm, FFN_HID_TILE), lambda i, j: (i, j)),
        out_shape=jax.ShapeDtypeStruct((n_rows, hidden), BF16),
        compiler_params=_params("parallel", "arbitrary"),
        name="ffn_gate_up",
    )(h2, w_gate_up, w_gate_up)
    tn = FFN_OUT_TILE
    return pl.pallas_call(
        _down_kernel,
        grid=(n_rows // tm, d // tn),
        in_specs=[
            pl.BlockSpec((tm, hidden), lambda i, j: (i, 0)),
            pl.BlockSpec((hidden, tn), lambda i, j: (0, j)),
            pl.BlockSpec((tm, tn), lambda i, j: (i, j)),
            _mod_spec(tn, 5 * (d // tn), tm, rows_per_batch, n_batch, col_axis=1),
        ],
        out_specs=pl.BlockSpec((tm, tn), lambda i, j: (i, j)),
        out_shape=jax.ShapeDtypeStruct((n_rows, d), F32),
        compiler_params=_params("parallel", "arbitrary"),
        name="ffn_down",
    )(p, w_down, x, mod)


def _final_norm_kernel(x_ref, g_ref, o_ref):
    def body(rows):
        x = x_ref[rows, :]
        o_ref[rows, :] = x * lax.rsqrt(jnp.mean(x * x, axis=-1, keepdims=True) + EPS) * g_ref[...]
    _row_loop(ROW_TILE, body, NORM_STEP)


def _final_norm(x, gain):
    r, d = x.shape
    return pl.pallas_call(
        _final_norm_kernel,
        grid=(r // ROW_TILE,),
        in_specs=[pl.BlockSpec((ROW_TILE, d), lambda i: (i, 0)), pl.BlockSpec((1, d), lambda i: (0, 0))],
        out_specs=pl.BlockSpec((ROW_TILE, d), lambda i: (i, 0)),
        out_shape=jax.ShapeDtypeStruct((r, d), F32),
        compiler_params=_params("parallel"),
        name="final_norm",
    )(x, gain.reshape(1, d))


def _ret_kernel(lg_ref, q_ref, k_ref, v_ref, qc_ref, kc_ref, vc_ref, cos_ref, sin_ref,
                o_ref, oc_ref, dec_ref, kr_ref, sbs_ref, sf_ref, sb_ref, *, k_scale):
    c = RET_CHUNK
    dk = q_ref.shape[1]
    half = dk // 2
    n_chunks = q_ref.shape[0] // c
    head = pl.program_id(1)
    lgf = lg_ref[0, head]
    lgb = lg_ref[1, head]

    n_i = lax.broadcasted_iota(jnp.int32, (c, c), 0).astype(F32)
    m_i = lax.broadcasted_iota(jnp.int32, (c, c), 1).astype(F32)
    diff = n_i - m_i
    dec_ref[0] = (jnp.where(diff >= 0, jnp.exp(lgf * jnp.maximum(diff, 0.0)), 0.0)
                  + jnp.where(diff <= 0, jnp.exp(lgb * jnp.maximum(-diff, 0.0)), 0.0))
    t_i = lax.broadcasted_iota(jnp.int32, (c, dk), 0).astype(F32)
    dec_ref[1] = jnp.exp(lgf * (t_i + 1.0))
    dec_ref[2] = jnp.exp(lgb * (c - t_i))
    dec_ref[3] = jnp.exp(lgf * (c - 1.0 - t_i))
    dec_ref[4] = jnp.exp(lgb * t_i)
    cf = jnp.exp(lgf * c)
    cb = jnp.exp(lgb * c)

    def head_norm(o):
        return o * lax.rsqrt(jnp.mean(o * o, axis=-1, keepdims=True) + EPS)

    def rope(t, rows):
        cos = cos_ref[rows, :]
        sin = sin_ref[rows, :]
        t1 = t[:, :half]
        t2 = t[:, half:]
        return jnp.concatenate([t1 * cos - t2 * sin, t1 * sin + t2 * cos], axis=-1)

    qc = qc_ref[...]
    kc = (kc_ref[...].astype(F32) * k_scale)
    vc = vc_ref[...]
    sc = _dot_nt(qc, kc.astype(BF16)) * dec_ref[0]
    oc_ref[...] = head_norm(_dot(sc.astype(BF16), vc)).astype(oc_ref.dtype)
    sf_ref[...] = _dot_tn((kc * dec_ref[3]).astype(BF16), vc)
    sb_ref[...] = _dot_tn((kc * dec_ref[4]).astype(BF16), vc)

    unroll = math.gcd(RET_UNROLL, n_chunks)
    n_steps = n_chunks // unroll

    def chunk_rows(ci):
        return pl.ds(pl.multiple_of(ci * c, c), c)

    def bwd_local(i, carry):
        cis = [i * unroll + u for u in range(unroll)]
        rows = [chunk_rows(ci) for ci in cis]
        krs = [rope(k_ref[r, :].astype(F32), r) * k_scale for r in rows]
        kvs = [_dot_tn((kr * dec_ref[4]).astype(BF16), v_ref[r, :]) for kr, r in zip(krs, rows)]
        for ci, r, kr, kv in zip(cis, rows, krs, kvs):
            kr_ref[r, :] = kr.astype(BF16)
            sbs_ref[ci] = kv
        return carry
    lax.fori_loop(0, n_steps, bwd_local, 0)

    def bwd_state(i, carry):
        ci = n_chunks - 1 - i
        s = sb_ref[...]
        kv = sbs_ref[ci]
        sbs_ref[ci] = s
        sb_ref[...] = s * cb + kv
        return carry
    lax.fori_loop(0, n_chunks, bwd_state, 0)

    def fwd(i, carry):
        cis = [i * unroll + u for u in range(unroll)]
        rows = [chunk_rows(ci) for ci in cis]
        qrs = [rope(q_ref[r, :].astype(F32), r).astype(BF16) for r in rows]
        krs = [kr_ref[r, :] for r in rows]
        vs = [v_ref[r, :] for r in rows]
        scs = [_dot_nt(qr, kr) for qr, kr in zip(qrs, krs)]
        kvs = [_dot_tn((kr.astype(F32) * dec_ref[3]).astype(BF16), v) for kr, v in zip(krs, vs)]
        sfs = [sf_ref[...]]
        for kv in kvs:
            sfs.append(sfs[-1] * cf + kv)
        sf_ref[...] = sfs[-1]
        intra = [_dot((sc * dec_ref[0]).astype(BF16), v) for sc, v in zip(scs, vs)]
        inter_f = [_dot(qr, s.astype(BF16)) for qr, s in zip(qrs, sfs)]
        inter_b = [_dot(qr, sbs_ref[ci].astype(BF16)) for qr, ci in zip(qrs, cis)]
        for r, o1, o2, o3 in zip(rows, intra, inter_f, inter_b):
            o = o1 + dec_ref[1] * o2 + dec_ref[2] * o3
            o_ref[r, :] = head_norm(o).astype(o_ref.dtype)
        return carry
    lax.fori_loop(0, n_steps, fwd, 0)


def _retention(proj, log_gamma, cos, sin, n_batch, seq, ctx_len):
    heads = RET_HEADS
    dk = proj.shape[1] // (4 * heads)
    d = heads * dk
    assert dk == RET_CHUNK and ctx_len == RET_CHUNK and seq % RET_CHUNK == 0
    ctx_row0 = (n_batch * seq) // ctx_len
    lat = lambda part: pl.BlockSpec((seq, dk), lambda b, h: (b, part * heads + h))
    ctx = lambda part: pl.BlockSpec((ctx_len, dk), lambda b, h: (ctx_row0 + b, part * heads + h))
    tab = pl.BlockSpec((seq, dk // 2), lambda b, h: (0, 0))
    n_chunks = seq // RET_CHUNK
    return pl.pallas_call(
        functools.partial(_ret_kernel, k_scale=dk ** -0.5),
        grid=(n_batch, heads),
        in_specs=[pl.BlockSpec(memory_space=pltpu.SMEM),
                  lat(0), lat(1), lat(2), ctx(0), ctx(1), ctx(2), tab, tab],
        out_specs=[pl.BlockSpec((seq, dk), lambda b, h: (b, h)),
                   pl.BlockSpec((ctx_len, dk), lambda b, h: (b, h))],
        out_shape=[jax.ShapeDtypeStruct((n_batch * seq, d), BF16),
                   jax.ShapeDtypeStruct((n_batch * ctx_len, d), BF16)],
        scratch_shapes=[
            pltpu.VMEM((5, RET_CHUNK, RET_CHUNK), F32),
            pltpu.VMEM((seq, dk), BF16),
            pltpu.VMEM((n_chunks, dk, dk), F32),
            pltpu.VMEM((dk, dk), F32),
            pltpu.VMEM((dk, dk), F32),
        ],
        compiler_params=_params("parallel", "arbitrary"),
        name="retention",
    )(log_gamma, proj, proj, proj, proj, proj, proj, cos, sin)


def _hg_gates(z, lb):
    en = jnp.exp(-jnp.abs(z))
    r = 1.0 / (1.0 + en)
    pos = z >= 0
    f = lb + (1.0 - lb) * jnp.where(pos, r, en * r)
    key = (1.0 - lb) * jnp.where(pos, en * r, r)
    return f, jnp.log(f) * LOG2E, key


def _hg_cumsum(tri_ref, g):
    dk = g.shape[1]
    hi = g.astype(BF16)
    lo = (g - hi.astype(F32)).astype(BF16)
    r = _dot(tri_ref[...], jnp.concatenate([hi, lo], axis=1))
    return r[:, :dk] + r[:, dk:]


def _tiles(x):
    return [x[j * V7X_SUBLANES:(j + 1) * V7X_SUBLANES] for j in range(x.shape[0] // V7X_SUBLANES)]


def _row_of_tile(tile, r):
    return jnp.broadcast_to(tile[r:r + 1, :], tile.shape)


def _hg_levels(up, lo, cv, cin):
    upt, lot, cvt, cint = _tiles(up), _tiles(lo), _tiles(cv), _tiles(cin)
    nt = len(cvt)
    zero = jnp.zeros_like(cvt[0])
    lastt = [_row_of_tile(t, V7X_SUBLANES - 1) for t in cint]
    out = []
    bt = nt
    while bt >= 2:
        ht = bt // 2
        hi_rows, lo_rows = [], []
        for j in range(nt):
            b0 = (j // bt) * bt
            ref = lastt[b0 + ht - 1]
            if j - b0 >= ht:
                hi_rows.append(jnp.exp2(cvt[j] - ref) * upt[j])
                lo_rows.append(zero)
            else:
                hi_rows.append(zero)
                lo_rows.append(jnp.exp2(ref - cvt[j]) * lot[j])
        out.append((jnp.concatenate(hi_rows, axis=0), jnp.concatenate(lo_rows, axis=0)))
        bt = ht
    sub = lax.broadcasted_iota(jnp.int32, zero.shape, 0)
    for size in (8, 4):
        upper = (sub & (size // 2)) != 0
        sign = jnp.where(upper, 1.0, -1.0)
        hi_rows, lo_rows = [], []
        for j in range(nt):
            if size == 8:
                ref = _row_of_tile(cint[j], 3)
            else:
                ref = jnp.where(sub < 4, _row_of_tile(cint[j], 1), _row_of_tile(cint[j], 5))
            z = jnp.exp2((cvt[j] - ref) * sign) * jnp.where(upper, upt[j], lot[j])
            hi_rows.append(jnp.where(upper, z, 0.0))
            lo_rows.append(jnp.where(upper, 0.0, z))
        out.append((jnp.concatenate(hi_rows, axis=0), jnp.concatenate(lo_rows, axis=0)))
    return out


def _hg_bwd_local(rows, ci, g_ref, key_ref, v_ref, tri_ref, cumb_ref, kv_ref):
    g = g_ref[rows, :]
    cum = _hg_cumsum(tri_ref, g)
    cumb_ref[rows, :] = cum
    k_dec = key_ref[rows, :].astype(F32) * jnp.exp2(cum - g)
    kv_ref[ci] = _dot_tn(v_ref[rows, :], k_dec.astype(BF16))


def _hg_step_out(rows_list, cis, q_ref, gf_ref, kf_ref, gb_ref, kb_ref, v_ref, tri_ref, cumb_ref,
                 sbs_ref, sf_ref, out_ref, pair_xor):
    n = len(cis)
    q = [q_ref[r, :].astype(F32) for r in rows_list]
    kf = [kf_ref[r, :].astype(F32) for r in rows_list]
    kb = [kb_ref[r, :].astype(F32) for r in rows_list]
    v = [v_ref[r, :] for r in rows_list]
    gf = [gf_ref[r, :] for r in rows_list]
    gb = [gb_ref[r, :] for r in rows_list]
    c = q[0].shape[0]

    cumf = [_hg_cumsum(tri_ref, g) for g in gf]
    cumb = [cumb_ref[r, :] for r in rows_list]
    cumxb = [cb - g for cb, g in zip(cumb, gb)]
    totf = [cf[c - 1:, :] for cf in cumf]
    totb = [cb[c - 1:, :] for cb in cumb]

    kvs = [_dot_tn(v[u], (kf[u] * jnp.exp2(totf[u] - cumf[u])).astype(BF16)) for u in range(n)]
    sf = [sf_ref[...]]
    for u in range(n):
        sf.append(sf[u] * jnp.exp2(totf[u]) + kvs[u])
    sf_ref[...] = sf[n]

    odd = (lax.broadcasted_iota(jnp.int32, q[0].shape, 0) & 1) != 0
    scores = []
    for u in range(n):
        lev_f = _hg_levels(q[u], kf[u], cumf[u], cumf[u])
        lev_b = _hg_levels(kb[u], q[u], cumxb[u], cumb[u])
        lev_f.append((jnp.where(odd, q[u] * jnp.exp2(gf[u]), 0.0), jnp.where(odd, 0.0, kf[u])))
        lev_b.append((jnp.where(odd, kb[u], 0.0), jnp.where(odd, 0.0, q[u] * jnp.exp2(gb[u]))))
        a = None
        size = c
        for (xf, yf), (yb, xb) in zip(lev_f, lev_b):
            x = jnp.concatenate([xf.astype(BF16), xb.astype(BF16)], axis=1)
            y = jnp.concatenate([yf.astype(BF16), yb.astype(BF16)], axis=1)
            p = _dot_nt(x, y)
            a = p if a is None else jnp.where(pair_xor < size, p, a)
            size //= 2
        scores.append(a.astype(BF16))

    inter = []
    for u in range(n):
        q_dec = jnp.concatenate([(q[u] * jnp.exp2(cumf[u])).astype(BF16),
                                 (q[u] * jnp.exp2(totb[u] - cumxb[u])).astype(BF16)], axis=1)
        states = jnp.concatenate([sf[u].astype(BF16), sbs_ref[cis[u]].astype(BF16)], axis=1)
        inter.append(_dot_nt(q_dec, states))

    for u in range(n):
        o = _dot(scores[u], v[u]) + inter[u]
        o += jnp.sum(q[u] * (kf[u] + kb[u]), axis=-1, keepdims=True) * v[u].astype(F32)
        out_ref[rows_list[u], :] = o.astype(out_ref.dtype)


def _hg_kernel(q_ref, gf_ref, kf_ref, gb_ref, kb_ref, v_ref,
               qc_ref, gfc_ref, kfc_ref, gbc_ref, kbc_ref, vc_ref,
               o_ref, oc_ref, tri_ref, cumb_ref, sbs_ref, sf_ref, sb_ref):
    c = HG_CHUNK
    n_i = lax.broadcasted_iota(jnp.int32, (c, c), 0)
    m_i = lax.broadcasted_iota(jnp.int32, (c, c), 1)
    pair_xor = n_i ^ m_i
    tri_ref[...] = jnp.where(m_i <= n_i, 1.0, 0.0).astype(BF16)

    def run(qr_ref, gfr_ref, kfr_ref, gbr_ref, kbr_ref, vr_ref, out_ref):
        n_chunks = qr_ref.shape[0] // c
        unroll = math.gcd(HG_UNROLL, n_chunks)
        n_steps = n_chunks // unroll

        def chunk_rows(ci):
            return pl.ds(pl.multiple_of(ci * c, c), c)

        def bwd_local(i, carry):
            cis = [i * unroll + u for u in range(unroll)]
            gs = [gbr_ref[chunk_rows(ci), :] for ci in cis]
            cums = [_hg_cumsum(tri_ref, g) for g in gs]
            k_decs = [(kbr_ref[chunk_rows(ci), :].astype(F32) * jnp.exp2(cum - g)).astype(BF16)
                      for ci, g, cum in zip(cis, gs, cums)]
            kvs = [_dot_tn(vr_ref[chunk_rows(ci), :], k_dec) for ci, k_dec in zip(cis, k_decs)]
            for ci, cum, kv in zip(cis, cums, kvs):
                cumb_ref[chunk_rows(ci), :] = cum
                sbs_ref[ci] = kv
            return carry
        lax.fori_loop(0, n_steps, bwd_local, 0)

        def bwd_state(i, carry):
            ci = n_chunks - 1 - i
            rows = chunk_rows(ci)
            s = sb_ref[...]
            kv = sbs_ref[ci]
            sbs_ref[ci] = s
            last_tile = cumb_ref[pl.ds(pl.multiple_of(ci * c + c - V7X_SUBLANES, V7X_SUBLANES), V7X_SUBLANES), :]
            sb_ref[...] = s * jnp.exp2(last_tile[V7X_SUBLANES - 1:, :]) + kv
            return carry
        lax.fori_loop(0, n_chunks, bwd_state, 0)

        def fwd(i, carry):
            cis = [i * unroll + u for u in range(unroll)]
            _hg_step_out([chunk_rows(ci) for ci in cis], cis, qr_ref, gfr_ref, kfr_ref, gbr_ref, kbr_ref,
                         vr_ref, tri_ref, cumb_ref, sbs_ref, sf_ref, out_ref, pair_xor)
            return carry
        lax.fori_loop(0, n_steps, fwd, 0)

    sf_ref[...] = jnp.zeros_like(sf_ref)
    sb_ref[...] = jnp.zeros_like(sb_ref)
    run(qc_ref, gfc_ref, kfc_ref, gbc_ref, kbc_ref, vc_ref, oc_ref)
    run(q_ref, gf_ref, kf_ref, gb_ref, kb_ref, v_ref, o_ref)


def _hgrn(qiv, log_f, key, n_batch, seq, ctx_len):
    dk = HG_EXPAND
    heads = log_f.shape[1] // (2 * dk)
    d = heads * dk
    assert seq % HG_CHUNK == 0 and ctx_len % HG_CHUNK == 0 and ctx_len <= seq
    ctx_row0 = (n_batch * seq) // ctx_len
    lat = lambda part: pl.BlockSpec((seq, dk), lambda b, h: (b, part * heads + h))
    ctx = lambda part: pl.BlockSpec((ctx_len, dk), lambda b, h: (ctx_row0 + b, part * heads + h))
    return pl.pallas_call(
        _hg_kernel,
        grid=(n_batch, heads),
        in_specs=[lat(0), lat(0), lat(0), lat(1), lat(1), lat(1),
                  ctx(0), ctx(0), ctx(0), ctx(1), ctx(1), ctx(1)],
        out_specs=[pl.BlockSpec((seq, dk), lambda b, h: (b, h)),
                   pl.BlockSpec((ctx_len, dk), lambda b, h: (b, h))],
        out_shape=[jax.ShapeDtypeStruct((n_batch * seq, d), BF16),
                   jax.ShapeDtypeStruct((n_batch * ctx_len, d), BF16)],
        scratch_shapes=[
            pltpu.VMEM((HG_CHUNK, HG_CHUNK), BF16),
            pltpu.VMEM((seq, dk), F32),
            pltpu.VMEM((seq // HG_CHUNK, dk, dk), F32),
            pltpu.VMEM((dk, dk), F32),
            pltpu.VMEM((dk, dk), F32),
        ],
        compiler_params=_params("parallel", "arbitrary"),
        name="hgrn",
    )(qiv, log_f, key, log_f, key, qiv, qiv, log_f, key, log_f, key, qiv)


def _rope_tables(seq, head_dim):
    quarter = head_dim // 4
    rows = jnp.repeat(jnp.arange(seq // GRID_W, dtype=F32), GRID_W)
    cols = jnp.tile(jnp.arange(GRID_W, dtype=F32), seq // GRID_W)
    inv_freq = ROPE_BASE ** (-jnp.arange(quarter, dtype=F32) / quarter)
    ang = jnp.concatenate([rows[:, None] * inv_freq, cols[:, None] * inv_freq], axis=-1)
    return jnp.cos(ang), jnp.sin(ang)


def kernel(x, c, ctx, c_ctx, ada_w, ada_b, norm1_g, norm2_g, ret_w_in, ret_w_out, ret_decay_logits,
           hg_w_in, hg_w_out, hg_norm_g, hg_lower_bounds, ffn_w_gate_up, ffn_w_down, final_norm_g):
    n_batch, seq, d = x.shape
    ctx_len = ctx.shape[1]
    depth = ada_w.shape[0]
    n_lat = n_batch * seq
    n_all = n_lat + n_batch * ctx_len
    assert seq % BIG_ROW_TILE == 0 and (n_batch * ctx_len) % BIG_ROW_TILE == 0

    xs = jnp.concatenate([x.reshape(n_lat, d), ctx.reshape(n_batch * ctx_len, d)], axis=0)

    cond_rows = -(-(n_batch + 1) // V7X_SUBLANES) * V7X_SUBLANES
    cond = jnp.zeros((cond_rows, d), F32).at[:n_batch].set(c).at[n_batch].set(c_ctx)
    mods = _ada_mod(cond, ada_w, ada_b).reshape(depth, cond_rows, 1, 6 * d)

    lb_p = jax.nn.softmax(hg_lower_bounds.astype(F32), axis=0)
    lower_bounds = jnp.cumsum(lb_p, axis=0) - lb_p[0]
    log_gamma = jax.nn.log_sigmoid(ret_decay_logits.astype(F32))
    cos, sin = _rope_tables(seq, d // RET_HEADS)

    for layer in range(depth):
        last = layer == depth - 1
        j = layer // N_MIXERS
        mod = mods[layer]
        n_rows = n_lat if last else n_all
        retention = layer % N_MIXERS == 0
        w_in = (ret_w_in if retention else hg_w_in)[j].astype(BF16)
        w_out = (ret_w_out if retention else hg_w_out)[j].astype(BF16)
        h1 = _prenorm(xs, norm1_g[layer], mod, seq, n_batch)
        tiles_per_part = d // IN_COL_TILE
        if retention:
            proj = _proj(h1, w_in, (0, w_in.shape[1], 0), 0)
            o_lat, o_ctx = _retention(proj, log_gamma[j], cos, sin, n_batch, seq, ctx_len)
            norm_gain = jnp.ones((d,), F32)
        else:
            proj = _proj(h1, w_in, (0, tiles_per_part, 2 * tiles_per_part), tiles_per_part)
            log_f, key = _proj_gates(h1, w_in, lower_bounds[j], tiles_per_part, 2 * tiles_per_part)
            o_lat, o_ctx = _hgrn(proj, log_f, key, n_batch, seq, ctx_len)
            norm_gain = hg_norm_g[j]
        o = o_lat if last else jnp.concatenate([o_lat, o_ctx], axis=0)
        gate_block = proj.shape[1] // d - 1
        xs, h2 = _out_proj(o, proj, gate_block, norm_gain, w_out, xs, mod, norm2_g[layer],
                           seq, n_batch, n_rows, not retention)
        xs = _ffn(h2, ffn_w_gate_up[layer].astype(BF16), ffn_w_down[layer].astype(BF16), xs, mod,
                  seq, n_batch, n_rows)
    return _final_norm(xs, final_norm_g).reshape(n_batch, seq, d)
```

```python
import functools
import math

import jax
import jax.numpy as jnp
from jax import lax
from jax.experimental import pallas as pl
from jax.experimental.pallas import tpu as pltpu

F32 = jnp.float32
BF16 = jnp.bfloat16

EPS = 1e-6
LOG2E = 1.4426950408889634
ROPE_BASE = 10000.0
GRID_W = 64
N_MIXERS = 2
RET_HEADS = 8
HG_EXPAND = 128

V7X_LANES = 128
V7X_SUBLANES = 8
V7X_VMEM_BYTES = 64 * 1024 * 1024

ROW_TILE = 512
BIG_ROW_TILE = 1024
IN_COL_TILE = 1024
FFN_HID_TILE = 512
FFN_OUT_TILE = 512
MXU_COLS = 256
ADA_COL_TILE = 1024
RET_CHUNK = 256
RET_UNROLL = 4
HG_CHUNK = 128
HG_UNROLL = 4
ROW_STEP = 64
NORM_STEP = 128
VMEM_LIMIT = 56 * 1024 * 1024


def _params(*semantics):
    return pltpu.CompilerParams(dimension_semantics=semantics, vmem_limit_bytes=VMEM_LIMIT)


def _sigmoid(x):
    return 1.0 / (1.0 + jnp.exp(-x))


def _dot(a, b):
    return jnp.dot(a, b, preferred_element_type=F32)


def _dot_nt(a, b):
    return lax.dot_general(a, b, (((1,), (1,)), ((), ())), preferred_element_type=F32)


def _dot_tn(a, b):
    return lax.dot_general(a, b, (((0,), (0,)), ((), ())), preferred_element_type=F32)


def _ada_kernel(c_ref, w_ref, b_ref, o_ref):
    c = c_ref[...]
    a = (c * _sigmoid(c)).astype(BF16)
    o_ref[...] = _dot(a, w_ref[...].astype(BF16)) + b_ref[...]


def _ada_mod(cond, ada_w, ada_b):
    depth, d, n = ada_w.shape
    rows = cond.shape[0]
    return pl.pallas_call(
        _ada_kernel,
        grid=(depth, n // ADA_COL_TILE),
        in_specs=[
            pl.BlockSpec((rows, d), lambda l, j: (0, 0)),
            pl.BlockSpec((None, d, ADA_COL_TILE), lambda l, j: (l, 0, j)),
            pl.BlockSpec((None, 1, ADA_COL_TILE), lambda l, j: (l, 0, j)),
        ],
        out_specs=pl.BlockSpec((None, rows, ADA_COL_TILE), lambda l, j: (l, 0, j)),
        out_shape=jax.ShapeDtypeStruct((depth, rows, n), F32),
        compiler_params=_params("parallel", "parallel"),
        name="ada_mod",
    )(cond, ada_w, ada_b.reshape(depth, 1, n))


def _norm_mod(x, gain, shift, scale):
    return x * lax.rsqrt(jnp.mean(x * x, axis=-1, keepdims=True) + EPS) * (gain * (1.0 + scale)) + shift


def _row_loop(rows, body, step_rows=ROW_STEP):
    def step(i, carry):
        body(pl.ds(pl.multiple_of(i * step_rows, step_rows), step_rows))
        return carry
    lax.fori_loop(0, rows // step_rows, step, 0)


def _mod_spec(width, slot, row_tile, rows_per_batch, n_batch, col_axis=None):
    def index(*ids):
        col = slot if col_axis is None else slot + ids[col_axis]
        return (jnp.minimum((ids[0] * row_tile) // rows_per_batch, n_batch), 0, col)
    return pl.BlockSpec((None, 1, width), index)


def _prenorm_kernel(x_ref, g_ref, sh_ref, sc_ref, h_ref):
    def body(rows):
        h = _norm_mod(x_ref[rows, :], g_ref[...], sh_ref[...], sc_ref[...])
        h_ref[rows, :] = h.astype(BF16)
    _row_loop(ROW_TILE, body, NORM_STEP)


def _prenorm(x, gain, mod, rows_per_batch, n_batch):
    r, d = x.shape
    tm = ROW_TILE
    return pl.pallas_call(
        _prenorm_kernel,
        grid=(r // tm,),
        in_specs=[
            pl.BlockSpec((tm, d), lambda i: (i, 0)),
            pl.BlockSpec((1, d), lambda i: (0, 0)),
            _mod_spec(d, 0, tm, rows_per_batch, n_batch),
            _mod_spec(d, 1, tm, rows_per_batch, n_batch),
        ],
        out_specs=pl.BlockSpec((tm, d), lambda i: (i, 0)),
        out_shape=jax.ShapeDtypeStruct((r, d), BF16),
        compiler_params=_params("parallel"),
        name="prenorm",
    )(x, gain.reshape(1, d), mod, mod)


def _col_tiles():
    return [slice(t * MXU_COLS, (t + 1) * MXU_COLS) for t in range(IN_COL_TILE // MXU_COLS)]


def _proj_kernel(h_ref, w_ref, o_ref, *, n_silu):
    h = h_ref[...]

    def run(silu):
        for cols in _col_tiles():
            a = _dot(h, w_ref[:, cols])
            if silu:
                a = a * _sigmoid(a)
            o_ref[:, cols] = a.astype(o_ref.dtype)

    if n_silu == 0:
        run(False)
    else:
        j = pl.program_id(1)
        pl.when(j < n_silu)(functools.partial(run, True))
        pl.when(j >= n_silu)(functools.partial(run, False))


def _proj(h, w, layer, col_tiles, n_silu):
    r, d = h.shape
    tm, tn = BIG_ROW_TILE, IN_COL_TILE
    first, skip_from, skip = col_tiles
    n_tiles = w.shape[2] // tn - first - skip
    wcol = lambda i, j: (layer, 0, first + j + jnp.where(j >= skip_from, skip, 0))
    return pl.pallas_call(
        functools.partial(_proj_kernel, n_silu=n_silu),
        grid=(r // tm, n_tiles),
        in_specs=[pl.BlockSpec((tm, d), lambda i, j: (i, 0)), pl.BlockSpec((None, d, tn), wcol)],
        out_specs=pl.BlockSpec((tm, tn), lambda i, j: (i, j)),
        out_shape=jax.ShapeDtypeStruct((r, n_tiles * tn), BF16),
        compiler_params=_params("parallel", "arbitrary"),
        name="in_proj",
    )(h, w)


def _proj_gates_kernel(h_ref, w_ref, lb_ref, g_ref, key_ref):
    h = h_ref[...]
    for cols in _col_tiles():
        _, g, key = _hg_gates(_dot(h, w_ref[:, cols]), lb_ref[:, cols])
        g_ref[:, cols] = g
        key_ref[:, cols] = key.astype(key_ref.dtype)


def _proj_gates(h, w, layer, lower_bound, first_tile, n_tiles):
    r, d = h.shape
    tm, tn = BIG_ROW_TILE, IN_COL_TILE
    lb_tiles = lower_bound.shape[0] // tn
    out = pl.BlockSpec((tm, tn), lambda i, j: (i, j))
    return pl.pallas_call(
        _proj_gates_kernel,
        grid=(r // tm, n_tiles),
        in_specs=[pl.BlockSpec((tm, d), lambda i, j: (i, 0)),
                  pl.BlockSpec((None, d, tn), lambda i, j: (layer, 0, first_tile + j)),
                  pl.BlockSpec((1, tn), lambda i, j: (0, j % lb_tiles))],
        out_specs=[out, out],
        out_shape=[jax.ShapeDtypeStruct((r, n_tiles * tn), F32),
                   jax.ShapeDtypeStruct((r, n_tiles * tn), BF16)],
        compiler_params=_params("parallel", "arbitrary"),
        name="in_proj_gates",
    )(h, w, lower_bound.reshape(1, -1))


def _outproj_kernel(ol_ref, oc_ref, g_ref, ng_ref, w_ref, x_ref, gt_ref, n2_ref, sh_ref, sc_ref,
                    xo_ref, h_ref, y_ref, *, hgrn, n_lat_tiles):
    def gate_from(o_ref):
        def gate(rows):
            o = o_ref[rows, :].astype(F32)
            g = g_ref[rows, :].astype(F32)
            if hgrn:
                o = o * lax.rsqrt(jnp.mean(o * o, axis=-1, keepdims=True) + EPS) * ng_ref[...]
                y = o * _sigmoid(g)
            else:
                y = o * (g * _sigmoid(g))
            y_ref[rows, :] = y.astype(BF16)
        _row_loop(ROW_TILE, gate)

    tile = pl.program_id(0)
    pl.when(tile < n_lat_tiles)(functools.partial(gate_from, ol_ref))
    pl.when(tile >= n_lat_tiles)(functools.partial(gate_from, oc_ref))

    y = y_ref[...]
    for t in range(w_ref.shape[1] // MXU_COLS):
        cols = slice(t * MXU_COLS, (t + 1) * MXU_COLS)
        xo_ref[:, cols] = x_ref[:, cols] + gt_ref[:, cols] * _dot(y, w_ref[:, cols])

    def norm(rows):
        h_ref[rows, :] = _norm_mod(xo_ref[rows, :], n2_ref[...], sh_ref[...], sc_ref[...]).astype(BF16)
    _row_loop(ROW_TILE, norm, NORM_STEP)


def _out_proj(o_lat, o_ctx, proj, gate_block, norm_gain, w, layer, x, mod, gain2, rows_per_batch, n_batch,
              n_rows, hgrn):
    d = x.shape[1]
    tm = ROW_TILE
    n_lat_tiles = o_lat.shape[0] // tm
    row = lambda i: (i, 0)
    const = lambda i: (0, 0)
    return pl.pallas_call(
        functools.partial(_outproj_kernel, hgrn=hgrn, n_lat_tiles=n_lat_tiles),
        grid=(n_rows // tm,),
        in_specs=[
            pl.BlockSpec((tm, d), lambda i: (jnp.minimum(i, n_lat_tiles - 1), 0)),
            pl.BlockSpec((tm, d), lambda i: (jnp.maximum(i - n_lat_tiles, 0), 0)),
            pl.BlockSpec((tm, d), lambda i: (i, gate_block)),
            pl.BlockSpec((1, d), const),
            pl.BlockSpec((None, d, d), lambda i: (layer, 0, 0)),
            pl.BlockSpec((tm, d), row),
            _mod_spec(d, 2, tm, rows_per_batch, n_batch),
            pl.BlockSpec((1, d), const),
            _mod_spec(d, 3, tm, rows_per_batch, n_batch),
            _mod_spec(d, 4, tm, rows_per_batch, n_batch),
        ],
        out_specs=[pl.BlockSpec((tm, d), row), pl.BlockSpec((tm, d), row)],
        out_shape=[jax.ShapeDtypeStruct((n_rows, d), F32), jax.ShapeDtypeStruct((n_rows, d), BF16)],
        scratch_shapes=[pltpu.VMEM((tm, d), BF16)],
        compiler_params=_params("parallel"),
        name="out_proj",
    )(o_lat, o_ctx, proj, norm_gain.reshape(1, d), w, x, mod, gain2.reshape(1, d), mod, mod)


def _gateup_kernel(h_ref, wg_ref, wu_ref, p_ref):
    h = h_ref[...]
    for t in range(FFN_HID_TILE // MXU_COLS):
        cols = slice(t * MXU_COLS, (t + 1) * MXU_COLS)
        a = _dot(h, wg_ref[:, cols])
        b = _dot(h, wu_ref[:, cols])
        p_ref[:, cols] = (a * _sigmoid(a) * b).astype(BF16)


def _down_kernel(p_ref, wd_ref, x_ref, gt_ref, o_ref):
    o_ref[...] = x_ref[...] + gt_ref[...] * _dot(p_ref[...], wd_ref[...])


def _ffn(h2, w_gate_up, w_down, layer, x, mod, rows_per_batch, n_batch, n_rows):
    d = x.shape[1]
    hidden = w_down.shape[1]
    tm = BIG_ROW_TILE
    n_hid = hidden // FFN_HID_TILE
    p = pl.pallas_call(
        _gateup_kernel,
        grid=(n_rows // tm, n_hid),
        in_specs=[
            pl.BlockSpec((tm, d), lambda i, j: (i, 0)),
            pl.BlockSpec((None, d, FFN_HID_TILE), lambda i, j: (layer, 0, j)),
            pl.BlockSpec((None, d, FFN_HID_TILE), lambda i, j: (layer, 0, j + n_hid)),
        ],
        out_specs=pl.BlockSpec((tm, FFN_HID_TILE), lambda i, j: (i, j)),
        out_shape=jax.ShapeDtypeStruct((n_rows, hidden), BF16),
        compiler_params=_params("parallel", "arbitrary"),
        name="ffn_gate_up",
    )(h2, w_gate_up, w_gate_up)
    tn = FFN_OUT_TILE
    return pl.pallas_call(
        _down_kernel,
        grid=(n_rows // tm, d // tn),
        in_specs=[
            pl.BlockSpec((tm, hidden), lambda i, j: (i, 0)),
            pl.BlockSpec((None, hidden, tn), lambda i, j: (layer, 0, j)),
            pl.BlockSpec((tm, tn), lambda i, j: (i, j)),
            _mod_spec(tn, 5 * (d // tn), tm, rows_per_batch, n_batch, col_axis=1),
        ],
        out_specs=pl.BlockSpec((tm, tn), lambda i, j: (i, j)),
        out_shape=jax.ShapeDtypeStruct((n_rows, d), F32),
        compiler_params=_params("parallel", "arbitrary"),
        name="ffn_down",
    )(p, w_down, x, mod)


def _final_norm_kernel(x_ref, g_ref, o_ref):
    def body(rows):
        x = x_ref[rows, :]
        o_ref[rows, :] = x * lax.rsqrt(jnp.mean(x * x, axis=-1, keepdims=True) + EPS) * g_ref[...]
    _row_loop(ROW_TILE, body, NORM_STEP)


def _final_norm(x, gain):
    r, d = x.shape
    return pl.pallas_call(
        _final_norm_kernel,
        grid=(r // ROW_TILE,),
        in_specs=[pl.BlockSpec((ROW_TILE, d), lambda i: (i, 0)), pl.BlockSpec((1, d), lambda i: (0, 0))],
        out_specs=pl.BlockSpec((ROW_TILE, d), lambda i: (i, 0)),
        out_shape=jax.ShapeDtypeStruct((r, d), F32),
        compiler_params=_params("parallel"),
        name="final_norm",
    )(x, gain.reshape(1, d))


def _ret_kernel(lg_ref, q_ref, k_ref, v_ref, qc_ref, kc_ref, vc_ref, cos_ref, sin_ref,
                o_ref, oc_ref, dec_ref, kr_ref, sbs_ref, sf_ref, sb_ref, *, k_scale):
    c = RET_CHUNK
    dk = q_ref.shape[1]
    half = dk // 2
    n_chunks = q_ref.shape[0] // c
    head = pl.program_id(1)
    lgf = lg_ref[0, head]
    lgb = lg_ref[1, head]

    n_i = lax.broadcasted_iota(jnp.int32, (c, c), 0).astype(F32)
    m_i = lax.broadcasted_iota(jnp.int32, (c, c), 1).astype(F32)
    diff = n_i - m_i
    dec_ref[0] = (jnp.where(diff >= 0, jnp.exp(lgf * jnp.maximum(diff, 0.0)), 0.0)
                  + jnp.where(diff <= 0, jnp.exp(lgb * jnp.maximum(-diff, 0.0)), 0.0))
    t_i = lax.broadcasted_iota(jnp.int32, (c, dk), 0).astype(F32)
    dec_ref[1] = jnp.exp(lgf * (t_i + 1.0))
    dec_ref[2] = jnp.exp(lgb * (c - t_i))
    dec_ref[3] = jnp.exp(lgf * (c - 1.0 - t_i))
    dec_ref[4] = jnp.exp(lgb * t_i)
    cf = jnp.exp(lgf * c)
    cb = jnp.exp(lgb * c)

    def head_norm(o):
        return o * lax.rsqrt(jnp.mean(o * o, axis=-1, keepdims=True) + EPS)

    def rope(t, rows):
        cos = cos_ref[rows, :]
        sin = sin_ref[rows, :]
        t1 = t[:, :half]
        t2 = t[:, half:]
        return jnp.concatenate([t1 * cos - t2 * sin, t1 * sin + t2 * cos], axis=-1)

    qc = qc_ref[...]
    kc = (kc_ref[...].astype(F32) * k_scale)
    vc = vc_ref[...]
    sc = _dot_nt(qc, kc.astype(BF16)) * dec_ref[0]
    oc_ref[...] = head_norm(_dot(sc.astype(BF16), vc)).astype(oc_ref.dtype)
    sf_ref[...] = _dot_tn((kc * dec_ref[3]).astype(BF16), vc)
    sb_ref[...] = _dot_tn((kc * dec_ref[4]).astype(BF16), vc)

    unroll = math.gcd(RET_UNROLL, n_chunks)
    n_steps = n_chunks // unroll

    def chunk_rows(ci):
        return pl.ds(pl.multiple_of(ci * c, c), c)

    def bwd_local(i, carry):
        cis = [i * unroll + u for u in range(unroll)]
        rows = [chunk_rows(ci) for ci in cis]
        krs = [rope(k_ref[r, :].astype(F32), r) * k_scale for r in rows]
        kvs = [_dot_tn((kr * dec_ref[4]).astype(BF16), v_ref[r, :]) for kr, r in zip(krs, rows)]
        for ci, r, kr, kv in zip(cis, rows, krs, kvs):
            kr_ref[r, :] = kr.astype(BF16)
            sbs_ref[ci] = kv
        return carry
    lax.fori_loop(0, n_steps, bwd_local, 0)

    def bwd_state(i, carry):
        ci = n_chunks - 1 - i
        s = sb_ref[...]
        kv = sbs_ref[ci]
        sbs_ref[ci] = s
        sb_ref[...] = s * cb + kv
        return carry
    lax.fori_loop(0, n_chunks, bwd_state, 0)

    def fwd(i, carry):
        cis = [i * unroll + u for u in range(unroll)]
        rows = [chunk_rows(ci) for ci in cis]
        qrs = [rope(q_ref[r, :].astype(F32), r).astype(BF16) for r in rows]
        krs = [kr_ref[r, :] for r in rows]
        vs = [v_ref[r, :] for r in rows]
        scs = [_dot_nt(qr, kr) for qr, kr in zip(qrs, krs)]
        kvs = [_dot_tn((kr.astype(F32) * dec_ref[3]).astype(BF16), v) for kr, v in zip(krs, vs)]
        sfs = [sf_ref[...]]
        for kv in kvs:
            sfs.append(sfs[-1] * cf + kv)
        sf_ref[...] = sfs[-1]
        intra = [_dot((sc * dec_ref[0]).astype(BF16), v) for sc, v in zip(scs, vs)]
        inter_f = [_dot(qr, s.astype(BF16)) for qr, s in zip(qrs, sfs)]
        inter_b = [_dot(qr, sbs_ref[ci].astype(BF16)) for qr, ci in zip(qrs, cis)]
        for r, o1, o2, o3 in zip(rows, intra, inter_f, inter_b):
            o = o1 + dec_ref[1] * o2 + dec_ref[2] * o3
            o_ref[r, :] = head_norm(o).astype(o_ref.dtype)
        return carry
    lax.fori_loop(0, n_steps, fwd, 0)


def _retention(proj, log_gamma, cos, sin, n_batch, seq, ctx_len):
    heads = RET_HEADS
    dk = proj.shape[1] // (4 * heads)
    d = heads * dk
    assert dk == RET_CHUNK and ctx_len == RET_CHUNK and seq % RET_CHUNK == 0
    ctx_row0 = (n_batch * seq) // ctx_len
    lat = lambda part: pl.BlockSpec((seq, dk), lambda b, h: (b, part * heads + h))
    ctx = lambda part: pl.BlockSpec((ctx_len, dk), lambda b, h: (ctx_row0 + b, part * heads + h))
    tab = pl.BlockSpec((seq, dk // 2), lambda b, h: (0, 0))
    n_chunks = seq // RET_CHUNK
    return pl.pallas_call(
        functools.partial(_ret_kernel, k_scale=dk ** -0.5),
        grid=(n_batch, heads),
        in_specs=[pl.BlockSpec(memory_space=pltpu.SMEM),
                  lat(0), lat(1), lat(2), ctx(0), ctx(1), ctx(2), tab, tab],
        out_specs=[pl.BlockSpec((seq, dk), lambda b, h: (b, h)),
                   pl.BlockSpec((ctx_len, dk), lambda b, h: (b, h))],
        out_shape=[jax.ShapeDtypeStruct((n_batch * seq, d), BF16),
                   jax.ShapeDtypeStruct((n_batch * ctx_len, d), BF16)],
        scratch_shapes=[
            pltpu.VMEM((5, RET_CHUNK, RET_CHUNK), F32),
            pltpu.VMEM((seq, dk), BF16),
            pltpu.VMEM((n_chunks, dk, dk), F32),
            pltpu.VMEM((dk, dk), F32),
            pltpu.VMEM((dk, dk), F32),
        ],
        compiler_params=_params("parallel", "arbitrary"),
        name="retention",
    )(log_gamma, proj, proj, proj, proj, proj, proj, cos, sin)


def _hg_gates(z, lb):
    f = lb + (1.0 - lb) * _sigmoid(z)
    return f, jnp.log2(f), 1.0 - f


def _hg_cumsum(tri_ref, g):
    dk = g.shape[1]
    hi = g.astype(BF16)
    lo = (g - hi.astype(F32)).astype(BF16)
    r = _dot(tri_ref[...], jnp.concatenate([hi, lo], axis=1))
    return r[:, :dk] + r[:, dk:]


def _tiles(x):
    return [x[j * V7X_SUBLANES:(j + 1) * V7X_SUBLANES] for j in range(x.shape[0] // V7X_SUBLANES)]


def _row_of_tile(tile, r):
    return jnp.broadcast_to(tile[r:r + 1, :], tile.shape)


def _hg_levels(up, lo, cv, cin):
    upt, lot, cvt, cint = _tiles(up), _tiles(lo), _tiles(cv), _tiles(cin)
    nt = len(cvt)
    zero = jnp.zeros_like(cvt[0])
    lastt = [_row_of_tile(t, V7X_SUBLANES - 1) for t in cint]
    out = []
    bt = nt
    while bt >= 2:
        ht = bt // 2
        hi_rows, lo_rows = [], []
        for j in range(nt):
            b0 = (j // bt) * bt
            ref = lastt[b0 + ht - 1]
            if j - b0 >= ht:
                hi_rows.append(jnp.exp2(cvt[j] - ref) * upt[j])
                lo_rows.append(zero)
            else:
                hi_rows.append(zero)
                lo_rows.append(jnp.exp2(ref - cvt[j]) * lot[j])
        out.append((jnp.concatenate(hi_rows, axis=0), jnp.concatenate(lo_rows, axis=0)))
        bt = ht
    sub = lax.broadcasted_iota(jnp.int32, zero.shape, 0)
    for size in (8, 4):
        upper = (sub & (size // 2)) != 0
        sign = jnp.where(upper, 1.0, -1.0)
        hi_rows, lo_rows = [], []
        for j in range(nt):
            if size == 8:
                ref = _row_of_tile(cint[j], 3)
            else:
                ref = jnp.where(sub < 4, _row_of_tile(cint[j], 1), _row_of_tile(cint[j], 5))
            z = jnp.exp2((cvt[j] - ref) * sign) * jnp.where(upper, upt[j], lot[j])
            hi_rows.append(jnp.where(upper, z, 0.0))
            lo_rows.append(jnp.where(upper, 0.0, z))
        out.append((jnp.concatenate(hi_rows, axis=0), jnp.concatenate(lo_rows, axis=0)))
    return out


def _hg_bwd_local(rows, ci, g_ref, key_ref, v_ref, tri_ref, cumb_ref, kv_ref):
    g = g_ref[rows, :]
    cum = _hg_cumsum(tri_ref, g)
    cumb_ref[rows, :] = cum
    k_dec = key_ref[rows, :].astype(F32) * jnp.exp2(cum - g)
    kv_ref[ci] = _dot_tn(v_ref[rows, :], k_dec.astype(BF16))


def _hg_step_out(rows_list, cis, q_ref, gf_ref, kf_ref, gb_ref, kb_ref, v_ref, tri_ref, cumb_ref,
                 sbs_ref, sf_ref, out_ref, pair_xor):
    n = len(cis)
    q = [q_ref[r, :].astype(F32) for r in rows_list]
    kf = [kf_ref[r, :].astype(F32) for r in rows_list]
    kb = [kb_ref[r, :].astype(F32) for r in rows_list]
    v = [v_ref[r, :] for r in rows_list]
    gf = [gf_ref[r, :] for r in rows_list]
    gb = [gb_ref[r, :] for r in rows_list]
    c = q[0].shape[0]

    cumf = [_hg_cumsum(tri_ref, g) for g in gf]
    cumb = [cumb_ref[r, :] for r in rows_list]
    cumxb = [cb - g for cb, g in zip(cumb, gb)]
    totf = [cf[c - 1:, :] for cf in cumf]
    totb = [cb[c - 1:, :] for cb in cumb]

    kvs = [_dot_tn(v[u], (kf[u] * jnp.exp2(totf[u] - cumf[u])).astype(BF16)) for u in range(n)]
    sf = [sf_ref[...]]
    for u in range(n):
        sf.append(sf[u] * jnp.exp2(totf[u]) + kvs[u])
    sf_ref[...] = sf[n]

    odd = (lax.broadcasted_iota(jnp.int32, q[0].shape, 0) & 1) != 0
    scores = []
    for u in range(n):
        lev_f = _hg_levels(q[u], kf[u], cumf[u], cumf[u])
        lev_b = _hg_levels(kb[u], q[u], cumxb[u], cumb[u])
        lev_f.append((jnp.where(odd, q[u] * jnp.exp2(gf[u]), 0.0), jnp.where(odd, 0.0, kf[u])))
        lev_b.append((jnp.where(odd, kb[u], 0.0), jnp.where(odd, 0.0, q[u] * jnp.exp2(gb[u]))))
        a = None
        size = c
        for (xf, yf), (yb, xb) in zip(lev_f, lev_b):
            x = jnp.concatenate([xf.astype(BF16), xb.astype(BF16)], axis=1)
            y = jnp.concatenate([yf.astype(BF16), yb.astype(BF16)], axis=1)
            p = _dot_nt(x, y)
            a = p if a is None else jnp.where(pair_xor < size, p, a)
            size //= 2
        scores.append(a.astype(BF16))

    inter = []
    for u in range(n):
        q_dec = jnp.concatenate([(q[u] * jnp.exp2(cumf[u])).astype(BF16),
                                 (q[u] * jnp.exp2(totb[u] - cumxb[u])).astype(BF16)], axis=1)
        states = jnp.concatenate([sf[u].astype(BF16), sbs_ref[cis[u]].astype(BF16)], axis=1)
        inter.append(_dot_nt(q_dec, states))

    for u in range(n):
        o = _dot(scores[u], v[u]) + inter[u]
        o += jnp.sum(q[u] * (kf[u] + kb[u]), axis=-1, keepdims=True) * v[u].astype(F32)
        out_ref[rows_list[u], :] = o.astype(out_ref.dtype)


def _hg_kernel(q_ref, gf_ref, kf_ref, gb_ref, kb_ref, v_ref,
               qc_ref, gfc_ref, kfc_ref, gbc_ref, kbc_ref, vc_ref,
               o_ref, oc_ref, tri_ref, cumb_ref, sbs_ref, sf_ref, sb_ref):
    c = HG_CHUNK
    n_i = lax.broadcasted_iota(jnp.int32, (c, c), 0)
    m_i = lax.broadcasted_iota(jnp.int32, (c, c), 1)
    pair_xor = n_i ^ m_i
    tri_ref[...] = jnp.where(m_i <= n_i, 1.0, 0.0).astype(BF16)

    def run(qr_ref, gfr_ref, kfr_ref, gbr_ref, kbr_ref, vr_ref, out_ref):
        n_chunks = qr_ref.shape[0] // c
        unroll = math.gcd(HG_UNROLL, n_chunks)
        n_steps = n_chunks // unroll

        def chunk_rows(ci):
            return pl.ds(pl.multiple_of(ci * c, c), c)

        def bwd_local(i, carry):
            cis = [i * unroll + u for u in range(unroll)]
            gs = [gbr_ref[chunk_rows(ci), :] for ci in cis]
            cums = [_hg_cumsum(tri_ref, g) for g in gs]
            k_decs = [(kbr_ref[chunk_rows(ci), :].astype(F32) * jnp.exp2(cum - g)).astype(BF16)
                      for ci, g, cum in zip(cis, gs, cums)]
            kvs = [_dot_tn(vr_ref[chunk_rows(ci), :], k_dec) for ci, k_dec in zip(cis, k_decs)]
            for ci, cum, kv in zip(cis, cums, kvs):
                cumb_ref[chunk_rows(ci), :] = cum
                sbs_ref[ci] = kv
            return carry
        lax.fori_loop(0, n_steps, bwd_local, 0)

        def bwd_state(i, carry):
            ci = n_chunks - 1 - i
            rows = chunk_rows(ci)
            s = sb_ref[...]
            kv = sbs_ref[ci]
            sbs_ref[ci] = s
            last_tile = cumb_ref[pl.ds(pl.multiple_of(ci * c + c - V7X_SUBLANES, V7X_SUBLANES), V7X_SUBLANES), :]
            sb_ref[...] = s * jnp.exp2(last_tile[V7X_SUBLANES - 1:, :]) + kv
            return carry
        lax.fori_loop(0, n_chunks, bwd_state, 0)

        def fwd(i, carry):
            cis = [i * unroll + u for u in range(unroll)]
            _hg_step_out([chunk_rows(ci) for ci in cis], cis, qr_ref, gfr_ref, kfr_ref, gbr_ref, kbr_ref,
                         vr_ref, tri_ref, cumb_ref, sbs_ref, sf_ref, out_ref, pair_xor)
            return carry
        lax.fori_loop(0, n_steps, fwd, 0)

    sf_ref[...] = jnp.zeros_like(sf_ref)
    sb_ref[...] = jnp.zeros_like(sb_ref)
    run(qc_ref, gfc_ref, kfc_ref, gbc_ref, kbc_ref, vc_ref, oc_ref)
    run(q_ref, gf_ref, kf_ref, gb_ref, kb_ref, v_ref, o_ref)


def _hgrn(qiv, log_f, key, n_batch, seq, ctx_len):
    dk = HG_EXPAND
    heads = log_f.shape[1] // (2 * dk)
    d = heads * dk
    assert seq % HG_CHUNK == 0 and ctx_len % HG_CHUNK == 0 and ctx_len <= seq
    ctx_row0 = (n_batch * seq) // ctx_len
    lat = lambda part: pl.BlockSpec((seq, dk), lambda b, h: (b, part * heads + h))
    ctx = lambda part: pl.BlockSpec((ctx_len, dk), lambda b, h: (ctx_row0 + b, part * heads + h))
    return pl.pallas_call(
        _hg_kernel,
        grid=(n_batch, heads),
        in_specs=[lat(0), lat(0), lat(0), lat(1), lat(1), lat(1),
                  ctx(0), ctx(0), ctx(0), ctx(1), ctx(1), ctx(1)],
        out_specs=[pl.BlockSpec((seq, dk), lambda b, h: (b, h)),
                   pl.BlockSpec((ctx_len, dk), lambda b, h: (b, h))],
        out_shape=[jax.ShapeDtypeStruct((n_batch * seq, d), BF16),
                   jax.ShapeDtypeStruct((n_batch * ctx_len, d), BF16)],
        scratch_shapes=[
            pltpu.VMEM((HG_CHUNK, HG_CHUNK), BF16),
            pltpu.VMEM((seq, dk), F32),
            pltpu.VMEM((seq // HG_CHUNK, dk, dk), F32),
            pltpu.VMEM((dk, dk), F32),
            pltpu.VMEM((dk, dk), F32),
        ],
        compiler_params=_params("parallel", "arbitrary"),
        name="hgrn",
    )(qiv, log_f, key, log_f, key, qiv, qiv, log_f, key, log_f, key, qiv)


def _rope_tables(seq, head_dim):
    quarter = head_dim // 4
    rows = jnp.repeat(jnp.arange(seq // GRID_W, dtype=F32), GRID_W)
    cols = jnp.tile(jnp.arange(GRID_W, dtype=F32), seq // GRID_W)
    inv_freq = ROPE_BASE ** (-jnp.arange(quarter, dtype=F32) / quarter)
    ang = jnp.concatenate([rows[:, None] * inv_freq, cols[:, None] * inv_freq], axis=-1)
    return jnp.cos(ang), jnp.sin(ang)


def kernel(x, c, ctx, c_ctx, ada_w, ada_b, norm1_g, norm2_g, ret_w_in, ret_w_out, ret_decay_logits,
           hg_w_in, hg_w_out, hg_norm_g, hg_lower_bounds, ffn_w_gate_up, ffn_w_down, final_norm_g):
    n_batch, seq, d = x.shape
    ctx_len = ctx.shape[1]
    depth = ada_w.shape[0]
    n_lat = n_batch * seq
    n_all = n_lat + n_batch * ctx_len
    assert seq % BIG_ROW_TILE == 0 and (n_batch * ctx_len) % BIG_ROW_TILE == 0

    xs = jnp.concatenate([x.reshape(n_lat, d), ctx.reshape(n_batch * ctx_len, d)], axis=0)

    cond_rows = -(-(n_batch + 1) // V7X_SUBLANES) * V7X_SUBLANES
    cond = jnp.zeros((cond_rows, d), F32).at[:n_batch].set(c).at[n_batch].set(c_ctx)
    mods = _ada_mod(cond, ada_w, ada_b).reshape(depth, cond_rows, 1, 6 * d)

    lb_p = jax.nn.softmax(hg_lower_bounds.astype(F32), axis=0)
    lower_bounds = jnp.cumsum(lb_p, axis=0) - lb_p[0]
    log_gamma = jax.nn.log_sigmoid(ret_decay_logits.astype(F32))
    cos, sin = _rope_tables(seq, d // RET_HEADS)

    ret_in, ret_out = ret_w_in.astype(BF16), ret_w_out.astype(BF16)
    hg_in, hg_out = hg_w_in.astype(BF16), hg_w_out.astype(BF16)
    ffn_gate_up, ffn_down = ffn_w_gate_up.astype(BF16), ffn_w_down.astype(BF16)

    for layer in range(depth):
        last = layer == depth - 1
        j = layer // N_MIXERS
        mod = mods[layer]
        n_rows = n_lat if last else n_all
        retention = layer % N_MIXERS == 0
        w_in, w_out = (ret_in, ret_out) if retention else (hg_in, hg_out)
        h1 = _prenorm(xs, norm1_g[layer], mod, seq, n_batch)
        tiles_per_part = d // IN_COL_TILE
        if retention:
            proj = _proj(h1, w_in, j, (0, w_in.shape[2], 0), 0)
            o_lat, o_ctx = _retention(proj, log_gamma[j], cos, sin, n_batch, seq, ctx_len)
            norm_gain = jnp.ones((d,), F32)
        else:
            proj = _proj(h1, w_in, j, (0, tiles_per_part, 2 * tiles_per_part), tiles_per_part)
            log_f, key = _proj_gates(h1, w_in, j, lower_bounds[j], tiles_per_part, 2 * tiles_per_part)
            o_lat, o_ctx = _hgrn(proj, log_f, key, n_batch, seq, ctx_len)
            norm_gain = hg_norm_g[j]
        gate_block = proj.shape[1] // d - 1
        xs, h2 = _out_proj(o_lat, o_ctx, proj, gate_block, norm_gain, w_out, j, xs, mod, norm2_g[layer],
                           seq, n_batch, n_rows, not retention)
        xs = _ffn(h2, ffn_gate_up, ffn_down, layer, xs, mod, seq, n_batch, n_rows)
    return _final_norm(xs, final_norm_g).reshape(n_batch, seq, d)
```

```python
import functools
import math

import jax
import jax.numpy as jnp
from jax import lax
from jax.experimental import pallas as pl
from jax.experimental.pallas import tpu as pltpu

F32 = jnp.float32
BF16 = jnp.bfloat16

EPS = 1e-6
LOG2E = 1.4426950408889634
ROPE_BASE = 10000.0
GRID_W = 64
N_MIXERS = 2
RET_HEADS = 8
HG_EXPAND = 128

V7X_LANES = 128
V7X_SUBLANES = 8
V7X_VMEM_BYTES = 64 * 1024 * 1024

ROW_TILE = 512
BIG_ROW_TILE = 1024
IN_COL_TILE = 1024
FFN_HID_TILE = 512
FFN_OUT_TILE = 512
MXU_COLS = 256
ADA_COL_TILE = 1024
RET_CHUNK = 256
RET_UNROLL = 4
HG_CHUNK = 128
HG_UNROLL = 4
ROW_STEP = 64
NORM_STEP = 128
CAST_STEP = 256
VMEM_LIMIT = 56 * 1024 * 1024


def _params(*semantics):
    return pltpu.CompilerParams(dimension_semantics=semantics, vmem_limit_bytes=VMEM_LIMIT)


def _sigmoid(x):
    return 1.0 / (1.0 + jnp.exp(-x))


def _dot(a, b):
    return jnp.dot(a, b, preferred_element_type=F32)


def _dot_nt(a, b):
    return lax.dot_general(a, b, (((1,), (1,)), ((), ())), preferred_element_type=F32)


def _dot_tn(a, b):
    return lax.dot_general(a, b, (((0,), (0,)), ((), ())), preferred_element_type=F32)


def _ada_kernel(c_ref, w_ref, b_ref, o_ref):
    c = c_ref[...]
    a = (c * _sigmoid(c)).astype(BF16)
    o_ref[...] = _dot(a, w_ref[...].astype(BF16)) + b_ref[...]


def _ada_mod(cond, ada_w, ada_b):
    depth, d, n = ada_w.shape
    rows = cond.shape[0]
    return pl.pallas_call(
        _ada_kernel,
        grid=(depth, n // ADA_COL_TILE),
        in_specs=[
            pl.BlockSpec((rows, d), lambda l, j: (0, 0)),
            pl.BlockSpec((None, d, ADA_COL_TILE), lambda l, j: (l, 0, j)),
            pl.BlockSpec((None, 1, ADA_COL_TILE), lambda l, j: (l, 0, j)),
        ],
        out_specs=pl.BlockSpec((None, rows, ADA_COL_TILE), lambda l, j: (l, 0, j)),
        out_shape=jax.ShapeDtypeStruct((depth, rows, n), F32),
        compiler_params=_params("parallel", "parallel"),
        name="ada_mod",
    )(cond, ada_w, ada_b.reshape(depth, 1, n))


def _norm_mod(x, gain, shift, scale):
    return x * lax.rsqrt(jnp.mean(x * x, axis=-1, keepdims=True) + EPS) * (gain * (1.0 + scale)) + shift


def _row_loop(rows, body, step_rows=ROW_STEP):
    def step(i, carry):
        body(pl.ds(pl.multiple_of(i * step_rows, step_rows), step_rows))
        return carry
    lax.fori_loop(0, rows // step_rows, step, 0)


def _mod_spec(width, slot, row_tile, rows_per_batch, n_batch, col_axis=None):
    def index(*ids):
        col = slot if col_axis is None else slot + ids[col_axis]
        return (jnp.minimum((ids[0] * row_tile) // rows_per_batch, n_batch), 0, col)
    return pl.BlockSpec((None, 1, width), index)


def _prenorm_kernel(x_ref, g_ref, sh_ref, sc_ref, h_ref):
    def body(rows):
        h = _norm_mod(x_ref[rows, :], g_ref[...], sh_ref[...], sc_ref[...])
        h_ref[rows, :] = h.astype(BF16)
    _row_loop(ROW_TILE, body, NORM_STEP)


def _prenorm(x, gain, mod, rows_per_batch, n_batch):
    r, d = x.shape
    tm = ROW_TILE
    return pl.pallas_call(
        _prenorm_kernel,
        grid=(r // tm,),
        in_specs=[
            pl.BlockSpec((tm, d), lambda i: (i, 0)),
            pl.BlockSpec((1, d), lambda i: (0, 0)),
            _mod_spec(d, 0, tm, rows_per_batch, n_batch),
            _mod_spec(d, 1, tm, rows_per_batch, n_batch),
        ],
        out_specs=pl.BlockSpec((tm, d), lambda i: (i, 0)),
        out_shape=jax.ShapeDtypeStruct((r, d), BF16),
        compiler_params=_params("parallel"),
        name="prenorm",
    )(x, gain.reshape(1, d), mod, mod)


def _col_tiles():
    return [slice(t * MXU_COLS, (t + 1) * MXU_COLS) for t in range(IN_COL_TILE // MXU_COLS)]


def _cast_weight_tile(w_ref, wb_ref):
    @pl.when(pl.program_id(1) == 0)
    def _():
        def body(rows):
            wb_ref[rows, :] = w_ref[rows, :].astype(BF16)
        _row_loop(w_ref.shape[0], body, CAST_STEP)


def _proj_kernel(h_ref, w_ref, o_ref, wb_ref, *, n_silu):
    _cast_weight_tile(w_ref, wb_ref)
    h = h_ref[...]

    def run(silu):
        for cols in _col_tiles():
            a = _dot(h, wb_ref[:, cols])
            if silu:
                a = a * _sigmoid(a)
            o_ref[:, cols] = a.astype(o_ref.dtype)

    if n_silu == 0:
        run(False)
    else:
        j = pl.program_id(0)
        pl.when(j < n_silu)(functools.partial(run, True))
        pl.when(j >= n_silu)(functools.partial(run, False))


def _proj(h, w, layer, col_tiles, n_silu):
    r, d = h.shape
    tm, tn = BIG_ROW_TILE, IN_COL_TILE
    first, skip_from, skip = col_tiles
    n_tiles = w.shape[2] // tn - first - skip
    wcol = lambda j, i: (layer, 0, first + j + jnp.where(j >= skip_from, skip, 0))
    return pl.pallas_call(
        functools.partial(_proj_kernel, n_silu=n_silu),
        grid=(n_tiles, r // tm),
        in_specs=[pl.BlockSpec((tm, d), lambda j, i: (i, 0)), pl.BlockSpec((None, d, tn), wcol)],
        out_specs=pl.BlockSpec((tm, tn), lambda j, i: (i, j)),
        out_shape=jax.ShapeDtypeStruct((r, n_tiles * tn), BF16),
        scratch_shapes=[pltpu.VMEM((d, tn), BF16)],
        compiler_params=_params("arbitrary", "arbitrary"),
        name="in_proj",
    )(h, w)


def _proj_gates_kernel(h_ref, w_ref, lb_ref, g_ref, key_ref, wb_ref):
    _cast_weight_tile(w_ref, wb_ref)
    h = h_ref[...]
    for cols in _col_tiles():
        _, g, key = _hg_gates(_dot(h, wb_ref[:, cols]), lb_ref[:, cols])
        g_ref[:, cols] = g
        key_ref[:, cols] = key.astype(key_ref.dtype)


def _proj_gates(h, w, layer, lower_bound, first_tile, n_tiles):
    r, d = h.shape
    tm, tn = BIG_ROW_TILE, IN_COL_TILE
    lb_tiles = lower_bound.shape[0] // tn
    out = pl.BlockSpec((tm, tn), lambda j, i: (i, j))
    return pl.pallas_call(
        _proj_gates_kernel,
        grid=(n_tiles, r // tm),
        in_specs=[pl.BlockSpec((tm, d), lambda j, i: (i, 0)),
                  pl.BlockSpec((None, d, tn), lambda j, i: (layer, 0, first_tile + j)),
                  pl.BlockSpec((1, tn), lambda j, i: (0, j % lb_tiles))],
        out_specs=[out, out],
        out_shape=[jax.ShapeDtypeStruct((r, n_tiles * tn), F32),
                   jax.ShapeDtypeStruct((r, n_tiles * tn), BF16)],
        scratch_shapes=[pltpu.VMEM((d, tn), BF16)],
        compiler_params=_params("arbitrary", "arbitrary"),
        name="in_proj_gates",
    )(h, w, lower_bound.reshape(1, -1))


def _outproj_kernel(ol_ref, oc_ref, g_ref, ng_ref, w_ref, x_ref, gt_ref, n2_ref, sh_ref, sc_ref,
                    xo_ref, h_ref, y_ref, *, hgrn, n_lat_tiles):
    def gate_from(o_ref):
        def gate(rows):
            o = o_ref[rows, :].astype(F32)
            g = g_ref[rows, :].astype(F32)
            if hgrn:
                o = o * lax.rsqrt(jnp.mean(o * o, axis=-1, keepdims=True) + EPS) * ng_ref[...]
                y = o * _sigmoid(g)
            else:
                y = o * (g * _sigmoid(g))
            y_ref[rows, :] = y.astype(BF16)
        _row_loop(ROW_TILE, gate)

    tile = pl.program_id(0)
    pl.when(tile < n_lat_tiles)(functools.partial(gate_from, ol_ref))
    pl.when(tile >= n_lat_tiles)(functools.partial(gate_from, oc_ref))

    y = y_ref[...]
    for t in range(w_ref.shape[1] // MXU_COLS):
        cols = slice(t * MXU_COLS, (t + 1) * MXU_COLS)
        xo_ref[:, cols] = x_ref[:, cols] + gt_ref[:, cols] * _dot(y, w_ref[:, cols])

    def norm(rows):
        h_ref[rows, :] = _norm_mod(xo_ref[rows, :], n2_ref[...], sh_ref[...], sc_ref[...]).astype(BF16)
    _row_loop(ROW_TILE, norm, NORM_STEP)


def _out_proj(o_lat, o_ctx, proj, gate_block, norm_gain, w, layer, x, mod, gain2, rows_per_batch, n_batch,
              n_rows, hgrn):
    d = x.shape[1]
    tm = ROW_TILE
    n_lat_tiles = o_lat.shape[0] // tm
    row = lambda i: (i, 0)
    const = lambda i: (0, 0)
    return pl.pallas_call(
        functools.partial(_outproj_kernel, hgrn=hgrn, n_lat_tiles=n_lat_tiles),
        grid=(n_rows // tm,),
        in_specs=[
            pl.BlockSpec((tm, d), lambda i: (jnp.minimum(i, n_lat_tiles - 1), 0)),
            pl.BlockSpec((tm, d), lambda i: (jnp.maximum(i - n_lat_tiles, 0), 0)),
            pl.BlockSpec((tm, d), lambda i: (i, gate_block)),
            pl.BlockSpec((1, d), const),
            pl.BlockSpec((None, d, d), lambda i: (layer, 0, 0)),
            pl.BlockSpec((tm, d), row),
            _mod_spec(d, 2, tm, rows_per_batch, n_batch),
            pl.BlockSpec((1, d), const),
            _mod_spec(d, 3, tm, rows_per_batch, n_batch),
            _mod_spec(d, 4, tm, rows_per_batch, n_batch),
        ],
        out_specs=[pl.BlockSpec((tm, d), row), pl.BlockSpec((tm, d), row)],
        out_shape=[jax.ShapeDtypeStruct((n_rows, d), F32), jax.ShapeDtypeStruct((n_rows, d), BF16)],
        scratch_shapes=[pltpu.VMEM((tm, d), BF16)],
        compiler_params=_params("parallel"),
        name="out_proj",
    )(o_lat, o_ctx, proj, norm_gain.reshape(1, d), w, x, mod, gain2.reshape(1, d), mod, mod)


def _gateup_kernel(h_ref, wg_ref, wu_ref, p_ref, wgb_ref, wub_ref):
    _cast_weight_tile(wg_ref, wgb_ref)
    _cast_weight_tile(wu_ref, wub_ref)
    h = h_ref[...]
    for t in range(FFN_HID_TILE // MXU_COLS):
        cols = slice(t * MXU_COLS, (t + 1) * MXU_COLS)
        a = _dot(h, wgb_ref[:, cols])
        b = _dot(h, wub_ref[:, cols])
        p_ref[:, cols] = (a * _sigmoid(a) * b).astype(BF16)


def _down_kernel(p_ref, wd_ref, x_ref, gt_ref, o_ref):
    o_ref[...] = x_ref[...] + gt_ref[...] * _dot(p_ref[...], wd_ref[...])


def _ffn(h2, w_gate_up, w_down, layer, x, mod, rows_per_batch, n_batch, n_rows):
    d = x.shape[1]
    hidden = w_down.shape[1]
    tm = BIG_ROW_TILE
    n_hid = hidden // FFN_HID_TILE
    p = pl.pallas_call(
        _gateup_kernel,
        grid=(n_hid, n_rows // tm),
        in_specs=[
            pl.BlockSpec((tm, d), lambda j, i: (i, 0)),
            pl.BlockSpec((None, d, FFN_HID_TILE), lambda j, i: (layer, 0, j)),
            pl.BlockSpec((None, d, FFN_HID_TILE), lambda j, i: (layer, 0, j + n_hid)),
        ],
        out_specs=pl.BlockSpec((tm, FFN_HID_TILE), lambda j, i: (i, j)),
        out_shape=jax.ShapeDtypeStruct((n_rows, hidden), BF16),
        scratch_shapes=[pltpu.VMEM((d, FFN_HID_TILE), BF16), pltpu.VMEM((d, FFN_HID_TILE), BF16)],
        compiler_params=_params("arbitrary", "arbitrary"),
        name="ffn_gate_up",
    )(h2, w_gate_up, w_gate_up)
    tn = FFN_OUT_TILE
    return pl.pallas_call(
        _down_kernel,
        grid=(n_rows // tm, d // tn),
        in_specs=[
            pl.BlockSpec((tm, hidden), lambda i, j: (i, 0)),
            pl.BlockSpec((None, hidden, tn), lambda i, j: (layer, 0, j)),
            pl.BlockSpec((tm, tn), lambda i, j: (i, j)),
            _mod_spec(tn, 5 * (d // tn), tm, rows_per_batch, n_batch, col_axis=1),
        ],
        out_specs=pl.BlockSpec((tm, tn), lambda i, j: (i, j)),
        out_shape=jax.ShapeDtypeStruct((n_rows, d), F32),
        compiler_params=_params("parallel", "arbitrary"),
        name="ffn_down",
    )(p, w_down, x, mod)


def _final_norm_kernel(x_ref, g_ref, o_ref):
    def body(rows):
        x = x_ref[rows, :]
        o_ref[rows, :] = x * lax.rsqrt(jnp.mean(x * x, axis=-1, keepdims=True) + EPS) * g_ref[...]
    _row_loop(ROW_TILE, body, NORM_STEP)


def _final_norm(x, gain):
    r, d = x.shape
    return pl.pallas_call(
        _final_norm_kernel,
        grid=(r // ROW_TILE,),
        in_specs=[pl.BlockSpec((ROW_TILE, d), lambda i: (i, 0)), pl.BlockSpec((1, d), lambda i: (0, 0))],
        out_specs=pl.BlockSpec((ROW_TILE, d), lambda i: (i, 0)),
        out_shape=jax.ShapeDtypeStruct((r, d), F32),
        compiler_params=_params("parallel"),
        name="final_norm",
    )(x, gain.reshape(1, d))


def _ret_kernel(lg_ref, q_ref, k_ref, v_ref, qc_ref, kc_ref, vc_ref, cos_ref, sin_ref,
                o_ref, oc_ref, dec_ref, kr_ref, sbs_ref, sf_ref, sb_ref, *, k_scale):
    c = RET_CHUNK
    dk = q_ref.shape[1]
    half = dk // 2
    n_chunks = q_ref.shape[0] // c
    head = pl.program_id(1)
    lgf = lg_ref[0, head]
    lgb = lg_ref[1, head]

    n_i = lax.broadcasted_iota(jnp.int32, (c, c), 0).astype(F32)
    m_i = lax.broadcasted_iota(jnp.int32, (c, c), 1).astype(F32)
    diff = n_i - m_i
    dec_ref[0] = (jnp.where(diff >= 0, jnp.exp(lgf * jnp.maximum(diff, 0.0)), 0.0)
                  + jnp.where(diff <= 0, jnp.exp(lgb * jnp.maximum(-diff, 0.0)), 0.0))
    t_i = lax.broadcasted_iota(jnp.int32, (c, dk), 0).astype(F32)
    dec_ref[1] = jnp.exp(lgf * (t_i + 1.0))
    dec_ref[2] = jnp.exp(lgb * (c - t_i))
    dec_ref[3] = jnp.exp(lgf * (c - 1.0 - t_i))
    dec_ref[4] = jnp.exp(lgb * t_i)
    cf = jnp.exp(lgf * c)
    cb = jnp.exp(lgb * c)

    def head_norm(o):
        return o * lax.rsqrt(jnp.mean(o * o, axis=-1, keepdims=True) + EPS)

    def rope(t, rows):
        cos = cos_ref[rows, :]
        sin = sin_ref[rows, :]
        t1 = t[:, :half]
        t2 = t[:, half:]
        return jnp.concatenate([t1 * cos - t2 * sin, t1 * sin + t2 * cos], axis=-1)

    qc = qc_ref[...]
    kc = (kc_ref[...].astype(F32) * k_scale)
    vc = vc_ref[...]
    sc = _dot_nt(qc, kc.astype(BF16)) * dec_ref[0]
    oc_ref[...] = head_norm(_dot(sc.astype(BF16), vc)).astype(oc_ref.dtype)
    sf_ref[...] = _dot_tn((kc * dec_ref[3]).astype(BF16), vc)
    sb_ref[...] = _dot_tn((kc * dec_ref[4]).astype(BF16), vc)

    unroll = math.gcd(RET_UNROLL, n_chunks)
    n_steps = n_chunks // unroll

    def chunk_rows(ci):
        return pl.ds(pl.multiple_of(ci * c, c), c)

    def bwd_local(i, carry):
        cis = [i * unroll + u for u in range(unroll)]
        rows = [chunk_rows(ci) for ci in cis]
        krs = [rope(k_ref[r, :].astype(F32), r) * k_scale for r in rows]
        kvs = [_dot_tn((kr * dec_ref[4]).astype(BF16), v_ref[r, :]) for kr, r in zip(krs, rows)]
        for ci, r, kr, kv in zip(cis, rows, krs, kvs):
            kr_ref[r, :] = kr.astype(BF16)
            sbs_ref[ci] = kv
        return carry
    lax.fori_loop(0, n_steps, bwd_local, 0)

    def bwd_state(i, carry):
        ci = n_chunks - 1 - i
        s = sb_ref[...]
        kv = sbs_ref[ci]
        sbs_ref[ci] = s
        sb_ref[...] = s * cb + kv
        return carry
    lax.fori_loop(0, n_chunks, bwd_state, 0)

    def fwd(i, carry):
        cis = [i * unroll + u for u in range(unroll)]
        rows = [chunk_rows(ci) for ci in cis]
        qrs = [rope(q_ref[r, :].astype(F32), r).astype(BF16) for r in rows]
        krs = [kr_ref[r, :] for r in rows]
        vs = [v_ref[r, :] for r in rows]
        scs = [_dot_nt(qr, kr) for qr, kr in zip(qrs, krs)]
        kvs = [_dot_tn((kr.astype(F32) * dec_ref[3]).astype(BF16), v) for kr, v in zip(krs, vs)]
        sfs = [sf_ref[...]]
        for kv in kvs:
            sfs.append(sfs[-1] * cf + kv)
        sf_ref[...] = sfs[-1]
        intra = [_dot((sc * dec_ref[0]).astype(BF16), v) for sc, v in zip(scs, vs)]
        inter_f = [_dot(qr, s.astype(BF16)) for qr, s in zip(qrs, sfs)]
        inter_b = [_dot(qr, sbs_ref[ci].astype(BF16)) for qr, ci in zip(qrs, cis)]
        for r, o1, o2, o3 in zip(rows, intra, inter_f, inter_b):
            o = o1 + dec_ref[1] * o2 + dec_ref[2] * o3
            o_ref[r, :] = head_norm(o).astype(o_ref.dtype)
        return carry
    lax.fori_loop(0, n_steps, fwd, 0)


def _retention(proj, log_gamma, cos, sin, n_batch, seq, ctx_len):
    heads = RET_HEADS
    dk = proj.shape[1] // (4 * heads)
    d = heads * dk
    assert dk == RET_CHUNK and ctx_len == RET_CHUNK and seq % RET_CHUNK == 0
    ctx_row0 = (n_batch * seq) // ctx_len
    lat = lambda part: pl.BlockSpec((seq, dk), lambda b, h: (b, part * heads + h))
    ctx = lambda part: pl.BlockSpec((ctx_len, dk), lambda b, h: (ctx_row0 + b, part * heads + h))
    tab = pl.BlockSpec((seq, dk // 2), lambda b, h: (0, 0))
    n_chunks = seq // RET_CHUNK
    return pl.pallas_call(
        functools.partial(_ret_kernel, k_scale=dk ** -0.5),
        grid=(n_batch, heads),
        in_specs=[pl.BlockSpec(memory_space=pltpu.SMEM),
                  lat(0), lat(1), lat(2), ctx(0), ctx(1), ctx(2), tab, tab],
        out_specs=[pl.BlockSpec((seq, dk), lambda b, h: (b, h)),
                   pl.BlockSpec((ctx_len, dk), lambda b, h: (b, h))],
        out_shape=[jax.ShapeDtypeStruct((n_batch * seq, d), BF16),
                   jax.ShapeDtypeStruct((n_batch * ctx_len, d), BF16)],
        scratch_shapes=[
            pltpu.VMEM((5, RET_CHUNK, RET_CHUNK), F32),
            pltpu.VMEM((seq, dk), BF16),
            pltpu.VMEM((n_chunks, dk, dk), F32),
            pltpu.VMEM((dk, dk), F32),
            pltpu.VMEM((dk, dk), F32),
        ],
        compiler_params=_params("parallel", "arbitrary"),
        name="retention",
    )(log_gamma, proj, proj, proj, proj, proj, proj, cos, sin)


def _hg_gates(z, lb):
    f = lb + (1.0 - lb) * _sigmoid(z)
    return f, jnp.log2(f), 1.0 - f


def _hg_cumsum(tri_ref, g):
    dk = g.shape[1]
    hi = g.astype(BF16)
    lo = (g - hi.astype(F32)).astype(BF16)
    r = _dot(tri_ref[...], jnp.concatenate([hi, lo], axis=1))
    return r[:, :dk] + r[:, dk:]


def _tiles(x):
    return [x[j * V7X_SUBLANES:(j + 1) * V7X_SUBLANES] for j in range(x.shape[0] // V7X_SUBLANES)]


def _row_of_tile(tile, r):
    return jnp.broadcast_to(tile[r:r + 1, :], tile.shape)


def _hg_levels(q, kf, kb, gf, gb, cumf, cumb):
    cumxb = cumb - gb
    qt, kft, kbt = _tiles(q), _tiles(kf), _tiles(kb)
    cft, cxt, cbt = _tiles(cumf), _tiles(cumxb), _tiles(cumb)
    nt = len(qt)
    lastf = [_row_of_tile(t, V7X_SUBLANES - 1) for t in cft]
    lastb = [_row_of_tile(t, V7X_SUBLANES - 1) for t in cbt]
    out = []
    bt = nt
    while bt >= 2:
        ht = bt // 2
        xs, ys = [], []
        for j in range(nt):
            mid = (j // bt) * bt + ht - 1
            if j % bt >= ht:
                xs.append(jnp.exp2(cft[j] - lastf[mid]) * qt[j])
                ys.append(jnp.exp2(cxt[j] - lastb[mid]) * kbt[j])
            else:
                xs.append(jnp.exp2(lastb[mid] - cxt[j]) * qt[j])
                ys.append(jnp.exp2(lastf[mid] - cft[j]) * kft[j])
        out.append((jnp.concatenate(xs, axis=0).astype(BF16), jnp.concatenate(ys, axis=0).astype(BF16)))
        bt = ht
    sub = lax.broadcasted_iota(jnp.int32, qt[0].shape, 0)
    for size in (8, 4):
        upper = (sub & (size // 2)) != 0
        sign = jnp.where(upper, 1.0, -1.0)
        xs, ys = [], []
        for j in range(nt):
            if size == 8:
                ref_f, ref_b = _row_of_tile(cft[j], 3), _row_of_tile(cbt[j], 3)
            else:
                ref_f = jnp.where(sub < 4, _row_of_tile(cft[j], 1), _row_of_tile(cft[j], 5))
                ref_b = jnp.where(sub < 4, _row_of_tile(cbt[j], 1), _row_of_tile(cbt[j], 5))
            zf = jnp.exp2((cft[j] - ref_f) * sign) * jnp.where(upper, qt[j], kft[j])
            zb = jnp.exp2((cxt[j] - ref_b) * sign) * jnp.where(upper, kbt[j], qt[j])
            xs.append(jnp.where(upper, zf, zb))
            ys.append(jnp.where(upper, zb, zf))
        out.append((jnp.concatenate(xs, axis=0).astype(BF16), jnp.concatenate(ys, axis=0).astype(BF16)))
    odd = (lax.broadcasted_iota(jnp.int32, q.shape, 0) & 1) != 0
    out.append(((q * jnp.exp2(jnp.where(odd, gf, gb))).astype(BF16), jnp.where(odd, kb, kf).astype(BF16)))
    return out


def _hg_bwd_local(rows, ci, g_ref, key_ref, v_ref, tri_ref, cumb_ref, kv_ref):
    g = g_ref[rows, :]
    cum = _hg_cumsum(tri_ref, g)
    cumb_ref[rows, :] = cum
    k_dec = key_ref[rows, :].astype(F32) * jnp.exp2(cum - g)
    kv_ref[ci] = _dot_tn(v_ref[rows, :], k_dec.astype(BF16))


def _hg_step_out(rows_list, cis, q_ref, gf_ref, kf_ref, gb_ref, kb_ref, v_ref, tri_ref, cumb_ref,
                 sbs_ref, sf_ref, out_ref, pair_xor):
    n = len(cis)
    q = [q_ref[r, :].astype(F32) for r in rows_list]
    kf = [kf_ref[r, :].astype(F32) for r in rows_list]
    kb = [kb_ref[r, :].astype(F32) for r in rows_list]
    v = [v_ref[r, :] for r in rows_list]
    gf = [gf_ref[r, :] for r in rows_list]
    gb = [gb_ref[r, :] for r in rows_list]
    c = q[0].shape[0]

    cumf = [_hg_cumsum(tri_ref, g) for g in gf]
    cumb = [cumb_ref[r, :] for r in rows_list]
    cumxb = [cb - g for cb, g in zip(cumb, gb)]
    totf = [cf[c - 1:, :] for cf in cumf]
    totb = [cb[c - 1:, :] for cb in cumb]

    kvs = [_dot_tn(v[u], (kf[u] * jnp.exp2(totf[u] - cumf[u])).astype(BF16)) for u in range(n)]
    sf = [sf_ref[...]]
    for u in range(n):
        sf.append(sf[u] * jnp.exp2(totf[u]) + kvs[u])
    sf_ref[...] = sf[n]

    scores = []
    for u in range(n):
        a = None
        size = c
        for x, y in _hg_levels(q[u], kf[u], kb[u], gf[u], gb[u], cumf[u], cumb[u]):
            p = _dot_nt(x, y)
            a = p if a is None else jnp.where(pair_xor < size, p, a)
            size //= 2
        scores.append(jnp.where(pair_xor == 0, 0.0, a).astype(BF16))

    inter = []
    for u in range(n):
        q_dec = jnp.concatenate([(q[u] * jnp.exp2(cumf[u])).astype(BF16),
                                 (q[u] * jnp.exp2(totb[u] - cumxb[u])).astype(BF16)], axis=1)
        states = jnp.concatenate([sf[u].astype(BF16), sbs_ref[cis[u]].astype(BF16)], axis=1)
        inter.append(_dot_nt(q_dec, states))

    for u in range(n):
        o = _dot(scores[u], v[u]) + inter[u]
        o += jnp.sum(q[u] * (kf[u] + kb[u]), axis=-1, keepdims=True) * v[u].astype(F32)
        out_ref[rows_list[u], :] = o.astype(out_ref.dtype)


def _hg_kernel(q_ref, gf_ref, kf_ref, gb_ref, kb_ref, v_ref,
               qc_ref, gfc_ref, kfc_ref, gbc_ref, kbc_ref, vc_ref,
               o_ref, oc_ref, tri_ref, cumb_ref, sbs_ref, sf_ref, sb_ref):
    c = HG_CHUNK
    n_i = lax.broadcasted_iota(jnp.int32, (c, c), 0)
    m_i = lax.broadcasted_iota(jnp.int32, (c, c), 1)
    pair_xor = n_i ^ m_i
    tri_ref[...] = jnp.where(m_i <= n_i, 1.0, 0.0).astype(BF16)

    def run(qr_ref, gfr_ref, kfr_ref, gbr_ref, kbr_ref, vr_ref, out_ref):
        n_chunks = qr_ref.shape[0] // c
        unroll = math.gcd(HG_UNROLL, n_chunks)
        n_steps = n_chunks // unroll

        def chunk_rows(ci):
            return pl.ds(pl.multiple_of(ci * c, c), c)

        def bwd_local(i, carry):
            cis = [i * unroll + u for u in range(unroll)]
            gs = [gbr_ref[chunk_rows(ci), :] for ci in cis]
            cums = [_hg_cumsum(tri_ref, g) for g in gs]
            k_decs = [(kbr_ref[chunk_rows(ci), :].astype(F32) * jnp.exp2(cum - g)).astype(BF16)
                      for ci, g, cum in zip(cis, gs, cums)]
            kvs = [_dot_tn(vr_ref[chunk_rows(ci), :], k_dec) for ci, k_dec in zip(cis, k_decs)]
            for ci, cum, kv in zip(cis, cums, kvs):
                cumb_ref[chunk_rows(ci), :] = cum
                sbs_ref[ci] = kv
            return carry
        lax.fori_loop(0, n_steps, bwd_local, 0)

        def bwd_state(i, carry):
            ci = n_chunks - 1 - i
            rows = chunk_rows(ci)
            s = sb_ref[...]
            kv = sbs_ref[ci]
            sbs_ref[ci] = s
            last_tile = cumb_ref[pl.ds(pl.multiple_of(ci * c + c - V7X_SUBLANES, V7X_SUBLANES), V7X_SUBLANES), :]
            sb_ref[...] = s * jnp.exp2(last_tile[V7X_SUBLANES - 1:, :]) + kv
            return carry
        lax.fori_loop(0, n_chunks, bwd_state, 0)

        def fwd(i, carry):
            cis = [i * unroll + u for u in range(unroll)]
            _hg_step_out([chunk_rows(ci) for ci in cis], cis, qr_ref, gfr_ref, kfr_ref, gbr_ref, kbr_ref,
                         vr_ref, tri_ref, cumb_ref, sbs_ref, sf_ref, out_ref, pair_xor)
            return carry
        lax.fori_loop(0, n_steps, fwd, 0)

    sf_ref[...] = jnp.zeros_like(sf_ref)
    sb_ref[...] = jnp.zeros_like(sb_ref)
    run(qc_ref, gfc_ref, kfc_ref, gbc_ref, kbc_ref, vc_ref, oc_ref)
    run(q_ref, gf_ref, kf_ref, gb_ref, kb_ref, v_ref, o_ref)


def _hgrn(qiv, log_f, key, n_batch, seq, ctx_len):
    dk = HG_EXPAND
    heads = log_f.shape[1] // (2 * dk)
    d = heads * dk
    assert seq % HG_CHUNK == 0 and ctx_len % HG_CHUNK == 0 and ctx_len <= seq
    ctx_row0 = (n_batch * seq) // ctx_len
    lat = lambda part: pl.BlockSpec((seq, dk), lambda b, h: (b, part * heads + h))
    ctx = lambda part: pl.BlockSpec((ctx_len, dk), lambda b, h: (ctx_row0 + b, part * heads + h))
    return pl.pallas_call(
        _hg_kernel,
        grid=(n_batch, heads),
        in_specs=[lat(0), lat(0), lat(0), lat(1), lat(1), lat(1),
                  ctx(0), ctx(0), ctx(0), ctx(1), ctx(1), ctx(1)],
        out_specs=[pl.BlockSpec((seq, dk), lambda b, h: (b, h)),
                   pl.BlockSpec((ctx_len, dk), lambda b, h: (b, h))],
        out_shape=[jax.ShapeDtypeStruct((n_batch * seq, d), BF16),
                   jax.ShapeDtypeStruct((n_batch * ctx_len, d), BF16)],
        scratch_shapes=[
            pltpu.VMEM((HG_CHUNK, HG_CHUNK), BF16),
            pltpu.VMEM((seq, dk), F32),
            pltpu.VMEM((seq // HG_CHUNK, dk, dk), F32),
            pltpu.VMEM((dk, dk), F32),
            pltpu.VMEM((dk, dk), F32),
        ],
        compiler_params=_params("parallel", "arbitrary"),
        name="hgrn",
    )(qiv, log_f, key, log_f, key, qiv, qiv, log_f, key, log_f, key, qiv)


def _rope_tables(seq, head_dim):
    quarter = head_dim // 4
    rows = jnp.repeat(jnp.arange(seq // GRID_W, dtype=F32), GRID_W)
    cols = jnp.tile(jnp.arange(GRID_W, dtype=F32), seq // GRID_W)
    inv_freq = ROPE_BASE ** (-jnp.arange(quarter, dtype=F32) / quarter)
    ang = jnp.concatenate([rows[:, None] * inv_freq, cols[:, None] * inv_freq], axis=-1)
    return jnp.cos(ang), jnp.sin(ang)


def kernel(x, c, ctx, c_ctx, ada_w, ada_b, norm1_g, norm2_g, ret_w_in, ret_w_out, ret_decay_logits,
           hg_w_in, hg_w_out, hg_norm_g, hg_lower_bounds, ffn_w_gate_up, ffn_w_down, final_norm_g):
    n_batch, seq, d = x.shape
    ctx_len = ctx.shape[1]
    depth = ada_w.shape[0]
    n_lat = n_batch * seq
    n_all = n_lat + n_batch * ctx_len
    assert seq % BIG_ROW_TILE == 0 and (n_batch * ctx_len) % BIG_ROW_TILE == 0

    xs = jnp.concatenate([x.reshape(n_lat, d), ctx.reshape(n_batch * ctx_len, d)], axis=0)

    cond_rows = -(-(n_batch + 1) // V7X_SUBLANES) * V7X_SUBLANES
    cond = jnp.zeros((cond_rows, d), F32).at[:n_batch].set(c).at[n_batch].set(c_ctx)
    mods = _ada_mod(cond, ada_w, ada_b).reshape(depth, cond_rows, 1, 6 * d)

    lb_p = jax.nn.softmax(hg_lower_bounds.astype(F32), axis=0)
    lower_bounds = jnp.cumsum(lb_p, axis=0) - lb_p[0]
    log_gamma = jax.nn.log_sigmoid(ret_decay_logits.astype(F32))
    cos, sin = _rope_tables(seq, d // RET_HEADS)

    ret_in, ret_out = ret_w_in, ret_w_out.astype(BF16)
    hg_in, hg_out = hg_w_in, hg_w_out.astype(BF16)
    ffn_gate_up, ffn_down = ffn_w_gate_up, ffn_w_down.astype(BF16)

    for layer in range(depth):
        last = layer == depth - 1
        j = layer // N_MIXERS
        mod = mods[layer]
        n_rows = n_lat if last else n_all
        retention = layer % N_MIXERS == 0
        w_in, w_out = (ret_in, ret_out) if retention else (hg_in, hg_out)
        h1 = _prenorm(xs, norm1_g[layer], mod, seq, n_batch)
        tiles_per_part = d // IN_COL_TILE
        if retention:
            proj = _proj(h1, w_in, j, (0, w_in.shape[2], 0), 0)
            o_lat, o_ctx = _retention(proj, log_gamma[j], cos, sin, n_batch, seq, ctx_len)
            norm_gain = jnp.ones((d,), F32)
        else:
            proj = _proj(h1, w_in, j, (0, tiles_per_part, 2 * tiles_per_part), tiles_per_part)
            log_f, key = _proj_gates(h1, w_in, j, lower_bounds[j], tiles_per_part, 2 * tiles_per_part)
            o_lat, o_ctx = _hgrn(proj, log_f, key, n_batch, seq, ctx_len)
            norm_gain = hg_norm_g[j]
        gate_block = proj.shape[1] // d - 1
        xs, h2 = _out_proj(o_lat, o_ctx, proj, gate_block, norm_gain, w_out, j, xs, mod, norm2_g[layer],
                           seq, n_batch, n_rows, not retention)
        xs = _ffn(h2, ffn_gate_up, ffn_down, layer, xs, mod, seq, n_batch, n_rows)
    return _final_norm(xs, final_norm_g).reshape(n_batch, seq, d)
```

```python
import functools
import math

import jax
import jax.numpy as jnp
from jax import lax
from jax.experimental import pallas as pl
from jax.experimental.pallas import tpu as pltpu

F32 = jnp.float32
BF16 = jnp.bfloat16

EPS = 1e-6
LOG2E = 1.4426950408889634
ROPE_BASE = 10000.0
GRID_W = 64
N_MIXERS = 2
RET_HEADS = 8
HG_EXPAND = 128

V7X_LANES = 128
V7X_SUBLANES = 8
V7X_VMEM_BYTES = 64 * 1024 * 1024

ROW_TILE = 512
BIG_ROW_TILE = 1024
IN_COL_TILE = 1024
FFN_HID_TILE = 512
FFN_OUT_TILE = 512
MXU_COLS = 256
ADA_COL_TILE = 1024
RET_CHUNK = 256
RET_UNROLL = 8
HG_CHUNK = 128
HG_UNROLL = 4
HG_LOCAL_UNROLL = 8
ROW_STEP = 64
NORM_STEP = 128
CAST_STEP = 256
VMEM_LIMIT = 56 * 1024 * 1024


def _params(*semantics):
    return pltpu.CompilerParams(dimension_semantics=semantics, vmem_limit_bytes=VMEM_LIMIT)


def _sigmoid(x):
    return 1.0 / (1.0 + jnp.exp(-x))


def _dot(a, b):
    return jnp.dot(a, b, preferred_element_type=F32)


def _dot_nt(a, b):
    return lax.dot_general(a, b, (((1,), (1,)), ((), ())), preferred_element_type=F32)


def _dot_tn(a, b):
    return lax.dot_general(a, b, (((0,), (0,)), ((), ())), preferred_element_type=F32)


def _ada_kernel(c_ref, w_ref, b_ref, o_ref):
    c = c_ref[...]
    a = (c * _sigmoid(c)).astype(BF16)
    o_ref[...] = _dot(a, w_ref[...].astype(BF16)) + b_ref[...]


def _ada_mod(cond, ada_w, ada_b):
    depth, d, n = ada_w.shape
    rows = cond.shape[0]
    return pl.pallas_call(
        _ada_kernel,
        grid=(depth, n // ADA_COL_TILE),
        in_specs=[
            pl.BlockSpec((rows, d), lambda l, j: (0, 0)),
            pl.BlockSpec((None, d, ADA_COL_TILE), lambda l, j: (l, 0, j)),
            pl.BlockSpec((None, 1, ADA_COL_TILE), lambda l, j: (l, 0, j)),
        ],
        out_specs=pl.BlockSpec((None, rows, ADA_COL_TILE), lambda l, j: (l, 0, j)),
        out_shape=jax.ShapeDtypeStruct((depth, rows, n), F32),
        compiler_params=_params("parallel", "parallel"),
        name="ada_mod",
    )(cond, ada_w, ada_b.reshape(depth, 1, n))


def _norm_mod(x, gain, shift, scale):
    return x * lax.rsqrt(jnp.mean(x * x, axis=-1, keepdims=True) + EPS) * (gain * (1.0 + scale)) + shift


def _row_loop(rows, body, step_rows=ROW_STEP):
    def step(i, carry):
        body(pl.ds(pl.multiple_of(i * step_rows, step_rows), step_rows))
        return carry
    lax.fori_loop(0, rows // step_rows, step, 0)


def _mod_spec(width, slot, row_tile, rows_per_batch, n_batch, col_axis=None):
    def index(*ids):
        col = slot if col_axis is None else slot + ids[col_axis]
        return (jnp.minimum((ids[0] * row_tile) // rows_per_batch, n_batch), 0, col)
    return pl.BlockSpec((None, 1, width), index)


def _prenorm_kernel(x_ref, g_ref, sh_ref, sc_ref, h_ref):
    def body(rows):
        h = _norm_mod(x_ref[rows, :], g_ref[...], sh_ref[...], sc_ref[...])
        h_ref[rows, :] = h.astype(BF16)
    _row_loop(ROW_TILE, body, NORM_STEP)


def _prenorm(x, gain, mod, rows_per_batch, n_batch):
    r, d = x.shape
    tm = ROW_TILE
    return pl.pallas_call(
        _prenorm_kernel,
        grid=(r // tm,),
        in_specs=[
            pl.BlockSpec((tm, d), lambda i: (i, 0)),
            pl.BlockSpec((1, d), lambda i: (0, 0)),
            _mod_spec(d, 0, tm, rows_per_batch, n_batch),
            _mod_spec(d, 1, tm, rows_per_batch, n_batch),
        ],
        out_specs=pl.BlockSpec((tm, d), lambda i: (i, 0)),
        out_shape=jax.ShapeDtypeStruct((r, d), BF16),
        compiler_params=_params("parallel"),
        name="prenorm",
    )(x, gain.reshape(1, d), mod, mod)


def _col_tiles():
    return [slice(t * MXU_COLS, (t + 1) * MXU_COLS) for t in range(IN_COL_TILE // MXU_COLS)]


def _cast_weight_tile(w_ref, wb_ref):
    @pl.when(pl.program_id(1) == 0)
    def _():
        def body(rows):
            wb_ref[rows, :] = w_ref[rows, :].astype(BF16)
        _row_loop(w_ref.shape[0], body, CAST_STEP)


def _proj_kernel(h_ref, w_ref, o_ref, wb_ref, *, n_silu):
    _cast_weight_tile(w_ref, wb_ref)
    h = h_ref[...]

    def run(silu):
        for cols in _col_tiles():
            a = _dot(h, wb_ref[:, cols])
            if silu:
                a = a * _sigmoid(a)
            o_ref[:, cols] = a.astype(o_ref.dtype)

    if n_silu == 0:
        run(False)
    else:
        j = pl.program_id(0)
        pl.when(j < n_silu)(functools.partial(run, True))
        pl.when(j >= n_silu)(functools.partial(run, False))


def _proj(h, w, layer, col_tiles, n_silu):
    r, d = h.shape
    tm, tn = BIG_ROW_TILE, IN_COL_TILE
    first, skip_from, skip = col_tiles
    n_tiles = w.shape[2] // tn - first - skip
    wcol = lambda j, i: (layer, 0, first + j + jnp.where(j >= skip_from, skip, 0))
    return pl.pallas_call(
        functools.partial(_proj_kernel, n_silu=n_silu),
        grid=(n_tiles, r // tm),
        in_specs=[pl.BlockSpec((tm, d), lambda j, i: (i, 0)), pl.BlockSpec((None, d, tn), wcol)],
        out_specs=pl.BlockSpec((tm, tn), lambda j, i: (i, j)),
        out_shape=jax.ShapeDtypeStruct((r, n_tiles * tn), BF16),
        scratch_shapes=[pltpu.VMEM((d, tn), BF16)],
        compiler_params=_params("arbitrary", "arbitrary"),
        name="in_proj",
    )(h, w)


def _proj_gates_kernel(h_ref, w_ref, lb_ref, g_ref, key_ref, wb_ref):
    _cast_weight_tile(w_ref, wb_ref)
    h = h_ref[...]
    for cols in _col_tiles():
        _, g, key = _hg_gates(_dot(h, wb_ref[:, cols]), lb_ref[:, cols])
        g_ref[:, cols] = g
        key_ref[:, cols] = key.astype(key_ref.dtype)


def _proj_gates(h, w, layer, lower_bound, first_tile, n_tiles):
    r, d = h.shape
    tm, tn = BIG_ROW_TILE, IN_COL_TILE
    lb_tiles = lower_bound.shape[0] // tn
    out = pl.BlockSpec((tm, tn), lambda j, i: (i, j))
    return pl.pallas_call(
        _proj_gates_kernel,
        grid=(n_tiles, r // tm),
        in_specs=[pl.BlockSpec((tm, d), lambda j, i: (i, 0)),
                  pl.BlockSpec((None, d, tn), lambda j, i: (layer, 0, first_tile + j)),
                  pl.BlockSpec((1, tn), lambda j, i: (0, j % lb_tiles))],
        out_specs=[out, out],
        out_shape=[jax.ShapeDtypeStruct((r, n_tiles * tn), F32),
                   jax.ShapeDtypeStruct((r, n_tiles * tn), BF16)],
        scratch_shapes=[pltpu.VMEM((d, tn), BF16)],
        compiler_params=_params("arbitrary", "arbitrary"),
        name="in_proj_gates",
    )(h, w, lower_bound.reshape(1, -1))


def _outproj_kernel(ol_ref, oc_ref, g_ref, ng_ref, w_ref, x_ref, gt_ref, n2_ref, sh_ref, sc_ref,
                    xo_ref, h_ref, y_ref, *, hgrn, n_lat_tiles):
    def gate_from(o_ref):
        def gate(rows):
            o = o_ref[rows, :].astype(F32)
            g = g_ref[rows, :].astype(F32)
            if hgrn:
                o = o * lax.rsqrt(jnp.mean(o * o, axis=-1, keepdims=True) + EPS) * ng_ref[...]
                y = o * _sigmoid(g)
            else:
                y = o * (g * _sigmoid(g))
            y_ref[rows, :] = y.astype(BF16)
        _row_loop(ROW_TILE, gate)

    tile = pl.program_id(0)
    pl.when(tile < n_lat_tiles)(functools.partial(gate_from, ol_ref))
    pl.when(tile >= n_lat_tiles)(functools.partial(gate_from, oc_ref))

    y = y_ref[...]
    for t in range(w_ref.shape[1] // MXU_COLS):
        cols = slice(t * MXU_COLS, (t + 1) * MXU_COLS)
        xo_ref[:, cols] = x_ref[:, cols] + gt_ref[:, cols] * _dot(y, w_ref[:, cols])

    def norm(rows):
        h_ref[rows, :] = _norm_mod(xo_ref[rows, :], n2_ref[...], sh_ref[...], sc_ref[...]).astype(BF16)
    _row_loop(ROW_TILE, norm, NORM_STEP)


def _out_proj(o_lat, o_ctx, proj, gate_block, norm_gain, w, layer, x, mod, gain2, rows_per_batch, n_batch,
              n_rows, hgrn):
    d = x.shape[1]
    tm = ROW_TILE
    n_lat_tiles = o_lat.shape[0] // tm
    row = lambda i: (i, 0)
    const = lambda i: (0, 0)
    return pl.pallas_call(
        functools.partial(_outproj_kernel, hgrn=hgrn, n_lat_tiles=n_lat_tiles),
        grid=(n_rows // tm,),
        in_specs=[
            pl.BlockSpec((tm, d), lambda i: (jnp.minimum(i, n_lat_tiles - 1), 0)),
            pl.BlockSpec((tm, d), lambda i: (jnp.maximum(i - n_lat_tiles, 0), 0)),
            pl.BlockSpec((tm, d), lambda i: (i, gate_block)),
            pl.BlockSpec((1, d), const),
            pl.BlockSpec((None, d, d), lambda i: (layer, 0, 0)),
            pl.BlockSpec((tm, d), row),
            _mod_spec(d, 2, tm, rows_per_batch, n_batch),
            pl.BlockSpec((1, d), const),
            _mod_spec(d, 3, tm, rows_per_batch, n_batch),
            _mod_spec(d, 4, tm, rows_per_batch, n_batch),
        ],
        out_specs=[pl.BlockSpec((tm, d), row), pl.BlockSpec((tm, d), row)],
        out_shape=[jax.ShapeDtypeStruct((n_rows, d), F32), jax.ShapeDtypeStruct((n_rows, d), BF16)],
        scratch_shapes=[pltpu.VMEM((tm, d), BF16)],
        compiler_params=_params("parallel"),
        name="out_proj",
    )(o_lat, o_ctx, proj, norm_gain.reshape(1, d), w, x, mod, gain2.reshape(1, d), mod, mod)


def _gateup_kernel(h_ref, wg_ref, wu_ref, p_ref, wgb_ref, wub_ref):
    _cast_weight_tile(wg_ref, wgb_ref)
    _cast_weight_tile(wu_ref, wub_ref)
    h = h_ref[...]
    for t in range(FFN_HID_TILE // MXU_COLS):
        cols = slice(t * MXU_COLS, (t + 1) * MXU_COLS)
        a = _dot(h, wgb_ref[:, cols])
        b = _dot(h, wub_ref[:, cols])
        p_ref[:, cols] = (a * _sigmoid(a) * b).astype(BF16)


def _down_kernel(p_ref, wd_ref, x_ref, gt_ref, o_ref):
    p = p_ref[...]
    for t in range(FFN_OUT_TILE // MXU_COLS):
        cols = slice(t * MXU_COLS, (t + 1) * MXU_COLS)
        o_ref[:, cols] = x_ref[:, cols] + gt_ref[:, cols] * _dot(p, wd_ref[:, cols])


def _ffn(h2, w_gate_up, w_down, layer, x, mod, rows_per_batch, n_batch, n_rows):
    d = x.shape[1]
    hidden = w_down.shape[1]
    tm = BIG_ROW_TILE
    n_hid = hidden // FFN_HID_TILE
    p = pl.pallas_call(
        _gateup_kernel,
        grid=(n_hid, n_rows // tm),
        in_specs=[
            pl.BlockSpec((tm, d), lambda j, i: (i, 0)),
            pl.BlockSpec((None, d, FFN_HID_TILE), lambda j, i: (layer, 0, j)),
            pl.BlockSpec((None, d, FFN_HID_TILE), lambda j, i: (layer, 0, j + n_hid)),
        ],
        out_specs=pl.BlockSpec((tm, FFN_HID_TILE), lambda j, i: (i, j)),
        out_shape=jax.ShapeDtypeStruct((n_rows, hidden), BF16),
        scratch_shapes=[pltpu.VMEM((d, FFN_HID_TILE), BF16), pltpu.VMEM((d, FFN_HID_TILE), BF16)],
        compiler_params=_params("arbitrary", "arbitrary"),
        name="ffn_gate_up",
    )(h2, w_gate_up, w_gate_up)
    tn = FFN_OUT_TILE
    return pl.pallas_call(
        _down_kernel,
        grid=(n_rows // tm, d // tn),
        in_specs=[
            pl.BlockSpec((tm, hidden), lambda i, j: (i, 0)),
            pl.BlockSpec((None, hidden, tn), lambda i, j: (layer, 0, j)),
            pl.BlockSpec((tm, tn), lambda i, j: (i, j)),
            _mod_spec(tn, 5 * (d // tn), tm, rows_per_batch, n_batch, col_axis=1),
        ],
        out_specs=pl.BlockSpec((tm, tn), lambda i, j: (i, j)),
        out_shape=jax.ShapeDtypeStruct((n_rows, d), F32),
        compiler_params=_params("parallel", "arbitrary"),
        name="ffn_down",
    )(p, w_down, x, mod)


def _final_norm_kernel(x_ref, g_ref, o_ref):
    def body(rows):
        x = x_ref[rows, :]
        o_ref[rows, :] = x * lax.rsqrt(jnp.mean(x * x, axis=-1, keepdims=True) + EPS) * g_ref[...]
    _row_loop(ROW_TILE, body, NORM_STEP)


def _final_norm(x, gain):
    r, d = x.shape
    return pl.pallas_call(
        _final_norm_kernel,
        grid=(r // ROW_TILE,),
        in_specs=[pl.BlockSpec((ROW_TILE, d), lambda i: (i, 0)), pl.BlockSpec((1, d), lambda i: (0, 0))],
        out_specs=pl.BlockSpec((ROW_TILE, d), lambda i: (i, 0)),
        out_shape=jax.ShapeDtypeStruct((r, d), F32),
        compiler_params=_params("parallel"),
        name="final_norm",
    )(x, gain.reshape(1, d))


def _ret_kernel(lg_ref, q_ref, k_ref, v_ref, qc_ref, kc_ref, vc_ref, cos_ref, sin_ref,
                o_ref, oc_ref, dec_ref, kr_ref, sbs_ref, sf_ref, sb_ref, *, k_scale):
    c = RET_CHUNK
    dk = q_ref.shape[1]
    half = dk // 2
    n_chunks = q_ref.shape[0] // c
    head = pl.program_id(1)
    lgf = lg_ref[0, head]
    lgb = lg_ref[1, head]

    n_i = lax.broadcasted_iota(jnp.int32, (c, c), 0).astype(F32)
    m_i = lax.broadcasted_iota(jnp.int32, (c, c), 1).astype(F32)
    diff = n_i - m_i
    dec_ref[0] = (jnp.where(diff >= 0, jnp.exp(lgf * jnp.maximum(diff, 0.0)), 0.0)
                  + jnp.where(diff <= 0, jnp.exp(lgb * jnp.maximum(-diff, 0.0)), 0.0))
    t_i = lax.broadcasted_iota(jnp.int32, (c, dk), 0).astype(F32)
    dec_ref[1] = jnp.exp(lgf * (t_i + 1.0))
    dec_ref[2] = jnp.exp(lgb * (c - t_i))
    dec_ref[3] = jnp.exp(lgf * (c - 1.0 - t_i))
    dec_ref[4] = jnp.exp(lgb * t_i)
    cf = jnp.exp(lgf * c)
    cb = jnp.exp(lgb * c)

    def head_norm(o):
        return o * lax.rsqrt(jnp.mean(o * o, axis=-1, keepdims=True) + EPS)

    def rope(t, rows):
        cos = cos_ref[rows, :]
        sin = sin_ref[rows, :]
        t1 = t[:, :half]
        t2 = t[:, half:]
        return jnp.concatenate([t1 * cos - t2 * sin, t1 * sin + t2 * cos], axis=-1)

    qc = qc_ref[...]
    kc = (kc_ref[...].astype(F32) * k_scale)
    vc = vc_ref[...]
    sc = _dot_nt(qc, kc.astype(BF16)) * dec_ref[0]
    oc_ref[...] = head_norm(_dot(sc.astype(BF16), vc)).astype(oc_ref.dtype)
    sf_ref[...] = _dot_tn((kc * dec_ref[3]).astype(BF16), vc)
    sb_ref[...] = _dot_tn((kc * dec_ref[4]).astype(BF16), vc)

    unroll = math.gcd(RET_UNROLL, n_chunks)
    n_steps = n_chunks // unroll

    def chunk_rows(ci):
        return pl.ds(pl.multiple_of(ci * c, c), c)

    def bwd_local(i, carry):
        cis = [i * unroll + u for u in range(unroll)]
        rows = [chunk_rows(ci) for ci in cis]
        krs = [rope(k_ref[r, :].astype(F32), r) * k_scale for r in rows]
        kvs = [_dot_tn((kr * dec_ref[4]).astype(BF16), v_ref[r, :]) for kr, r in zip(krs, rows)]
        for ci, r, kr, kv in zip(cis, rows, krs, kvs):
            kr_ref[r, :] = kr.astype(BF16)
            sbs_ref[ci] = kv
        return carry
    lax.fori_loop(0, n_steps, bwd_local, 0)

    def bwd_state(i, carry):
        ci = n_chunks - 1 - i
        s = sb_ref[...]
        kv = sbs_ref[ci]
        sbs_ref[ci] = s
        sb_ref[...] = s * cb + kv
        return carry
    lax.fori_loop(0, n_chunks, bwd_state, 0)

    def fwd(i, carry):
        cis = [i * unroll + u for u in range(unroll)]
        rows = [chunk_rows(ci) for ci in cis]
        qrs = [rope(q_ref[r, :].astype(F32), r).astype(BF16) for r in rows]
        krs = [kr_ref[r, :] for r in rows]
        vs = [v_ref[r, :] for r in rows]
        scs = [_dot_nt(qr, kr) for qr, kr in zip(qrs, krs)]
        kvs = [_dot_tn((kr.astype(F32) * dec_ref[3]).astype(BF16), v) for kr, v in zip(krs, vs)]
        sfs = [sf_ref[...]]
        for kv in kvs:
            sfs.append(sfs[-1] * cf + kv)
        sf_ref[...] = sfs[-1]
        intra = [_dot((sc * dec_ref[0]).astype(BF16), v) for sc, v in zip(scs, vs)]
        inter_f = [_dot(qr, s.astype(BF16)) for qr, s in zip(qrs, sfs)]
        inter_b = [_dot(qr, sbs_ref[ci].astype(BF16)) for qr, ci in zip(qrs, cis)]
        for r, o1, o2, o3 in zip(rows, intra, inter_f, inter_b):
            o = o1 + dec_ref[1] * o2 + dec_ref[2] * o3
            o_ref[r, :] = head_norm(o).astype(o_ref.dtype)
        return carry
    lax.fori_loop(0, n_steps, fwd, 0)


def _retention(proj, log_gamma, cos, sin, n_batch, seq, ctx_len):
    heads = RET_HEADS
    dk = proj.shape[1] // (4 * heads)
    d = heads * dk
    assert dk == RET_CHUNK and ctx_len == RET_CHUNK and seq % RET_CHUNK == 0
    ctx_row0 = (n_batch * seq) // ctx_len
    lat = lambda part: pl.BlockSpec((seq, dk), lambda b, h: (b, part * heads + h))
    ctx = lambda part: pl.BlockSpec((ctx_len, dk), lambda b, h: (ctx_row0 + b, part * heads + h))
    tab = pl.BlockSpec((seq, dk // 2), lambda b, h: (0, 0))
    n_chunks = seq // RET_CHUNK
    return pl.pallas_call(
        functools.partial(_ret_kernel, k_scale=dk ** -0.5),
        grid=(n_batch, heads),
        in_specs=[pl.BlockSpec(memory_space=pltpu.SMEM),
                  lat(0), lat(1), lat(2), ctx(0), ctx(1), ctx(2), tab, tab],
        out_specs=[pl.BlockSpec((seq, dk), lambda b, h: (b, h)),
                   pl.BlockSpec((ctx_len, dk), lambda b, h: (b, h))],
        out_shape=[jax.ShapeDtypeStruct((n_batch * seq, d), BF16),
                   jax.ShapeDtypeStruct((n_batch * ctx_len, d), BF16)],
        scratch_shapes=[
            pltpu.VMEM((5, RET_CHUNK, RET_CHUNK), F32),
            pltpu.VMEM((seq, dk), BF16),
            pltpu.VMEM((n_chunks, dk, dk), F32),
            pltpu.VMEM((dk, dk), F32),
            pltpu.VMEM((dk, dk), F32),
        ],
        compiler_params=_params("parallel", "arbitrary"),
        name="retention",
    )(log_gamma, proj, proj, proj, proj, proj, proj, cos, sin)


def _hg_gates(z, lb):
    f = lb + (1.0 - lb) * _sigmoid(z)
    return f, jnp.log2(f), 1.0 - f


def _hg_cumsum(tri_ref, g):
    dk = g.shape[1]
    hi = g.astype(BF16)
    lo = (g - hi.astype(F32)).astype(BF16)
    r = _dot(tri_ref[...], jnp.concatenate([hi, lo], axis=1))
    return r[:, :dk] + r[:, dk:]


def _tiles(x):
    return [x[j * V7X_SUBLANES:(j + 1) * V7X_SUBLANES] for j in range(x.shape[0] // V7X_SUBLANES)]


def _row_of_tile(tile, r):
    return jnp.broadcast_to(tile[r:r + 1, :], tile.shape)


def _hg_levels(up, lo, cv, cin):
    upt, lot, cvt, cint = _tiles(up), _tiles(lo), _tiles(cv), _tiles(cin)
    nt = len(cvt)
    zero = jnp.zeros_like(cvt[0])
    lastt = [_row_of_tile(t, V7X_SUBLANES - 1) for t in cint]
    out = []
    bt = nt
    while bt >= 2:
        ht = bt // 2
        hi_rows, lo_rows = [], []
        for j in range(nt):
            b0 = (j // bt) * bt
            ref = lastt[b0 + ht - 1]
            if j - b0 >= ht:
                hi_rows.append(jnp.exp2(cvt[j] - ref) * upt[j])
                lo_rows.append(zero)
            else:
                hi_rows.append(zero)
                lo_rows.append(jnp.exp2(ref - cvt[j]) * lot[j])
        out.append((jnp.concatenate(hi_rows, axis=0), jnp.concatenate(lo_rows, axis=0)))
        bt = ht
    sub = lax.broadcasted_iota(jnp.int32, zero.shape, 0)
    for size in (8, 4):
        upper = (sub & (size // 2)) != 0
        sign = jnp.where(upper, 1.0, -1.0)
        hi_rows, lo_rows = [], []
        for j in range(nt):
            if size == 8:
                ref = _row_of_tile(cint[j], 3)
            else:
                ref = jnp.where(sub < 4, _row_of_tile(cint[j], 1), _row_of_tile(cint[j], 5))
            z = jnp.exp2((cvt[j] - ref) * sign) * jnp.where(upper, upt[j], lot[j])
            hi_rows.append(jnp.where(upper, z, 0.0))
            lo_rows.append(jnp.where(upper, 0.0, z))
        out.append((jnp.concatenate(hi_rows, axis=0), jnp.concatenate(lo_rows, axis=0)))
    return out


def _hg_bwd_local(rows, ci, g_ref, key_ref, v_ref, tri_ref, cumb_ref, kv_ref):
    g = g_ref[rows, :]
    cum = _hg_cumsum(tri_ref, g)
    cumb_ref[rows, :] = cum
    k_dec = key_ref[rows, :].astype(F32) * jnp.exp2(cum - g)
    kv_ref[ci] = _dot_tn(v_ref[rows, :], k_dec.astype(BF16))


def _hg_step_out(rows_list, cis, q_ref, gf_ref, kf_ref, gb_ref, kb_ref, v_ref, tri_ref, cumb_ref,
                 sbs_ref, sf_ref, out_ref, pair_xor):
    n = len(cis)
    q = [q_ref[r, :].astype(F32) for r in rows_list]
    kf = [kf_ref[r, :].astype(F32) for r in rows_list]
    kb = [kb_ref[r, :].astype(F32) for r in rows_list]
    v = [v_ref[r, :] for r in rows_list]
    gf = [gf_ref[r, :] for r in rows_list]
    gb = [gb_ref[r, :] for r in rows_list]
    c = q[0].shape[0]

    cumf = [_hg_cumsum(tri_ref, g) for g in gf]
    cumb = [cumb_ref[r, :] for r in rows_list]
    cumxb = [cb - g for cb, g in zip(cumb, gb)]
    totf = [cf[c - 1:, :] for cf in cumf]
    totb = [cb[c - 1:, :] for cb in cumb]

    kvs = [_dot_tn(v[u], (kf[u] * jnp.exp2(totf[u] - cumf[u])).astype(BF16)) for u in range(n)]
    sf = [sf_ref[...]]
    for u in range(n):
        sf.append(sf[u] * jnp.exp2(totf[u]) + kvs[u])
    sf_ref[...] = sf[n]

    odd = (lax.broadcasted_iota(jnp.int32, q[0].shape, 0) & 1) != 0
    scores = []
    for u in range(n):
        lev_f = _hg_levels(q[u], kf[u], cumf[u], cumf[u])
        lev_b = _hg_levels(kb[u], q[u], cumxb[u], cumb[u])
        lev_f.append((jnp.where(odd, q[u] * jnp.exp2(gf[u]), 0.0), jnp.where(odd, 0.0, kf[u])))
        lev_b.append((jnp.where(odd, kb[u], 0.0), jnp.where(odd, 0.0, q[u] * jnp.exp2(gb[u]))))
        a = None
        size = c
        for (xf, yf), (yb, xb) in zip(lev_f, lev_b):
            x = jnp.concatenate([xf.astype(BF16), xb.astype(BF16)], axis=1)
            y = jnp.concatenate([yf.astype(BF16), yb.astype(BF16)], axis=1)
            p = _dot_nt(x, y)
            a = p if a is None else jnp.where(pair_xor < size, p, a)
            size //= 2
        scores.append(a.astype(BF16))

    inter = []
    for u in range(n):
        q_dec = jnp.concatenate([(q[u] * jnp.exp2(cumf[u])).astype(BF16),
                                 (q[u] * jnp.exp2(totb[u] - cumxb[u])).astype(BF16)], axis=1)
        states = jnp.concatenate([sf[u].astype(BF16), sbs_ref[cis[u]].astype(BF16)], axis=1)
        inter.append(_dot_nt(q_dec, states))

    for u in range(n):
        o = _dot(scores[u], v[u]) + inter[u]
        o += jnp.sum(q[u] * (kf[u] + kb[u]), axis=-1, keepdims=True) * v[u].astype(F32)
        out_ref[rows_list[u], :] = o.astype(out_ref.dtype)


def _hg_kernel(q_ref, gf_ref, kf_ref, gb_ref, kb_ref, v_ref,
               qc_ref, gfc_ref, kfc_ref, gbc_ref, kbc_ref, vc_ref,
               o_ref, oc_ref, tri_ref, cumb_ref, sbs_ref, sf_ref, sb_ref):
    c = HG_CHUNK
    n_i = lax.broadcasted_iota(jnp.int32, (c, c), 0)
    m_i = lax.broadcasted_iota(jnp.int32, (c, c), 1)
    pair_xor = n_i ^ m_i
    tri_ref[...] = jnp.where(m_i <= n_i, 1.0, 0.0).astype(BF16)

    def run(qr_ref, gfr_ref, kfr_ref, gbr_ref, kbr_ref, vr_ref, out_ref):
        n_chunks = qr_ref.shape[0] // c
        unroll = math.gcd(HG_UNROLL, n_chunks)
        n_steps = n_chunks // unroll

        def chunk_rows(ci):
            return pl.ds(pl.multiple_of(ci * c, c), c)

        local_unroll = math.gcd(HG_LOCAL_UNROLL, n_chunks)

        def bwd_local(i, carry):
            cis = [i * local_unroll + u for u in range(local_unroll)]
            gs = [gbr_ref[chunk_rows(ci), :] for ci in cis]
            cums = [_hg_cumsum(tri_ref, g) for g in gs]
            k_decs = [(kbr_ref[chunk_rows(ci), :].astype(F32) * jnp.exp2(cum - g)).astype(BF16)
                      for ci, g, cum in zip(cis, gs, cums)]
            kvs = [_dot_tn(vr_ref[chunk_rows(ci), :], k_dec) for ci, k_dec in zip(cis, k_decs)]
            for ci, cum, kv in zip(cis, cums, kvs):
                cumb_ref[chunk_rows(ci), :] = cum
                sbs_ref[ci] = kv
            return carry
        lax.fori_loop(0, n_chunks // local_unroll, bwd_local, 0)

        def bwd_state(i, carry):
            ci = n_chunks - 1 - i
            rows = chunk_rows(ci)
            s = sb_ref[...]
            kv = sbs_ref[ci]
            sbs_ref[ci] = s
            last_tile = cumb_ref[pl.ds(pl.multiple_of(ci * c + c - V7X_SUBLANES, V7X_SUBLANES), V7X_SUBLANES), :]
            sb_ref[...] = s * jnp.exp2(last_tile[V7X_SUBLANES - 1:, :]) + kv
            return carry
        lax.fori_loop(0, n_chunks, bwd_state, 0)

        def fwd(i, carry):
            cis = [i * unroll + u for u in range(unroll)]
            _hg_step_out([chunk_rows(ci) for ci in cis], cis, qr_ref, gfr_ref, kfr_ref, gbr_ref, kbr_ref,
                         vr_ref, tri_ref, cumb_ref, sbs_ref, sf_ref, out_ref, pair_xor)
            return carry
        lax.fori_loop(0, n_steps, fwd, 0)

    sf_ref[...] = jnp.zeros_like(sf_ref)
    sb_ref[...] = jnp.zeros_like(sb_ref)
    run(qc_ref, gfc_ref, kfc_ref, gbc_ref, kbc_ref, vc_ref, oc_ref)
    run(q_ref, gf_ref, kf_ref, gb_ref, kb_ref, v_ref, o_ref)


def _hgrn(qiv, log_f, key, n_batch, seq, ctx_len):
    dk = HG_EXPAND
    heads = log_f.shape[1] // (2 * dk)
    d = heads * dk
    assert seq % HG_CHUNK == 0 and ctx_len % HG_CHUNK == 0 and ctx_len <= seq
    ctx_row0 = (n_batch * seq) // ctx_len
    lat = lambda part: pl.BlockSpec((seq, dk), lambda b, h: (b, part * heads + h))
    ctx = lambda part: pl.BlockSpec((ctx_len, dk), lambda b, h: (ctx_row0 + b, part * heads + h))
    return pl.pallas_call(
        _hg_kernel,
        grid=(n_batch, heads),
        in_specs=[lat(0), lat(0), lat(0), lat(1), lat(1), lat(1),
                  ctx(0), ctx(0), ctx(0), ctx(1), ctx(1), ctx(1)],
        out_specs=[pl.BlockSpec((seq, dk), lambda b, h: (b, h)),
                   pl.BlockSpec((ctx_len, dk), lambda b, h: (b, h))],
        out_shape=[jax.ShapeDtypeStruct((n_batch * seq, d), BF16),
                   jax.ShapeDtypeStruct((n_batch * ctx_len, d), BF16)],
        scratch_shapes=[
            pltpu.VMEM((HG_CHUNK, HG_CHUNK), BF16),
            pltpu.VMEM((seq, dk), F32),
            pltpu.VMEM((seq // HG_CHUNK, dk, dk), F32),
            pltpu.VMEM((dk, dk), F32),
            pltpu.VMEM((dk, dk), F32),
        ],
        compiler_params=_params("parallel", "arbitrary"),
        name="hgrn",
    )(qiv, log_f, key, log_f, key, qiv, qiv, log_f, key, log_f, key, qiv)


def _rope_tables(seq, head_dim):
    quarter = head_dim // 4
    rows = jnp.repeat(jnp.arange(seq // GRID_W, dtype=F32), GRID_W)
    cols = jnp.tile(jnp.arange(GRID_W, dtype=F32), seq // GRID_W)
    inv_freq = ROPE_BASE ** (-jnp.arange(quarter, dtype=F32) / quarter)
    ang = jnp.concatenate([rows[:, None] * inv_freq, cols[:, None] * inv_freq], axis=-1)
    return jnp.cos(ang), jnp.sin(ang)


def kernel(x, c, ctx, c_ctx, ada_w, ada_b, norm1_g, norm2_g, ret_w_in, ret_w_out, ret_decay_logits,
           hg_w_in, hg_w_out, hg_norm_g, hg_lower_bounds, ffn_w_gate_up, ffn_w_down, final_norm_g):
    n_batch, seq, d = x.shape
    ctx_len = ctx.shape[1]
    depth = ada_w.shape[0]
    n_lat = n_batch * seq
    n_all = n_lat + n_batch * ctx_len
    assert seq % BIG_ROW_TILE == 0 and (n_batch * ctx_len) % BIG_ROW_TILE == 0

    xs = jnp.concatenate([x.reshape(n_lat, d), ctx.reshape(n_batch * ctx_len, d)], axis=0)

    cond_rows = -(-(n_batch + 1) // V7X_SUBLANES) * V7X_SUBLANES
    cond = jnp.zeros((cond_rows, d), F32).at[:n_batch].set(c).at[n_batch].set(c_ctx)
    mods = _ada_mod(cond, ada_w, ada_b).reshape(depth, cond_rows, 1, 6 * d)

    lb_p = jax.nn.softmax(hg_lower_bounds.astype(F32), axis=0)
    lower_bounds = jnp.cumsum(lb_p, axis=0) - lb_p[0]
    log_gamma = jax.nn.log_sigmoid(ret_decay_logits.astype(F32))
    cos, sin = _rope_tables(seq, d // RET_HEADS)

    ret_in, ret_out = ret_w_in, ret_w_out.astype(BF16)
    hg_in, hg_out = hg_w_in, hg_w_out.astype(BF16)
    ffn_gate_up, ffn_down = ffn_w_gate_up, ffn_w_down.astype(BF16)

    for layer in range(depth):
        last = layer == depth - 1
        j = layer // N_MIXERS
        mod = mods[layer]
        n_rows = n_lat if last else n_all
        retention = layer % N_MIXERS == 0
        w_in, w_out = (ret_in, ret_out) if retention else (hg_in, hg_out)
        h1 = _prenorm(xs, norm1_g[layer], mod, seq, n_batch)
        tiles_per_part = d // IN_COL_TILE
        if retention:
            proj = _proj(h1, w_in, j, (0, w_in.shape[2], 0), 0)
            o_lat, o_ctx = _retention(proj, log_gamma[j], cos, sin, n_batch, seq, ctx_len)
            norm_gain = jnp.ones((d,), F32)
        else:
            proj = _proj(h1, w_in, j, (0, tiles_per_part, 2 * tiles_per_part), tiles_per_part)
            log_f, key = _proj_gates(h1, w_in, j, lower_bounds[j], tiles_per_part, 2 * tiles_per_part)
            o_lat, o_ctx = _hgrn(proj, log_f, key, n_batch, seq, ctx_len)
            norm_gain = hg_norm_g[j]
        gate_block = proj.shape[1] // d - 1
        xs, h2 = _out_proj(o_lat, o_ctx, proj, gate_block, norm_gain, w_out, j, xs, mod, norm2_g[layer],
                           seq, n_batch, n_rows, not retention)
        xs = _ffn(h2, ffn_gate_up, ffn_down, layer, xs, mod, seq, n_batch, n_rows)
    return _final_norm(xs, final_norm_g).reshape(n_batch, seq, d)
```

```python
import functools
import math

import jax
import jax.numpy as jnp
from jax import lax
from jax.experimental import pallas as pl
from jax.experimental.pallas import tpu as pltpu

F32 = jnp.float32
BF16 = jnp.bfloat16

EPS = 1e-6
LOG2E = 1.4426950408889634
ROPE_BASE = 10000.0
GRID_W = 64
N_MIXERS = 2
RET_HEADS = 8
HG_EXPAND = 128

V7X_LANES = 128
V7X_SUBLANES = 8
V7X_VMEM_BYTES = 64 * 1024 * 1024

ROW_TILE = 512
BIG_ROW_TILE = 1024
IN_COL_TILE = 1024
FFN_HID_TILE = 512
MXU_COLS = 256
ADA_COL_TILE = 1024
RET_CHUNK = 256
RET_UNROLL = 8
HG_CHUNK = 128
HG_UNROLL = 4
HG_LOCAL_UNROLL = 8
ROW_STEP = 64
NORM_STEP = 128
CAST_STEP = 256
VMEM_LIMIT = 56 * 1024 * 1024


def _params(*semantics):
    return pltpu.CompilerParams(dimension_semantics=semantics, vmem_limit_bytes=VMEM_LIMIT)


def _sigmoid(x):
    return 1.0 / (1.0 + jnp.exp(-x))


def _dot(a, b):
    return jnp.dot(a, b, preferred_element_type=F32)


def _dot_nt(a, b):
    return lax.dot_general(a, b, (((1,), (1,)), ((), ())), preferred_element_type=F32)


def _dot_tn(a, b):
    return lax.dot_general(a, b, (((0,), (0,)), ((), ())), preferred_element_type=F32)


def _ada_kernel(c_ref, w_ref, b_ref, o_ref):
    c = c_ref[...]
    a = (c * _sigmoid(c)).astype(BF16)
    o_ref[...] = _dot(a, w_ref[...].astype(BF16)) + b_ref[...]


def _ada_mod(cond, ada_w, ada_b):
    depth, d, n = ada_w.shape
    rows = cond.shape[0]
    return pl.pallas_call(
        _ada_kernel,
        grid=(depth, n // ADA_COL_TILE),
        in_specs=[
            pl.BlockSpec((rows, d), lambda l, j: (0, 0)),
            pl.BlockSpec((None, d, ADA_COL_TILE), lambda l, j: (l, 0, j)),
            pl.BlockSpec((None, 1, ADA_COL_TILE), lambda l, j: (l, 0, j)),
        ],
        out_specs=pl.BlockSpec((None, rows, ADA_COL_TILE), lambda l, j: (l, 0, j)),
        out_shape=jax.ShapeDtypeStruct((depth, rows, n), F32),
        compiler_params=_params("parallel", "parallel"),
        name="ada_mod",
    )(cond, ada_w, ada_b.reshape(depth, 1, n))


def _norm_mod(x, gain, shift, scale):
    return x * lax.rsqrt(jnp.mean(x * x, axis=-1, keepdims=True) + EPS) * (gain * (1.0 + scale)) + shift


def _row_loop(rows, body, step_rows=ROW_STEP):
    def step(i, carry):
        body(pl.ds(pl.multiple_of(i * step_rows, step_rows), step_rows))
        return carry
    lax.fori_loop(0, rows // step_rows, step, 0)


def _mod_spec(width, slot, row_tile, rows_per_batch, n_batch, col_axis=None):
    def index(*ids):
        col = slot if col_axis is None else slot + ids[col_axis]
        return (jnp.minimum((ids[0] * row_tile) // rows_per_batch, n_batch), 0, col)
    return pl.BlockSpec((None, 1, width), index)


def _prenorm_kernel(x_ref, g_ref, sh_ref, sc_ref, h_ref):
    def body(rows):
        h = _norm_mod(x_ref[rows, :], g_ref[...], sh_ref[...], sc_ref[...])
        h_ref[rows, :] = h.astype(BF16)
    _row_loop(ROW_TILE, body, NORM_STEP)


def _prenorm(x, gain, mod, rows_per_batch, n_batch):
    r, d = x.shape
    tm = ROW_TILE
    return pl.pallas_call(
        _prenorm_kernel,
        grid=(r // tm,),
        in_specs=[
            pl.BlockSpec((tm, d), lambda i: (i, 0)),
            pl.BlockSpec((1, d), lambda i: (0, 0)),
            _mod_spec(d, 0, tm, rows_per_batch, n_batch),
            _mod_spec(d, 1, tm, rows_per_batch, n_batch),
        ],
        out_specs=pl.BlockSpec((tm, d), lambda i: (i, 0)),
        out_shape=jax.ShapeDtypeStruct((r, d), BF16),
        compiler_params=_params("parallel"),
        name="prenorm",
    )(x, gain.reshape(1, d), mod, mod)


def _col_tiles():
    return [slice(t * MXU_COLS, (t + 1) * MXU_COLS) for t in range(IN_COL_TILE // MXU_COLS)]


def _cast_weight_tile(w_ref, wb_ref):
    @pl.when(pl.program_id(1) == 0)
    def _():
        def body(rows):
            wb_ref[rows, :] = w_ref[rows, :].astype(BF16)
        _row_loop(w_ref.shape[0], body, CAST_STEP)


def _proj_kernel(h_ref, w_ref, o_ref, wb_ref, *, n_silu):
    _cast_weight_tile(w_ref, wb_ref)
    h = h_ref[...]

    def run(silu):
        for cols in _col_tiles():
            a = _dot(h, wb_ref[:, cols])
            if silu:
                a = a * _sigmoid(a)
            o_ref[:, cols] = a.astype(o_ref.dtype)

    if n_silu == 0:
        run(False)
    else:
        j = pl.program_id(0)
        pl.when(j < n_silu)(functools.partial(run, True))
        pl.when(j >= n_silu)(functools.partial(run, False))


def _proj(h, w, layer, col_tiles, n_silu):
    r, d = h.shape
    tm, tn = BIG_ROW_TILE, IN_COL_TILE
    first, skip_from, skip = col_tiles
    n_tiles = w.shape[2] // tn - first - skip
    wcol = lambda j, i: (layer, 0, first + j + jnp.where(j >= skip_from, skip, 0))
    return pl.pallas_call(
        functools.partial(_proj_kernel, n_silu=n_silu),
        grid=(n_tiles, r // tm),
        in_specs=[pl.BlockSpec((tm, d), lambda j, i: (i, 0)), pl.BlockSpec((None, d, tn), wcol)],
        out_specs=pl.BlockSpec((tm, tn), lambda j, i: (i, j)),
        out_shape=jax.ShapeDtypeStruct((r, n_tiles * tn), BF16),
        scratch_shapes=[pltpu.VMEM((d, tn), BF16)],
        compiler_params=_params("arbitrary", "arbitrary"),
        name="in_proj",
    )(h, w)


def _proj_gates_kernel(h_ref, w_ref, lb_ref, g_ref, key_ref, wb_ref):
    _cast_weight_tile(w_ref, wb_ref)
    h = h_ref[...]
    for cols in _col_tiles():
        _, g, key = _hg_gates(_dot(h, wb_ref[:, cols]), lb_ref[:, cols])
        g_ref[:, cols] = g
        key_ref[:, cols] = key.astype(key_ref.dtype)


def _proj_gates(h, w, layer, lower_bound, first_tile, n_tiles):
    r, d = h.shape
    tm, tn = BIG_ROW_TILE, IN_COL_TILE
    lb_tiles = lower_bound.shape[0] // tn
    out = pl.BlockSpec((tm, tn), lambda j, i: (i, j))
    return pl.pallas_call(
        _proj_gates_kernel,
        grid=(n_tiles, r // tm),
        in_specs=[pl.BlockSpec((tm, d), lambda j, i: (i, 0)),
                  pl.BlockSpec((None, d, tn), lambda j, i: (layer, 0, first_tile + j)),
                  pl.BlockSpec((1, tn), lambda j, i: (0, j % lb_tiles))],
        out_specs=[out, out],
        out_shape=[jax.ShapeDtypeStruct((r, n_tiles * tn), F32),
                   jax.ShapeDtypeStruct((r, n_tiles * tn), BF16)],
        scratch_shapes=[pltpu.VMEM((d, tn), BF16)],
        compiler_params=_params("arbitrary", "arbitrary"),
        name="in_proj_gates",
    )(h, w, lower_bound.reshape(1, -1))


def _outproj_kernel(ol_ref, oc_ref, g_ref, ng_ref, w_ref, x_ref, gt_ref, n2_ref, sh_ref, sc_ref,
                    xo_ref, h_ref, y_ref, *, hgrn, n_lat_tiles):
    def gate_from(o_ref):
        def gate(rows):
            o = o_ref[rows, :].astype(F32)
            g = g_ref[rows, :].astype(F32)
            if hgrn:
                o = o * lax.rsqrt(jnp.mean(o * o, axis=-1, keepdims=True) + EPS) * ng_ref[...]
                y = o * _sigmoid(g)
            else:
                y = o * (g * _sigmoid(g))
            y_ref[rows, :] = y.astype(BF16)
        _row_loop(ROW_TILE, gate)

    tile = pl.program_id(0)
    pl.when(tile < n_lat_tiles)(functools.partial(gate_from, ol_ref))
    pl.when(tile >= n_lat_tiles)(functools.partial(gate_from, oc_ref))

    y = y_ref[...]
    for t in range(w_ref.shape[1] // MXU_COLS):
        cols = slice(t * MXU_COLS, (t + 1) * MXU_COLS)
        xo_ref[:, cols] = x_ref[:, cols] + gt_ref[:, cols] * _dot(y, w_ref[:, cols])

    def norm(rows):
        h_ref[rows, :] = _norm_mod(xo_ref[rows, :], n2_ref[...], sh_ref[...], sc_ref[...]).astype(BF16)
    _row_loop(ROW_TILE, norm, NORM_STEP)


def _out_proj(o_lat, o_ctx, proj, gate_block, norm_gain, w, layer, x, mod, gain2, rows_per_batch, n_batch,
              n_rows, hgrn):
    d = x.shape[1]
    tm = ROW_TILE
    n_lat_tiles = o_lat.shape[0] // tm
    row = lambda i: (i, 0)
    const = lambda i: (0, 0)
    return pl.pallas_call(
        functools.partial(_outproj_kernel, hgrn=hgrn, n_lat_tiles=n_lat_tiles),
        grid=(n_rows // tm,),
        in_specs=[
            pl.BlockSpec((tm, d), lambda i: (jnp.minimum(i, n_lat_tiles - 1), 0)),
            pl.BlockSpec((tm, d), lambda i: (jnp.maximum(i - n_lat_tiles, 0), 0)),
            pl.BlockSpec((tm, d), lambda i: (i, gate_block)),
            pl.BlockSpec((1, d), const),
            pl.BlockSpec((None, d, d), lambda i: (layer, 0, 0)),
            pl.BlockSpec((tm, d), row),
            _mod_spec(d, 2, tm, rows_per_batch, n_batch),
            pl.BlockSpec((1, d), const),
            _mod_spec(d, 3, tm, rows_per_batch, n_batch),
            _mod_spec(d, 4, tm, rows_per_batch, n_batch),
        ],
        out_specs=[pl.BlockSpec((tm, d), row), pl.BlockSpec((tm, d), row)],
        out_shape=[jax.ShapeDtypeStruct((n_rows, d), F32), jax.ShapeDtypeStruct((n_rows, d), BF16)],
        scratch_shapes=[pltpu.VMEM((tm, d), BF16)],
        compiler_params=_params("parallel"),
        name="out_proj",
    )(o_lat, o_ctx, proj, norm_gain.reshape(1, d), w, x, mod, gain2.reshape(1, d), mod, mod)


def _gateup_kernel(h_ref, wg_ref, wu_ref, p_ref, wgb_ref, wub_ref):
    _cast_weight_tile(wg_ref, wgb_ref)
    _cast_weight_tile(wu_ref, wub_ref)
    h = h_ref[...]
    for t in range(FFN_HID_TILE // MXU_COLS):
        cols = slice(t * MXU_COLS, (t + 1) * MXU_COLS)
        a = _dot(h, wgb_ref[:, cols])
        b = _dot(h, wub_ref[:, cols])
        p_ref[:, cols] = (a * _sigmoid(a) * b).astype(BF16)


def _down_kernel(p_ref, wd_ref, x_ref, gt_ref, ng_ref, sh_ref, sc_ref, *out_refs, final):
    xo_ref, h_ref = (out_refs[1], out_refs[0]) if final else out_refs
    p = p_ref[...]
    for t in range(wd_ref.shape[1] // MXU_COLS):
        cols = slice(t * MXU_COLS, (t + 1) * MXU_COLS)
        xo_ref[:, cols] = x_ref[:, cols] + gt_ref[:, cols] * _dot(p, wd_ref[:, cols])

    def norm(rows):
        x = xo_ref[rows, :]
        if final:
            h_ref[rows, :] = x * lax.rsqrt(jnp.mean(x * x, axis=-1, keepdims=True) + EPS) * ng_ref[...]
        else:
            h_ref[rows, :] = _norm_mod(x, ng_ref[...], sh_ref[...], sc_ref[...]).astype(BF16)
    _row_loop(ROW_TILE, norm, NORM_STEP)


def _ffn(h2, w_gate_up, w_down, layer, x, mod, next_gain, next_mod, rows_per_batch, n_batch, n_rows):
    d = x.shape[1]
    hidden = w_down.shape[1]
    tm = BIG_ROW_TILE
    n_hid = hidden // FFN_HID_TILE
    p = pl.pallas_call(
        _gateup_kernel,
        grid=(n_hid, n_rows // tm),
        in_specs=[
            pl.BlockSpec((tm, d), lambda j, i: (i, 0)),
            pl.BlockSpec((None, d, FFN_HID_TILE), lambda j, i: (layer, 0, j)),
            pl.BlockSpec((None, d, FFN_HID_TILE), lambda j, i: (layer, 0, j + n_hid)),
        ],
        out_specs=pl.BlockSpec((tm, FFN_HID_TILE), lambda j, i: (i, j)),
        out_shape=jax.ShapeDtypeStruct((n_rows, hidden), BF16),
        scratch_shapes=[pltpu.VMEM((d, FFN_HID_TILE), BF16), pltpu.VMEM((d, FFN_HID_TILE), BF16)],
        compiler_params=_params("arbitrary", "arbitrary"),
        name="ffn_gate_up",
    )(h2, w_gate_up, w_gate_up)
    tm = ROW_TILE
    final = next_mod is None
    row = lambda i: (i, 0)
    const = lambda i: (0, 0)
    norm_mod = mod if final else next_mod
    x_out = jax.ShapeDtypeStruct((n_rows, d), F32)
    outs = pl.pallas_call(
        functools.partial(_down_kernel, final=final),
        grid=(n_rows // tm,),
        in_specs=[
            pl.BlockSpec((tm, hidden), row),
            pl.BlockSpec((None, hidden, d), lambda i: (layer, 0, 0)),
            pl.BlockSpec((tm, d), row),
            _mod_spec(d, 5, tm, rows_per_batch, n_batch),
            pl.BlockSpec((1, d), const),
            _mod_spec(d, 0, tm, rows_per_batch, n_batch),
            _mod_spec(d, 1, tm, rows_per_batch, n_batch),
        ],
        out_specs=[pl.BlockSpec((tm, d), row)] * (1 if final else 2),
        out_shape=[x_out] if final else [x_out, jax.ShapeDtypeStruct((n_rows, d), BF16)],
        scratch_shapes=[pltpu.VMEM((tm, d), F32)] if final else [],
        compiler_params=_params("parallel"),
        name="ffn_down",
    )(p, w_down, x, mod, next_gain.reshape(1, d), norm_mod, norm_mod)
    return (outs[0], None) if final else outs


def _ret_kernel(lg_ref, q_ref, k_ref, v_ref, qc_ref, kc_ref, vc_ref, cos_ref, sin_ref,
                o_ref, oc_ref, dec_ref, kr_ref, sbs_ref, sf_ref, sb_ref, *, k_scale):
    c = RET_CHUNK
    dk = q_ref.shape[1]
    half = dk // 2
    n_chunks = q_ref.shape[0] // c
    head = pl.program_id(1)
    lgf = lg_ref[0, head]
    lgb = lg_ref[1, head]

    n_i = lax.broadcasted_iota(jnp.int32, (c, c), 0).astype(F32)
    m_i = lax.broadcasted_iota(jnp.int32, (c, c), 1).astype(F32)
    diff = n_i - m_i
    dec_ref[0] = (jnp.where(diff >= 0, jnp.exp(lgf * jnp.maximum(diff, 0.0)), 0.0)
                  + jnp.where(diff <= 0, jnp.exp(lgb * jnp.maximum(-diff, 0.0)), 0.0))
    t_i = lax.broadcasted_iota(jnp.int32, (c, dk), 0).astype(F32)
    dec_ref[1] = jnp.exp(lgf * (t_i + 1.0))
    dec_ref[2] = jnp.exp(lgb * (c - t_i))
    dec_ref[3] = jnp.exp(lgf * (c - 1.0 - t_i))
    dec_ref[4] = jnp.exp(lgb * t_i)
    cf = jnp.exp(lgf * c)
    cb = jnp.exp(lgb * c)

    def head_norm(o):
        return o * lax.rsqrt(jnp.mean(o * o, axis=-1, keepdims=True) + EPS)

    def rope(t, rows):
        cos = cos_ref[rows, :]
        sin = sin_ref[rows, :]
        t1 = t[:, :half]
        t2 = t[:, half:]
        return jnp.concatenate([t1 * cos - t2 * sin, t1 * sin + t2 * cos], axis=-1)

    qc = qc_ref[...]
    kc = (kc_ref[...].astype(F32) * k_scale)
    vc = vc_ref[...]
    sc = _dot_nt(qc, kc.astype(BF16)) * dec_ref[0]
    oc_ref[...] = head_norm(_dot(sc.astype(BF16), vc)).astype(oc_ref.dtype)
    sf_ref[...] = _dot_tn((kc * dec_ref[3]).astype(BF16), vc)
    sb_ref[...] = _dot_tn((kc * dec_ref[4]).astype(BF16), vc)

    unroll = math.gcd(RET_UNROLL, n_chunks)
    n_steps = n_chunks // unroll

    def chunk_rows(ci):
        return pl.ds(pl.multiple_of(ci * c, c), c)

    def bwd_local(i, carry):
        cis = [i * unroll + u for u in range(unroll)]
        rows = [chunk_rows(ci) for ci in cis]
        krs = [rope(k_ref[r, :].astype(F32), r) * k_scale for r in rows]
        kvs = [_dot_tn((kr * dec_ref[4]).astype(BF16), v_ref[r, :]) for kr, r in zip(krs, rows)]
        for ci, r, kr, kv in zip(cis, rows, krs, kvs):
            kr_ref[r, :] = kr.astype(BF16)
            sbs_ref[ci] = kv
        return carry
    lax.fori_loop(0, n_steps, bwd_local, 0)

    def bwd_state(i, carry):
        ci = n_chunks - 1 - i
        s = sb_ref[...]
        kv = sbs_ref[ci]
        sbs_ref[ci] = s
        sb_ref[...] = s * cb + kv
        return carry
    lax.fori_loop(0, n_chunks, bwd_state, 0)

    def fwd(i, carry):
        cis = [i * unroll + u for u in range(unroll)]
        rows = [chunk_rows(ci) for ci in cis]
        qrs = [rope(q_ref[r, :].astype(F32), r).astype(BF16) for r in rows]
        krs = [kr_ref[r, :] for r in rows]
        vs = [v_ref[r, :] for r in rows]
        scs = [_dot_nt(qr, kr) for qr, kr in zip(qrs, krs)]
        kvs = [_dot_tn((kr.astype(F32) * dec_ref[3]).astype(BF16), v) for kr, v in zip(krs, vs)]
        sfs = [sf_ref[...]]
        for kv in kvs:
            sfs.append(sfs[-1] * cf + kv)
        sf_ref[...] = sfs[-1]
        intra = [_dot((sc * dec_ref[0]).astype(BF16), v) for sc, v in zip(scs, vs)]
        inter_f = [_dot(qr, s.astype(BF16)) for qr, s in zip(qrs, sfs)]
        inter_b = [_dot(qr, sbs_ref[ci].astype(BF16)) for qr, ci in zip(qrs, cis)]
        for r, o1, o2, o3 in zip(rows, intra, inter_f, inter_b):
            o = o1 + dec_ref[1] * o2 + dec_ref[2] * o3
            o_ref[r, :] = head_norm(o).astype(o_ref.dtype)
        return carry
    lax.fori_loop(0, n_steps, fwd, 0)


def _retention(proj, log_gamma, cos, sin, n_batch, seq, ctx_len):
    heads = RET_HEADS
    dk = proj.shape[1] // (4 * heads)
    d = heads * dk
    assert dk == RET_CHUNK and ctx_len == RET_CHUNK and seq % RET_CHUNK == 0
    ctx_row0 = (n_batch * seq) // ctx_len
    lat = lambda part: pl.BlockSpec((seq, dk), lambda b, h: (b, part * heads + h))
    ctx = lambda part: pl.BlockSpec((ctx_len, dk), lambda b, h: (ctx_row0 + b, part * heads + h))
    tab = pl.BlockSpec((seq, dk // 2), lambda b, h: (0, 0))
    n_chunks = seq // RET_CHUNK
    return pl.pallas_call(
        functools.partial(_ret_kernel, k_scale=dk ** -0.5),
        grid=(n_batch, heads),
        in_specs=[pl.BlockSpec(memory_space=pltpu.SMEM),
                  lat(0), lat(1), lat(2), ctx(0), ctx(1), ctx(2), tab, tab],
        out_specs=[pl.BlockSpec((seq, dk), lambda b, h: (b, h)),
                   pl.BlockSpec((ctx_len, dk), lambda b, h: (b, h))],
        out_shape=[jax.ShapeDtypeStruct((n_batch * seq, d), BF16),
                   jax.ShapeDtypeStruct((n_batch * ctx_len, d), BF16)],
        scratch_shapes=[
            pltpu.VMEM((5, RET_CHUNK, RET_CHUNK), F32),
            pltpu.VMEM((seq, dk), BF16),
            pltpu.VMEM((n_chunks, dk, dk), F32),
            pltpu.VMEM((dk, dk), F32),
            pltpu.VMEM((dk, dk), F32),
        ],
        compiler_params=_params("parallel", "arbitrary"),
        name="retention",
    )(log_gamma, proj, proj, proj, proj, proj, proj, cos, sin)


def _hg_gates(z, lb):
    f = lb + (1.0 - lb) * _sigmoid(z)
    return f, jnp.log2(f), 1.0 - f


def _hg_cumsum(tri_ref, g):
    dk = g.shape[1]
    hi = g.astype(BF16)
    lo = (g - hi.astype(F32)).astype(BF16)
    r = _dot(tri_ref[...], jnp.concatenate([hi, lo], axis=1))
    return r[:, :dk] + r[:, dk:]


def _tiles(x):
    return [x[j * V7X_SUBLANES:(j + 1) * V7X_SUBLANES] for j in range(x.shape[0] // V7X_SUBLANES)]


def _row_of_tile(tile, r):
    return jnp.broadcast_to(tile[r:r + 1, :], tile.shape)


def _hg_levels(up, lo, cv, cin):
    upt, lot, cvt, cint = _tiles(up), _tiles(lo), _tiles(cv), _tiles(cin)
    nt = len(cvt)
    zero = jnp.zeros_like(cvt[0])
    lastt = [_row_of_tile(t, V7X_SUBLANES - 1) for t in cint]
    out = []
    bt = nt
    while bt >= 2:
        ht = bt // 2
        hi_rows, lo_rows = [], []
        for j in range(nt):
            b0 = (j // bt) * bt
            ref = lastt[b0 + ht - 1]
            if j - b0 >= ht:
                hi_rows.append(jnp.exp2(cvt[j] - ref) * upt[j])
                lo_rows.append(zero)
            else:
                hi_rows.append(zero)
                lo_rows.append(jnp.exp2(ref - cvt[j]) * lot[j])
        out.append((jnp.concatenate(hi_rows, axis=0), jnp.concatenate(lo_rows, axis=0)))
        bt = ht
    sub = lax.broadcasted_iota(jnp.int32, zero.shape, 0)
    for size in (8, 4):
        upper = (sub & (size // 2)) != 0
        sign = jnp.where(upper, 1.0, -1.0)
        hi_rows, lo_rows = [], []
        for j in range(nt):
            if size == 8:
                ref = _row_of_tile(cint[j], 3)
            else:
                ref = jnp.where(sub < 4, _row_of_tile(cint[j], 1), _row_of_tile(cint[j], 5))
            z = jnp.exp2((cvt[j] - ref) * sign) * jnp.where(upper, upt[j], lot[j])
            hi_rows.append(jnp.where(upper, z, 0.0))
            lo_rows.append(jnp.where(upper, 0.0, z))
        out.append((jnp.concatenate(hi_rows, axis=0), jnp.concatenate(lo_rows, axis=0)))
    return out


def _hg_bwd_local(rows, ci, g_ref, key_ref, v_ref, tri_ref, cumb_ref, kv_ref):
    g = g_ref[rows, :]
    cum = _hg_cumsum(tri_ref, g)
    cumb_ref[rows, :] = cum
    k_dec = key_ref[rows, :].astype(F32) * jnp.exp2(cum - g)
    kv_ref[ci] = _dot_tn(v_ref[rows, :], k_dec.astype(BF16))


def _hg_step_out(rows_list, cis, q_ref, gf_ref, kf_ref, gb_ref, kb_ref, v_ref, tri_ref, cumb_ref,
                 sbs_ref, sf_ref, out_ref, pair_xor):
    n = len(cis)
    q = [q_ref[r, :].astype(F32) for r in rows_list]
    kf = [kf_ref[r, :].astype(F32) for r in rows_list]
    kb = [kb_ref[r, :].astype(F32) for r in rows_list]
    v = [v_ref[r, :] for r in rows_list]
    gf = [gf_ref[r, :] for r in rows_list]
    gb = [gb_ref[r, :] for r in rows_list]
    c = q[0].shape[0]

    cumf = [_hg_cumsum(tri_ref, g) for g in gf]
    cumb = [cumb_ref[r, :] for r in rows_list]
    cumxb = [cb - g for cb, g in zip(cumb, gb)]
    totf = [cf[c - 1:, :] for cf in cumf]
    totb = [cb[c - 1:, :] for cb in cumb]

    kvs = [_dot_tn(v[u], (kf[u] * jnp.exp2(totf[u] - cumf[u])).astype(BF16)) for u in range(n)]
    sf = [sf_ref[...]]
    for u in range(n):
        sf.append(sf[u] * jnp.exp2(totf[u]) + kvs[u])
    sf_ref[...] = sf[n]

    odd = (lax.broadcasted_iota(jnp.int32, q[0].shape, 0) & 1) != 0
    scores = []
    for u in range(n):
        lev_f = _hg_levels(q[u], kf[u], cumf[u], cumf[u])
        lev_b = _hg_levels(kb[u], q[u], cumxb[u], cumb[u])
        lev_f.append((jnp.where(odd, q[u] * jnp.exp2(gf[u]), 0.0), jnp.where(odd, 0.0, kf[u])))
        lev_b.append((jnp.where(odd, kb[u], 0.0), jnp.where(odd, 0.0, q[u] * jnp.exp2(gb[u]))))
        a = None
        size = c
        for (xf, yf), (yb, xb) in zip(lev_f, lev_b):
            x = jnp.concatenate([xf.astype(BF16), xb.astype(BF16)], axis=1)
            y = jnp.concatenate([yf.astype(BF16), yb.astype(BF16)], axis=1)
            p = _dot_nt(x, y)
            a = p if a is None else jnp.where(pair_xor < size, p, a)
            size //= 2
        scores.append(a.astype(BF16))

    inter = []
    for u in range(n):
        q_dec = jnp.concatenate([(q[u] * jnp.exp2(cumf[u])).astype(BF16),
                                 (q[u] * jnp.exp2(totb[u] - cumxb[u])).astype(BF16)], axis=1)
        states = jnp.concatenate([sf[u].astype(BF16), sbs_ref[cis[u]].astype(BF16)], axis=1)
        inter.append(_dot_nt(q_dec, states))

    for u in range(n):
        o = _dot(scores[u], v[u]) + inter[u]
        o += jnp.sum(q[u] * (kf[u] + kb[u]), axis=-1, keepdims=True) * v[u].astype(F32)
        out_ref[rows_list[u], :] = o.astype(out_ref.dtype)


def _hg_kernel(q_ref, gf_ref, kf_ref, gb_ref, kb_ref, v_ref,
               qc_ref, gfc_ref, kfc_ref, gbc_ref, kbc_ref, vc_ref,
               o_ref, oc_ref, tri_ref, cumb_ref, sbs_ref, sf_ref, sb_ref):
    c = HG_CHUNK
    n_i = lax.broadcasted_iota(jnp.int32, (c, c), 0)
    m_i = lax.broadcasted_iota(jnp.int32, (c, c), 1)
    pair_xor = n_i ^ m_i
    tri_ref[...] = jnp.where(m_i <= n_i, 1.0, 0.0).astype(BF16)

    def run(qr_ref, gfr_ref, kfr_ref, gbr_ref, kbr_ref, vr_ref, out_ref):
        n_chunks = qr_ref.shape[0] // c
        unroll = math.gcd(HG_UNROLL, n_chunks)
        n_steps = n_chunks // unroll

        def chunk_rows(ci):
            return pl.ds(pl.multiple_of(ci * c, c), c)

        local_unroll = math.gcd(HG_LOCAL_UNROLL, n_chunks)

        def bwd_local(i, carry):
            cis = [i * local_unroll + u for u in range(local_unroll)]
            gs = [gbr_ref[chunk_rows(ci), :] for ci in cis]
            cums = [_hg_cumsum(tri_ref, g) for g in gs]
            k_decs = [(kbr_ref[chunk_rows(ci), :].astype(F32) * jnp.exp2(cum - g)).astype(BF16)
                      for ci, g, cum in zip(cis, gs, cums)]
            kvs = [_dot_tn(vr_ref[chunk_rows(ci), :], k_dec) for ci, k_dec in zip(cis, k_decs)]
            for ci, cum, kv in zip(cis, cums, kvs):
                cumb_ref[chunk_rows(ci), :] = cum
                sbs_ref[ci] = kv
            return carry
        lax.fori_loop(0, n_chunks // local_unroll, bwd_local, 0)

        def bwd_state(i, carry):
            ci = n_chunks - 1 - i
            rows = chunk_rows(ci)
            s = sb_ref[...]
            kv = sbs_ref[ci]
            sbs_ref[ci] = s
            last_tile = cumb_ref[pl.ds(pl.multiple_of(ci * c + c - V7X_SUBLANES, V7X_SUBLANES), V7X_SUBLANES), :]
            sb_ref[...] = s * jnp.exp2(last_tile[V7X_SUBLANES - 1:, :]) + kv
            return carry
        lax.fori_loop(0, n_chunks, bwd_state, 0)

        def fwd(i, carry):
            cis = [i * unroll + u for u in range(unroll)]
            _hg_step_out([chunk_rows(ci) for ci in cis], cis, qr_ref, gfr_ref, kfr_ref, gbr_ref, kbr_ref,
                         vr_ref, tri_ref, cumb_ref, sbs_ref, sf_ref, out_ref, pair_xor)
            return carry
        lax.fori_loop(0, n_steps, fwd, 0)

    sf_ref[...] = jnp.zeros_like(sf_ref)
    sb_ref[...] = jnp.zeros_like(sb_ref)
    run(qc_ref, gfc_ref, kfc_ref, gbc_ref, kbc_ref, vc_ref, oc_ref)
    run(q_ref, gf_ref, kf_ref, gb_ref, kb_ref, v_ref, o_ref)


def _hgrn(qiv, log_f, key, n_batch, seq, ctx_len):
    dk = HG_EXPAND
    heads = log_f.shape[1] // (2 * dk)
    d = heads * dk
    assert seq % HG_CHUNK == 0 and ctx_len % HG_CHUNK == 0 and ctx_len <= seq
    ctx_row0 = (n_batch * seq) // ctx_len
    lat = lambda part: pl.BlockSpec((seq, dk), lambda b, h: (b, part * heads + h))
    ctx = lambda part: pl.BlockSpec((ctx_len, dk), lambda b, h: (ctx_row0 + b, part * heads + h))
    return pl.pallas_call(
        _hg_kernel,
        grid=(n_batch, heads),
        in_specs=[lat(0), lat(0), lat(0), lat(1), lat(1), lat(1),
                  ctx(0), ctx(0), ctx(0), ctx(1), ctx(1), ctx(1)],
        out_specs=[pl.BlockSpec((seq, dk), lambda b, h: (b, h)),
                   pl.BlockSpec((ctx_len, dk), lambda b, h: (b, h))],
        out_shape=[jax.ShapeDtypeStruct((n_batch * seq, d), BF16),
                   jax.ShapeDtypeStruct((n_batch * ctx_len, d), BF16)],
        scratch_shapes=[
            pltpu.VMEM((HG_CHUNK, HG_CHUNK), BF16),
            pltpu.VMEM((seq, dk), F32),
            pltpu.VMEM((seq // HG_CHUNK, dk, dk), F32),
            pltpu.VMEM((dk, dk), F32),
            pltpu.VMEM((dk, dk), F32),
        ],
        compiler_params=_params("parallel", "arbitrary"),
        name="hgrn",
    )(qiv, log_f, key, log_f, key, qiv, qiv, log_f, key, log_f, key, qiv)


def _rope_tables(seq, head_dim):
    quarter = head_dim // 4
    rows = jnp.repeat(jnp.arange(seq // GRID_W, dtype=F32), GRID_W)
    cols = jnp.tile(jnp.arange(GRID_W, dtype=F32), seq // GRID_W)
    inv_freq = ROPE_BASE ** (-jnp.arange(quarter, dtype=F32) / quarter)
    ang = jnp.concatenate([rows[:, None] * inv_freq, cols[:, None] * inv_freq], axis=-1)
    return jnp.cos(ang), jnp.sin(ang)


def kernel(x, c, ctx, c_ctx, ada_w, ada_b, norm1_g, norm2_g, ret_w_in, ret_w_out, ret_decay_logits,
           hg_w_in, hg_w_out, hg_norm_g, hg_lower_bounds, ffn_w_gate_up, ffn_w_down, final_norm_g):
    n_batch, seq, d = x.shape
    ctx_len = ctx.shape[1]
    depth = ada_w.shape[0]
    n_lat = n_batch * seq
    n_all = n_lat + n_batch * ctx_len
    assert seq % BIG_ROW_TILE == 0 and (n_batch * ctx_len) % BIG_ROW_TILE == 0

    xs = jnp.concatenate([x.reshape(n_lat, d), ctx.reshape(n_batch * ctx_len, d)], axis=0)

    cond_rows = -(-(n_batch + 1) // V7X_SUBLANES) * V7X_SUBLANES
    cond = jnp.zeros((cond_rows, d), F32).at[:n_batch].set(c).at[n_batch].set(c_ctx)
    mods = _ada_mod(cond, ada_w, ada_b).reshape(depth, cond_rows, 1, 6 * d)

    lb_p = jax.nn.softmax(hg_lower_bounds.astype(F32), axis=0)
    lower_bounds = jnp.cumsum(lb_p, axis=0) - lb_p[0]
    log_gamma = jax.nn.log_sigmoid(ret_decay_logits.astype(F32))
    cos, sin = _rope_tables(seq, d // RET_HEADS)

    ret_in, ret_out = ret_w_in, ret_w_out.astype(BF16)
    hg_in, hg_out = hg_w_in, hg_w_out.astype(BF16)
    ffn_gate_up, ffn_down = ffn_w_gate_up, ffn_w_down.astype(BF16)

    h1 = _prenorm(xs, norm1_g[0], mods[0], seq, n_batch)
    for layer in range(depth):
        last = layer == depth - 1
        j = layer // N_MIXERS
        mod = mods[layer]
        n_rows = n_lat if last else n_all
        retention = layer % N_MIXERS == 0
        w_in, w_out = (ret_in, ret_out) if retention else (hg_in, hg_out)
        tiles_per_part = d // IN_COL_TILE
        if retention:
            proj = _proj(h1, w_in, j, (0, w_in.shape[2], 0), 0)
            o_lat, o_ctx = _retention(proj, log_gamma[j], cos, sin, n_batch, seq, ctx_len)
            norm_gain = jnp.ones((d,), F32)
        else:
            proj = _proj(h1, w_in, j, (0, tiles_per_part, 2 * tiles_per_part), tiles_per_part)
            log_f, key = _proj_gates(h1, w_in, j, lower_bounds[j], tiles_per_part, 2 * tiles_per_part)
            o_lat, o_ctx = _hgrn(proj, log_f, key, n_batch, seq, ctx_len)
            norm_gain = hg_norm_g[j]
        gate_block = proj.shape[1] // d - 1
        xs, h2 = _out_proj(o_lat, o_ctx, proj, gate_block, norm_gain, w_out, j, xs, mod, norm2_g[layer],
                           seq, n_batch, n_rows, not retention)
        next_gain, next_mod = (final_norm_g, None) if last else (norm1_g[layer + 1], mods[layer + 1])
        xs, h1 = _ffn(h2, ffn_gate_up, ffn_down, layer, xs, mod, next_gain, next_mod, seq, n_batch, n_rows)
    return xs.reshape(n_batch, seq, d)
```

```python
import functools
import math

import jax
import jax.numpy as jnp
from jax import lax
from jax.experimental import pallas as pl
from jax.experimental.pallas import tpu as pltpu

F32 = jnp.float32
BF16 = jnp.bfloat16

EPS = 1e-6
LOG2E = 1.4426950408889634
ROPE_BASE = 10000.0
GRID_W = 64
N_MIXERS = 2
RET_HEADS = 8
HG_EXPAND = 128

V7X_LANES = 128
V7X_SUBLANES = 8
V7X_VMEM_BYTES = 64 * 1024 * 1024

ROW_TILE = 512
BIG_ROW_TILE = 1024
IN_COL_TILE = 1024
FFN_HID_TILE = 512
MXU_COLS = 256
ADA_COL_TILE = 1024
RET_CHUNK = 256
RET_UNROLL = 8
HG_CHUNK = 128
HG_UNROLL = 4
HG_LOCAL_UNROLL = 8
ROW_STEP = 64
NORM_STEP = 128
CAST_STEP = 256
VMEM_LIMIT = 56 * 1024 * 1024


def _params(*semantics):
    return pltpu.CompilerParams(dimension_semantics=semantics, vmem_limit_bytes=VMEM_LIMIT)


def _sigmoid(x):
    return 1.0 / (1.0 + jnp.exp(-x))


def _dot(a, b):
    return jnp.dot(a, b, preferred_element_type=F32)


def _dot_nt(a, b):
    return lax.dot_general(a, b, (((1,), (1,)), ((), ())), preferred_element_type=F32)


def _dot_tn(a, b):
    return lax.dot_general(a, b, (((0,), (0,)), ((), ())), preferred_element_type=F32)


def _ada_kernel(c_ref, w_ref, b_ref, o_ref):
    c = c_ref[...]
    a = (c * _sigmoid(c)).astype(BF16)
    o_ref[...] = _dot(a, w_ref[...].astype(BF16)) + b_ref[...]


def _ada_mod(cond, ada_w, ada_b):
    depth, d, n = ada_w.shape
    rows = cond.shape[0]
    return pl.pallas_call(
        _ada_kernel,
        grid=(depth, n // ADA_COL_TILE),
        in_specs=[
            pl.BlockSpec((rows, d), lambda l, j: (0, 0)),
            pl.BlockSpec((None, d, ADA_COL_TILE), lambda l, j: (l, 0, j)),
            pl.BlockSpec((None, 1, ADA_COL_TILE), lambda l, j: (l, 0, j)),
        ],
        out_specs=pl.BlockSpec((None, rows, ADA_COL_TILE), lambda l, j: (l, 0, j)),
        out_shape=jax.ShapeDtypeStruct((depth, rows, n), F32),
        compiler_params=_params("parallel", "parallel"),
        name="ada_mod",
    )(cond, ada_w, ada_b.reshape(depth, 1, n))


def _norm_mod(x, gain, shift, scale):
    return x * lax.rsqrt(jnp.mean(x * x, axis=-1, keepdims=True) + EPS) * (gain * (1.0 + scale)) + shift


def _row_loop(rows, body, step_rows=ROW_STEP):
    def step(i, carry):
        body(pl.ds(pl.multiple_of(i * step_rows, step_rows), step_rows))
        return carry
    lax.fori_loop(0, rows // step_rows, step, 0)


def _mod_spec(width, slot, row_tile, rows_per_batch, n_batch, col_axis=None):
    def index(*ids):
        col = slot if col_axis is None else slot + ids[col_axis]
        return (jnp.minimum((ids[0] * row_tile) // rows_per_batch, n_batch), 0, col)
    return pl.BlockSpec((None, 1, width), index)


def _prenorm_kernel(xl_ref, xc_ref, g_ref, sh_ref, sc_ref, xs_ref, h_ref, *, n_lat_tiles):
    def copy_norm(x_ref):
        def body(rows):
            x = x_ref[rows, :]
            xs_ref[rows, :] = x
            h_ref[rows, :] = _norm_mod(x, g_ref[...], sh_ref[...], sc_ref[...]).astype(BF16)
        _row_loop(ROW_TILE, body, NORM_STEP)

    tile = pl.program_id(0)
    pl.when(tile < n_lat_tiles)(functools.partial(copy_norm, xl_ref))
    pl.when(tile >= n_lat_tiles)(functools.partial(copy_norm, xc_ref))


def _prenorm(x_lat, x_ctx, gain, mod, rows_per_batch, n_batch):
    d = x_lat.shape[1]
    tm = ROW_TILE
    n_lat_tiles = x_lat.shape[0] // tm
    r = x_lat.shape[0] + x_ctx.shape[0]
    row = lambda i: (i, 0)
    return pl.pallas_call(
        functools.partial(_prenorm_kernel, n_lat_tiles=n_lat_tiles),
        grid=(r // tm,),
        in_specs=[
            pl.BlockSpec((tm, d), lambda i: (jnp.minimum(i, n_lat_tiles - 1), 0)),
            pl.BlockSpec((tm, d), lambda i: (jnp.maximum(i - n_lat_tiles, 0), 0)),
            pl.BlockSpec((1, d), lambda i: (0, 0)),
            _mod_spec(d, 0, tm, rows_per_batch, n_batch),
            _mod_spec(d, 1, tm, rows_per_batch, n_batch),
        ],
        out_specs=[pl.BlockSpec((tm, d), row), pl.BlockSpec((tm, d), row)],
        out_shape=[jax.ShapeDtypeStruct((r, d), F32), jax.ShapeDtypeStruct((r, d), BF16)],
        compiler_params=_params("parallel"),
        name="prenorm",
    )(x_lat, x_ctx, gain.reshape(1, d), mod, mod)


def _col_tiles():
    return [slice(t * MXU_COLS, (t + 1) * MXU_COLS) for t in range(IN_COL_TILE // MXU_COLS)]


def _cast_weight_tile(w_ref, wb_ref):
    @pl.when(pl.program_id(1) == 0)
    def _():
        def body(rows):
            wb_ref[rows, :] = w_ref[rows, :].astype(BF16)
        _row_loop(w_ref.shape[0], body, CAST_STEP)


def _proj_kernel(h_ref, w_ref, o_ref, wb_ref, *, n_silu):
    _cast_weight_tile(w_ref, wb_ref)
    h = h_ref[...]

    def run(silu):
        for cols in _col_tiles():
            a = _dot(h, wb_ref[:, cols])
            if silu:
                a = a * _sigmoid(a)
            o_ref[:, cols] = a.astype(o_ref.dtype)

    if n_silu == 0:
        run(False)
    else:
        j = pl.program_id(0)
        pl.when(j < n_silu)(functools.partial(run, True))
        pl.when(j >= n_silu)(functools.partial(run, False))


def _proj(h, w, layer, col_tiles, n_silu):
    r, d = h.shape
    tm, tn = BIG_ROW_TILE, IN_COL_TILE
    first, skip_from, skip = col_tiles
    n_tiles = w.shape[2] // tn - first - skip
    wcol = lambda j, i: (layer, 0, first + j + jnp.where(j >= skip_from, skip, 0))
    return pl.pallas_call(
        functools.partial(_proj_kernel, n_silu=n_silu),
        grid=(n_tiles, r // tm),
        in_specs=[pl.BlockSpec((tm, d), lambda j, i: (i, 0)), pl.BlockSpec((None, d, tn), wcol)],
        out_specs=pl.BlockSpec((tm, tn), lambda j, i: (i, j)),
        out_shape=jax.ShapeDtypeStruct((r, n_tiles * tn), BF16),
        scratch_shapes=[pltpu.VMEM((d, tn), BF16)],
        compiler_params=_params("arbitrary", "arbitrary"),
        name="in_proj",
    )(h, w)


def _proj_gates_kernel(h_ref, w_ref, lb_ref, g_ref, key_ref, wb_ref):
    _cast_weight_tile(w_ref, wb_ref)
    h = h_ref[...]
    for cols in _col_tiles():
        _, g, key = _hg_gates(_dot(h, wb_ref[:, cols]), lb_ref[:, cols])
        g_ref[:, cols] = g
        key_ref[:, cols] = key.astype(key_ref.dtype)


def _proj_gates(h, w, layer, lower_bound, first_tile, n_tiles):
    r, d = h.shape
    tm, tn = BIG_ROW_TILE, IN_COL_TILE
    lb_tiles = lower_bound.shape[0] // tn
    out = pl.BlockSpec((tm, tn), lambda j, i: (i, j))
    return pl.pallas_call(
        _proj_gates_kernel,
        grid=(n_tiles, r // tm),
        in_specs=[pl.BlockSpec((tm, d), lambda j, i: (i, 0)),
                  pl.BlockSpec((None, d, tn), lambda j, i: (layer, 0, first_tile + j)),
                  pl.BlockSpec((1, tn), lambda j, i: (0, j % lb_tiles))],
        out_specs=[out, out],
        out_shape=[jax.ShapeDtypeStruct((r, n_tiles * tn), F32),
                   jax.ShapeDtypeStruct((r, n_tiles * tn), BF16)],
        scratch_shapes=[pltpu.VMEM((d, tn), BF16)],
        compiler_params=_params("arbitrary", "arbitrary"),
        name="in_proj_gates",
    )(h, w, lower_bound.reshape(1, -1))


def _outproj_kernel(ol_ref, oc_ref, g_ref, ng_ref, w_ref, x_ref, gt_ref, n2_ref, sh_ref, sc_ref,
                    xo_ref, h_ref, y_ref, *, hgrn, n_lat_tiles):
    def gate_from(o_ref):
        def gate(rows):
            o = o_ref[rows, :].astype(F32)
            g = g_ref[rows, :].astype(F32)
            if hgrn:
                o = o * lax.rsqrt(jnp.mean(o * o, axis=-1, keepdims=True) + EPS) * ng_ref[...]
                y = o * _sigmoid(g)
            else:
                y = o * (g * _sigmoid(g))
            y_ref[rows, :] = y.astype(BF16)
        _row_loop(ROW_TILE, gate)

    tile = pl.program_id(0)
    pl.when(tile < n_lat_tiles)(functools.partial(gate_from, ol_ref))
    pl.when(tile >= n_lat_tiles)(functools.partial(gate_from, oc_ref))

    y = y_ref[...]
    for t in range(w_ref.shape[1] // MXU_COLS):
        cols = slice(t * MXU_COLS, (t + 1) * MXU_COLS)
        xo_ref[:, cols] = x_ref[:, cols] + gt_ref[:, cols] * _dot(y, w_ref[:, cols])

    def norm(rows):
        h_ref[rows, :] = _norm_mod(xo_ref[rows, :], n2_ref[...], sh_ref[...], sc_ref[...]).astype(BF16)
    _row_loop(ROW_TILE, norm, NORM_STEP)


def _out_proj(o_lat, o_ctx, proj, gate_block, norm_gain, w, layer, x, mod, gain2, rows_per_batch, n_batch,
              n_rows, hgrn):
    d = x.shape[1]
    tm = ROW_TILE
    n_lat_tiles = o_lat.shape[0] // tm
    row = lambda i: (i, 0)
    const = lambda i: (0, 0)
    return pl.pallas_call(
        functools.partial(_outproj_kernel, hgrn=hgrn, n_lat_tiles=n_lat_tiles),
        grid=(n_rows // tm,),
        in_specs=[
            pl.BlockSpec((tm, d), lambda i: (jnp.minimum(i, n_lat_tiles - 1), 0)),
            pl.BlockSpec((tm, d), lambda i: (jnp.maximum(i - n_lat_tiles, 0), 0)),
            pl.BlockSpec((tm, d), lambda i: (i, gate_block)),
            pl.BlockSpec((1, d), const),
            pl.BlockSpec((None, d, d), lambda i: (layer, 0, 0)),
            pl.BlockSpec((tm, d), row),
            _mod_spec(d, 2, tm, rows_per_batch, n_batch),
            pl.BlockSpec((1, d), const),
            _mod_spec(d, 3, tm, rows_per_batch, n_batch),
            _mod_spec(d, 4, tm, rows_per_batch, n_batch),
        ],
        out_specs=[pl.BlockSpec((tm, d), row), pl.BlockSpec((tm, d), row)],
        out_shape=[jax.ShapeDtypeStruct((n_rows, d), F32), jax.ShapeDtypeStruct((n_rows, d), BF16)],
        scratch_shapes=[pltpu.VMEM((tm, d), BF16)],
        compiler_params=_params("parallel"),
        name="out_proj",
    )(o_lat, o_ctx, proj, norm_gain.reshape(1, d), w, x, mod, gain2.reshape(1, d), mod, mod)


def _gateup_kernel(h_ref, wg_ref, wu_ref, p_ref, wgb_ref, wub_ref):
    _cast_weight_tile(wg_ref, wgb_ref)
    _cast_weight_tile(wu_ref, wub_ref)
    h = h_ref[...]
    for t in range(FFN_HID_TILE // MXU_COLS):
        cols = slice(t * MXU_COLS, (t + 1) * MXU_COLS)
        a = _dot(h, wgb_ref[:, cols])
        b = _dot(h, wub_ref[:, cols])
        p_ref[:, cols] = (a * _sigmoid(a) * b).astype(BF16)


def _down_kernel(p_ref, wd_ref, x_ref, gt_ref, ng_ref, sh_ref, sc_ref, *out_refs, final):
    xo_ref, h_ref = (out_refs[1], out_refs[0]) if final else out_refs
    p = p_ref[...]
    for t in range(wd_ref.shape[1] // MXU_COLS):
        cols = slice(t * MXU_COLS, (t + 1) * MXU_COLS)
        xo_ref[:, cols] = x_ref[:, cols] + gt_ref[:, cols] * _dot(p, wd_ref[:, cols])

    def norm(rows):
        x = xo_ref[rows, :]
        if final:
            h_ref[rows, :] = x * lax.rsqrt(jnp.mean(x * x, axis=-1, keepdims=True) + EPS) * ng_ref[...]
        else:
            h_ref[rows, :] = _norm_mod(x, ng_ref[...], sh_ref[...], sc_ref[...]).astype(BF16)
    _row_loop(ROW_TILE, norm, NORM_STEP)


def _ffn(h2, w_gate_up, w_down, layer, x, mod, next_gain, next_mod, rows_per_batch, n_batch, n_rows):
    d = x.shape[1]
    hidden = w_down.shape[1]
    tm = BIG_ROW_TILE
    n_hid = hidden // FFN_HID_TILE
    p = pl.pallas_call(
        _gateup_kernel,
        grid=(n_hid, n_rows // tm),
        in_specs=[
            pl.BlockSpec((tm, d), lambda j, i: (i, 0)),
            pl.BlockSpec((None, d, FFN_HID_TILE), lambda j, i: (layer, 0, j)),
            pl.BlockSpec((None, d, FFN_HID_TILE), lambda j, i: (layer, 0, j + n_hid)),
        ],
        out_specs=pl.BlockSpec((tm, FFN_HID_TILE), lambda j, i: (i, j)),
        out_shape=jax.ShapeDtypeStruct((n_rows, hidden), BF16),
        scratch_shapes=[pltpu.VMEM((d, FFN_HID_TILE), BF16), pltpu.VMEM((d, FFN_HID_TILE), BF16)],
        compiler_params=_params("arbitrary", "arbitrary"),
        name="ffn_gate_up",
    )(h2, w_gate_up, w_gate_up)
    tm = ROW_TILE
    final = next_mod is None
    row = lambda i: (i, 0)
    const = lambda i: (0, 0)
    norm_mod = mod if final else next_mod
    x_out = jax.ShapeDtypeStruct((n_rows, d), F32)
    outs = pl.pallas_call(
        functools.partial(_down_kernel, final=final),
        grid=(n_rows // tm,),
        in_specs=[
            pl.BlockSpec((tm, hidden), row),
            pl.BlockSpec((None, hidden, d), lambda i: (layer, 0, 0)),
            pl.BlockSpec((tm, d), row),
            _mod_spec(d, 5, tm, rows_per_batch, n_batch),
            pl.BlockSpec((1, d), const),
            _mod_spec(d, 0, tm, rows_per_batch, n_batch),
            _mod_spec(d, 1, tm, rows_per_batch, n_batch),
        ],
        out_specs=[pl.BlockSpec((tm, d), row)] * (1 if final else 2),
        out_shape=[x_out] if final else [x_out, jax.ShapeDtypeStruct((n_rows, d), BF16)],
        scratch_shapes=[pltpu.VMEM((tm, d), F32)] if final else [],
        compiler_params=_params("parallel"),
        name="ffn_down",
    )(p, w_down, x, mod, next_gain.reshape(1, d), norm_mod, norm_mod)
    return (outs[0], None) if final else outs


def _ret_kernel(lg_ref, q_ref, k_ref, v_ref, qc_ref, kc_ref, vc_ref, cos_ref, sin_ref,
                o_ref, oc_ref, dec_ref, kr_ref, sbs_ref, sf_ref, sb_ref, *, k_scale):
    c = RET_CHUNK
    dk = q_ref.shape[1]
    half = dk // 2
    n_chunks = q_ref.shape[0] // c
    head = pl.program_id(1)
    lgf = lg_ref[0, head]
    lgb = lg_ref[1, head]

    n_i = lax.broadcasted_iota(jnp.int32, (c, c), 0).astype(F32)
    m_i = lax.broadcasted_iota(jnp.int32, (c, c), 1).astype(F32)
    diff = n_i - m_i
    dec_ref[0] = (jnp.where(diff >= 0, jnp.exp(lgf * jnp.maximum(diff, 0.0)), 0.0)
                  + jnp.where(diff <= 0, jnp.exp(lgb * jnp.maximum(-diff, 0.0)), 0.0))
    t_i = lax.broadcasted_iota(jnp.int32, (c, dk), 0).astype(F32)
    dec_ref[1] = jnp.exp(lgf * (t_i + 1.0))
    dec_ref[2] = jnp.exp(lgb * (c - t_i))
    dec_ref[3] = jnp.exp(lgf * (c - 1.0 - t_i))
    dec_ref[4] = jnp.exp(lgb * t_i)
    cf = jnp.exp(lgf * c)
    cb = jnp.exp(lgb * c)

    def head_norm(o):
        return o * lax.rsqrt(jnp.mean(o * o, axis=-1, keepdims=True) + EPS)

    def rope(t, rows):
        cos = cos_ref[rows, :]
        sin = sin_ref[rows, :]
        t1 = t[:, :half]
        t2 = t[:, half:]
        return jnp.concatenate([t1 * cos - t2 * sin, t1 * sin + t2 * cos], axis=-1)

    qc = qc_ref[...]
    kc = (kc_ref[...].astype(F32) * k_scale)
    vc = vc_ref[...]
    sc = _dot_nt(qc, kc.astype(BF16)) * dec_ref[0]
    oc_ref[...] = head_norm(_dot(sc.astype(BF16), vc)).astype(oc_ref.dtype)
    sf_ref[...] = _dot_tn((kc * dec_ref[3]).astype(BF16), vc)
    sb_ref[...] = _dot_tn((kc * dec_ref[4]).astype(BF16), vc)

    unroll = math.gcd(RET_UNROLL, n_chunks)
    n_steps = n_chunks // unroll

    def chunk_rows(ci):
        return pl.ds(pl.multiple_of(ci * c, c), c)

    def bwd_local(i, carry):
        cis = [i * unroll + u for u in range(unroll)]
        rows = [chunk_rows(ci) for ci in cis]
        krs = [rope(k_ref[r, :].astype(F32), r) * k_scale for r in rows]
        kvs = [_dot_tn((kr * dec_ref[4]).astype(BF16), v_ref[r, :]) for kr, r in zip(krs, rows)]
        for ci, r, kr, kv in zip(cis, rows, krs, kvs):
            kr_ref[r, :] = kr.astype(BF16)
            sbs_ref[ci] = kv
        return carry
    lax.fori_loop(0, n_steps, bwd_local, 0)

    def bwd_state(i, carry):
        ci = n_chunks - 1 - i
        s = sb_ref[...]
        kv = sbs_ref[ci]
        sbs_ref[ci] = s
        sb_ref[...] = s * cb + kv
        return carry
    lax.fori_loop(0, n_chunks, bwd_state, 0)

    def fwd(i, carry):
        cis = [i * unroll + u for u in range(unroll)]
        rows = [chunk_rows(ci) for ci in cis]
        qrs = [rope(q_ref[r, :].astype(F32), r).astype(BF16) for r in rows]
        krs = [kr_ref[r, :] for r in rows]
        vs = [v_ref[r, :] for r in rows]
        scs = [_dot_nt(qr, kr) for qr, kr in zip(qrs, krs)]
        kvs = [_dot_tn((kr.astype(F32) * dec_ref[3]).astype(BF16), v) for kr, v in zip(krs, vs)]
        sfs = [sf_ref[...]]
        for kv in kvs:
            sfs.append(sfs[-1] * cf + kv)
        sf_ref[...] = sfs[-1]
        intra = [_dot((sc * dec_ref[0]).astype(BF16), v) for sc, v in zip(scs, vs)]
        inter_f = [_dot(qr, s.astype(BF16)) for qr, s in zip(qrs, sfs)]
        inter_b = [_dot(qr, sbs_ref[ci].astype(BF16)) for qr, ci in zip(qrs, cis)]
        for r, o1, o2, o3 in zip(rows, intra, inter_f, inter_b):
            o = o1 + dec_ref[1] * o2 + dec_ref[2] * o3
            o_ref[r, :] = head_norm(o).astype(o_ref.dtype)
        return carry
    lax.fori_loop(0, n_steps, fwd, 0)


def _retention(proj, log_gamma, cos, sin, n_batch, seq, ctx_len):
    heads = RET_HEADS
    dk = proj.shape[1] // (4 * heads)
    d = heads * dk
    assert dk == RET_CHUNK and ctx_len == RET_CHUNK and seq % RET_CHUNK == 0
    ctx_row0 = (n_batch * seq) // ctx_len
    lat = lambda part: pl.BlockSpec((seq, dk), lambda b, h: (b, part * heads + h))
    ctx = lambda part: pl.BlockSpec((ctx_len, dk), lambda b, h: (ctx_row0 + b, part * heads + h))
    tab = pl.BlockSpec((seq, dk // 2), lambda b, h: (0, 0))
    n_chunks = seq // RET_CHUNK
    return pl.pallas_call(
        functools.partial(_ret_kernel, k_scale=dk ** -0.5),
        grid=(n_batch, heads),
        in_specs=[pl.BlockSpec(memory_space=pltpu.SMEM),
                  lat(0), lat(1), lat(2), ctx(0), ctx(1), ctx(2), tab, tab],
        out_specs=[pl.BlockSpec((seq, dk), lambda b, h: (b, h)),
                   pl.BlockSpec((ctx_len, dk), lambda b, h: (b, h))],
        out_shape=[jax.ShapeDtypeStruct((n_batch * seq, d), BF16),
                   jax.ShapeDtypeStruct((n_batch * ctx_len, d), BF16)],
        scratch_shapes=[
            pltpu.VMEM((5, RET_CHUNK, RET_CHUNK), F32),
            pltpu.VMEM((seq, dk), BF16),
            pltpu.VMEM((n_chunks, dk, dk), F32),
            pltpu.VMEM((dk, dk), F32),
            pltpu.VMEM((dk, dk), F32),
        ],
        compiler_params=_params("parallel", "arbitrary"),
        name="retention",
    )(log_gamma, proj, proj, proj, proj, proj, proj, cos, sin)


def _hg_gates(z, lb):
    f = lb + (1.0 - lb) * _sigmoid(z)
    return f, jnp.log2(f), 1.0 - f


def _hg_cumsum(tri_ref, g):
    dk = g.shape[1]
    hi = g.astype(BF16)
    lo = (g - hi.astype(F32)).astype(BF16)
    r = _dot(tri_ref[...], jnp.concatenate([hi, lo], axis=1))
    return r[:, :dk] + r[:, dk:]


def _tiles(x):
    return [x[j * V7X_SUBLANES:(j + 1) * V7X_SUBLANES] for j in range(x.shape[0] // V7X_SUBLANES)]


def _row_of_tile(tile, r):
    return jnp.broadcast_to(tile[r:r + 1, :], tile.shape)


def _hg_levels(up, lo, cv, cin):
    upt, lot, cvt, cint = _tiles(up), _tiles(lo), _tiles(cv), _tiles(cin)
    nt = len(cvt)
    zero = jnp.zeros_like(cvt[0])
    lastt = [_row_of_tile(t, V7X_SUBLANES - 1) for t in cint]
    out = []
    bt = nt
    while bt >= 2:
        ht = bt // 2
        hi_rows, lo_rows = [], []
        for j in range(nt):
            b0 = (j // bt) * bt
            ref = lastt[b0 + ht - 1]
            if j - b0 >= ht:
                hi_rows.append(jnp.exp2(cvt[j] - ref) * upt[j])
                lo_rows.append(zero)
            else:
                hi_rows.append(zero)
                lo_rows.append(jnp.exp2(ref - cvt[j]) * lot[j])
        out.append((jnp.concatenate(hi_rows, axis=0), jnp.concatenate(lo_rows, axis=0)))
        bt = ht
    sub = lax.broadcasted_iota(jnp.int32, zero.shape, 0)
    for size in (8, 4):
        upper = (sub & (size // 2)) != 0
        sign = jnp.where(upper, 1.0, -1.0)
        hi_rows, lo_rows = [], []
        for j in range(nt):
            if size == 8:
                ref = _row_of_tile(cint[j], 3)
            else:
                ref = jnp.where(sub < 4, _row_of_tile(cint[j], 1), _row_of_tile(cint[j], 5))
            z = jnp.exp2((cvt[j] - ref) * sign) * jnp.where(upper, upt[j], lot[j])
            hi_rows.append(jnp.where(upper, z, 0.0))
            lo_rows.append(jnp.where(upper, 0.0, z))
        out.append((jnp.concatenate(hi_rows, axis=0), jnp.concatenate(lo_rows, axis=0)))
    return out


def _hg_bwd_local(rows, ci, g_ref, key_ref, v_ref, tri_ref, cumb_ref, kv_ref):
    g = g_ref[rows, :]
    cum = _hg_cumsum(tri_ref, g)
    cumb_ref[rows, :] = cum
    k_dec = key_ref[rows, :].astype(F32) * jnp.exp2(cum - g)
    kv_ref[ci] = _dot_tn(v_ref[rows, :], k_dec.astype(BF16))


def _hg_step_out(rows_list, cis, q_ref, gf_ref, kf_ref, gb_ref, kb_ref, v_ref, tri_ref, cumb_ref,
                 sbs_ref, sf_ref, out_ref, pair_xor):
    n = len(cis)
    q = [q_ref[r, :].astype(F32) for r in rows_list]
    kf = [kf_ref[r, :].astype(F32) for r in rows_list]
    kb = [kb_ref[r, :].astype(F32) for r in rows_list]
    v = [v_ref[r, :] for r in rows_list]
    gf = [gf_ref[r, :] for r in rows_list]
    gb = [gb_ref[r, :] for r in rows_list]
    c = q[0].shape[0]

    cumf = [_hg_cumsum(tri_ref, g) for g in gf]
    cumb = [cumb_ref[r, :] for r in rows_list]
    cumxb = [cb - g for cb, g in zip(cumb, gb)]
    totf = [cf[c - 1:, :] for cf in cumf]
    totb = [cb[c - 1:, :] for cb in cumb]

    kvs = [_dot_tn(v[u], (kf[u] * jnp.exp2(totf[u] - cumf[u])).astype(BF16)) for u in range(n)]
    sf = [sf_ref[...]]
    for u in range(n):
        sf.append(sf[u] * jnp.exp2(totf[u]) + kvs[u])
    sf_ref[...] = sf[n]

    odd = (lax.broadcasted_iota(jnp.int32, q[0].shape, 0) & 1) != 0
    scores = []
    for u in range(n):
        lev_f = _hg_levels(q[u], kf[u], cumf[u], cumf[u])
        lev_b = _hg_levels(kb[u], q[u], cumxb[u], cumb[u])
        lev_f.append((jnp.where(odd, q[u] * jnp.exp2(gf[u]), 0.0), jnp.where(odd, 0.0, kf[u])))
        lev_b.append((jnp.where(odd, kb[u], 0.0), jnp.where(odd, 0.0, q[u] * jnp.exp2(gb[u]))))
        a = None
        size = c
        for (xf, yf), (yb, xb) in zip(lev_f, lev_b):
            x = jnp.concatenate([xf.astype(BF16), xb.astype(BF16)], axis=1)
            y = jnp.concatenate([yf.astype(BF16), yb.astype(BF16)], axis=1)
            p = _dot_nt(x, y)
            a = p if a is None else jnp.where(pair_xor < size, p, a)
            size //= 2
        scores.append(a.astype(BF16))

    inter = []
    for u in range(n):
        q_dec = jnp.concatenate([(q[u] * jnp.exp2(cumf[u])).astype(BF16),
                                 (q[u] * jnp.exp2(totb[u] - cumxb[u])).astype(BF16)], axis=1)
        states = jnp.concatenate([sf[u].astype(BF16), sbs_ref[cis[u]].astype(BF16)], axis=1)
        inter.append(_dot_nt(q_dec, states))

    for u in range(n):
        o = _dot(scores[u], v[u]) + inter[u]
        o += jnp.sum(q[u] * (kf[u] + kb[u]), axis=-1, keepdims=True) * v[u].astype(F32)
        out_ref[rows_list[u], :] = o.astype(out_ref.dtype)


def _hg_kernel(q_ref, gf_ref, kf_ref, gb_ref, kb_ref, v_ref,
               qc_ref, gfc_ref, kfc_ref, gbc_ref, kbc_ref, vc_ref,
               o_ref, oc_ref, tri_ref, cumb_ref, sbs_ref, sf_ref, sb_ref):
    c = HG_CHUNK
    n_i = lax.broadcasted_iota(jnp.int32, (c, c), 0)
    m_i = lax.broadcasted_iota(jnp.int32, (c, c), 1)
    pair_xor = n_i ^ m_i
    tri_ref[...] = jnp.where(m_i <= n_i, 1.0, 0.0).astype(BF16)

    def run(qr_ref, gfr_ref, kfr_ref, gbr_ref, kbr_ref, vr_ref, out_ref):
        n_chunks = qr_ref.shape[0] // c
        unroll = math.gcd(HG_UNROLL, n_chunks)
        n_steps = n_chunks // unroll

        def chunk_rows(ci):
            return pl.ds(pl.multiple_of(ci * c, c), c)

        local_unroll = math.gcd(HG_LOCAL_UNROLL, n_chunks)

        def bwd_local(i, carry):
            cis = [i * local_unroll + u for u in range(local_unroll)]
            gs = [gbr_ref[chunk_rows(ci), :] for ci in cis]
            cums = [_hg_cumsum(tri_ref, g) for g in gs]
            k_decs = [(kbr_ref[chunk_rows(ci), :].astype(F32) * jnp.exp2(cum - g)).astype(BF16)
                      for ci, g, cum in zip(cis, gs, cums)]
            kvs = [_dot_tn(vr_ref[chunk_rows(ci), :], k_dec) for ci, k_dec in zip(cis, k_decs)]
            for ci, cum, kv in zip(cis, cums, kvs):
                cumb_ref[chunk_rows(ci), :] = cum
                sbs_ref[ci] = kv
            return carry
        lax.fori_loop(0, n_chunks // local_unroll, bwd_local, 0)

        def bwd_state(i, carry):
            s = sb_ref[...]
            for u in range(unroll):
                ci = n_chunks - 1 - (i * unroll + u)
                kv = sbs_ref[ci]
                sbs_ref[ci] = s
                last_tile = cumb_ref[pl.ds(pl.multiple_of(ci * c + c - V7X_SUBLANES, V7X_SUBLANES),
                                           V7X_SUBLANES), :]
                s = s * jnp.exp2(last_tile[V7X_SUBLANES - 1:, :]) + kv
            sb_ref[...] = s
            return carry
        lax.fori_loop(0, n_steps, bwd_state, 0)

        def fwd(i, carry):
            cis = [i * unroll + u for u in range(unroll)]
            _hg_step_out([chunk_rows(ci) for ci in cis], cis, qr_ref, gfr_ref, kfr_ref, gbr_ref, kbr_ref,
                         vr_ref, tri_ref, cumb_ref, sbs_ref, sf_ref, out_ref, pair_xor)
            return carry
        lax.fori_loop(0, n_steps, fwd, 0)

    sf_ref[...] = jnp.zeros_like(sf_ref)
    sb_ref[...] = jnp.zeros_like(sb_ref)
    run(qc_ref, gfc_ref, kfc_ref, gbc_ref, kbc_ref, vc_ref, oc_ref)
    run(q_ref, gf_ref, kf_ref, gb_ref, kb_ref, v_ref, o_ref)


def _hgrn(qiv, log_f, key, n_batch, seq, ctx_len):
    dk = HG_EXPAND
    heads = log_f.shape[1] // (2 * dk)
    d = heads * dk
    assert seq % HG_CHUNK == 0 and ctx_len % HG_CHUNK == 0 and ctx_len <= seq
    ctx_row0 = (n_batch * seq) // ctx_len
    lat = lambda part: pl.BlockSpec((seq, dk), lambda b, h: (b, part * heads + h))
    ctx = lambda part: pl.BlockSpec((ctx_len, dk), lambda b, h: (ctx_row0 + b, part * heads + h))
    return pl.pallas_call(
        _hg_kernel,
        grid=(n_batch, heads),
        in_specs=[lat(0), lat(0), lat(0), lat(1), lat(1), lat(1),
                  ctx(0), ctx(0), ctx(0), ctx(1), ctx(1), ctx(1)],
        out_specs=[pl.BlockSpec((seq, dk), lambda b, h: (b, h)),
                   pl.BlockSpec((ctx_len, dk), lambda b, h: (b, h))],
        out_shape=[jax.ShapeDtypeStruct((n_batch * seq, d), BF16),
                   jax.ShapeDtypeStruct((n_batch * ctx_len, d), BF16)],
        scratch_shapes=[
            pltpu.VMEM((HG_CHUNK, HG_CHUNK), BF16),
            pltpu.VMEM((seq, dk), F32),
            pltpu.VMEM((seq // HG_CHUNK, dk, dk), F32),
            pltpu.VMEM((dk, dk), F32),
            pltpu.VMEM((dk, dk), F32),
        ],
        compiler_params=_params("parallel", "arbitrary"),
        name="hgrn",
    )(qiv, log_f, key, log_f, key, qiv, qiv, log_f, key, log_f, key, qiv)


def _rope_tables(seq, head_dim):
    quarter = head_dim // 4
    rows = jnp.repeat(jnp.arange(seq // GRID_W, dtype=F32), GRID_W)
    cols = jnp.tile(jnp.arange(GRID_W, dtype=F32), seq // GRID_W)
    inv_freq = ROPE_BASE ** (-jnp.arange(quarter, dtype=F32) / quarter)
    ang = jnp.concatenate([rows[:, None] * inv_freq, cols[:, None] * inv_freq], axis=-1)
    return jnp.cos(ang), jnp.sin(ang)


def kernel(x, c, ctx, c_ctx, ada_w, ada_b, norm1_g, norm2_g, ret_w_in, ret_w_out, ret_decay_logits,
           hg_w_in, hg_w_out, hg_norm_g, hg_lower_bounds, ffn_w_gate_up, ffn_w_down, final_norm_g):
    n_batch, seq, d = x.shape
    ctx_len = ctx.shape[1]
    depth = ada_w.shape[0]
    n_lat = n_batch * seq
    n_all = n_lat + n_batch * ctx_len
    assert seq % BIG_ROW_TILE == 0 and (n_batch * ctx_len) % BIG_ROW_TILE == 0

    cond_rows = -(-(n_batch + 1) // V7X_SUBLANES) * V7X_SUBLANES
    cond = jnp.zeros((cond_rows, d), F32).at[:n_batch].set(c).at[n_batch].set(c_ctx)
    mods = _ada_mod(cond, ada_w, ada_b).reshape(depth, cond_rows, 1, 6 * d)

    lb_p = jax.nn.softmax(hg_lower_bounds.astype(F32), axis=0)
    lower_bounds = jnp.cumsum(lb_p, axis=0) - lb_p[0]
    log_gamma = jax.nn.log_sigmoid(ret_decay_logits.astype(F32))
    cos, sin = _rope_tables(seq, d // RET_HEADS)

    ret_in, ret_out = ret_w_in, ret_w_out.astype(BF16)
    hg_in, hg_out = hg_w_in, hg_w_out.astype(BF16)
    ffn_gate_up, ffn_down = ffn_w_gate_up, ffn_w_down.astype(BF16)

    xs, h1 = _prenorm(x.reshape(n_lat, d), ctx.reshape(n_batch * ctx_len, d), norm1_g[0], mods[0],
                      seq, n_batch)
    for layer in range(depth):
        last = layer == depth - 1
        j = layer // N_MIXERS
        mod = mods[layer]
        n_rows = n_lat if last else n_all
        retention = layer % N_MIXERS == 0
        w_in, w_out = (ret_in, ret_out) if retention else (hg_in, hg_out)
        tiles_per_part = d // IN_COL_TILE
        if retention:
            proj = _proj(h1, w_in, j, (0, w_in.shape[2], 0), 0)
            o_lat, o_ctx = _retention(proj, log_gamma[j], cos, sin, n_batch, seq, ctx_len)
            norm_gain = jnp.ones((d,), F32)
        else:
            proj = _proj(h1, w_in, j, (0, tiles_per_part, 2 * tiles_per_part), tiles_per_part)
            log_f, key = _proj_gates(h1, w_in, j, lower_bounds[j], tiles_per_part, 2 * tiles_per_part)
            o_lat, o_ctx = _hgrn(proj, log_f, key, n_batch, seq, ctx_len)
            norm_gain = hg_norm_g[j]
        gate_block = proj.shape[1] // d - 1
        xs, h2 = _out_proj(o_lat, o_ctx, proj, gate_block, norm_gain, w_out, j, xs, mod, norm2_g[layer],
                           seq, n_batch, n_rows, not retention)
        next_gain, next_mod = (final_norm_g, None) if last else (norm1_g[layer + 1], mods[layer + 1])
        xs, h1 = _ffn(h2, ffn_gate_up, ffn_down, layer, xs, mod, next_gain, next_mod, seq, n_batch, n_rows)
    return xs.reshape(n_batch, seq, d)
```

```python
import functools
import math

import jax
import jax.numpy as jnp
from jax import lax
from jax.experimental import pallas as pl
from jax.experimental.pallas import tpu as pltpu

F32 = jnp.float32
BF16 = jnp.bfloat16

EPS = 1e-6
LOG2E = 1.4426950408889634
ROPE_BASE = 10000.0
GRID_W = 64
N_MIXERS = 2
RET_HEADS = 8
HG_EXPAND = 128

V7X_LANES = 128
V7X_SUBLANES = 8
V7X_VMEM_BYTES = 64 * 1024 * 1024

ROW_TILE = 512
BIG_ROW_TILE = 1024
IN_COL_TILE = 1024
FFN_HID_TILE = 512
MXU_COLS = 256
ADA_COL_TILE = 1024
RET_CHUNK = 256
RET_UNROLL = 8
HG_CHUNK = 128
HG_UNROLL = 8
HG_LOCAL_UNROLL = 16
ROW_STEP = 64
NORM_STEP = 128
CAST_STEP = 256
VMEM_LIMIT = 56 * 1024 * 1024


def _params(*semantics):
    return pltpu.CompilerParams(dimension_semantics=semantics, vmem_limit_bytes=VMEM_LIMIT)


def _sigmoid(x):
    return 1.0 / (1.0 + jnp.exp(-x))


def _dot(a, b):
    return jnp.dot(a, b, preferred_element_type=F32)


def _dot_nt(a, b):
    return lax.dot_general(a, b, (((1,), (1,)), ((), ())), preferred_element_type=F32)


def _dot_tn(a, b):
    return lax.dot_general(a, b, (((0,), (0,)), ((), ())), preferred_element_type=F32)


def _ada_kernel(c_ref, w_ref, b_ref, o_ref):
    c = c_ref[...]
    a = (c * _sigmoid(c)).astype(BF16)
    o_ref[...] = _dot(a, w_ref[...].astype(BF16)) + b_ref[...]


def _ada_mod(cond, ada_w, ada_b):
    depth, d, n = ada_w.shape
    rows = cond.shape[0]
    return pl.pallas_call(
        _ada_kernel,
        grid=(depth, n // ADA_COL_TILE),
        in_specs=[
            pl.BlockSpec((rows, d), lambda l, j: (0, 0)),
            pl.BlockSpec((None, d, ADA_COL_TILE), lambda l, j: (l, 0, j)),
            pl.BlockSpec((None, 1, ADA_COL_TILE), lambda l, j: (l, 0, j)),
        ],
        out_specs=pl.BlockSpec((None, rows, ADA_COL_TILE), lambda l, j: (l, 0, j)),
        out_shape=jax.ShapeDtypeStruct((depth, rows, n), F32),
        compiler_params=_params("parallel", "parallel"),
        name="ada_mod",
    )(cond, ada_w, ada_b.reshape(depth, 1, n))


def _norm_mod(x, gain, shift, scale):
    return x * lax.rsqrt(jnp.mean(x * x, axis=-1, keepdims=True) + EPS) * (gain * (1.0 + scale)) + shift


def _row_loop(rows, body, step_rows=ROW_STEP):
    def step(i, carry):
        body(pl.ds(pl.multiple_of(i * step_rows, step_rows), step_rows))
        return carry
    lax.fori_loop(0, rows // step_rows, step, 0)


def _mod_spec(width, slot, row_tile, rows_per_batch, n_batch, col_axis=None):
    def index(*ids):
        col = slot if col_axis is None else slot + ids[col_axis]
        return (jnp.minimum((ids[0] * row_tile) // rows_per_batch, n_batch), 0, col)
    return pl.BlockSpec((None, 1, width), index)


def _prenorm_kernel(xl_ref, xc_ref, g_ref, sh_ref, sc_ref, xs_ref, h_ref, *, n_lat_tiles):
    def copy_norm(x_ref):
        def body(rows):
            x = x_ref[rows, :]
            xs_ref[rows, :] = x
            h_ref[rows, :] = _norm_mod(x, g_ref[...], sh_ref[...], sc_ref[...]).astype(BF16)
        _row_loop(ROW_TILE, body, NORM_STEP)

    tile = pl.program_id(0)
    pl.when(tile < n_lat_tiles)(functools.partial(copy_norm, xl_ref))
    pl.when(tile >= n_lat_tiles)(functools.partial(copy_norm, xc_ref))


def _prenorm(x_lat, x_ctx, gain, mod, rows_per_batch, n_batch):
    d = x_lat.shape[1]
    tm = ROW_TILE
    n_lat_tiles = x_lat.shape[0] // tm
    r = x_lat.shape[0] + x_ctx.shape[0]
    row = lambda i: (i, 0)
    return pl.pallas_call(
        functools.partial(_prenorm_kernel, n_lat_tiles=n_lat_tiles),
        grid=(r // tm,),
        in_specs=[
            pl.BlockSpec((tm, d), lambda i: (jnp.minimum(i, n_lat_tiles - 1), 0)),
            pl.BlockSpec((tm, d), lambda i: (jnp.maximum(i - n_lat_tiles, 0), 0)),
            pl.BlockSpec((1, d), lambda i: (0, 0)),
            _mod_spec(d, 0, tm, rows_per_batch, n_batch),
            _mod_spec(d, 1, tm, rows_per_batch, n_batch),
        ],
        out_specs=[pl.BlockSpec((tm, d), row), pl.BlockSpec((tm, d), row)],
        out_shape=[jax.ShapeDtypeStruct((r, d), F32), jax.ShapeDtypeStruct((r, d), BF16)],
        compiler_params=_params("parallel"),
        name="prenorm",
    )(x_lat, x_ctx, gain.reshape(1, d), mod, mod)


def _col_tiles():
    return [slice(t * MXU_COLS, (t + 1) * MXU_COLS) for t in range(IN_COL_TILE // MXU_COLS)]


def _cast_weight_tile(w_ref, wb_ref):
    @pl.when(pl.program_id(1) == 0)
    def _():
        def body(rows):
            wb_ref[rows, :] = w_ref[rows, :].astype(BF16)
        _row_loop(w_ref.shape[0], body, CAST_STEP)


def _proj_kernel(h_ref, w_ref, o_ref, wb_ref, *, n_silu):
    _cast_weight_tile(w_ref, wb_ref)
    h = h_ref[...]

    def run(silu):
        for cols in _col_tiles():
            a = _dot(h, wb_ref[:, cols])
            if silu:
                a = a * _sigmoid(a)
            o_ref[:, cols] = a.astype(o_ref.dtype)

    if n_silu == 0:
        run(False)
    else:
        j = pl.program_id(0)
        pl.when(j < n_silu)(functools.partial(run, True))
        pl.when(j >= n_silu)(functools.partial(run, False))


def _proj(h, w, layer, col_tiles, n_silu):
    r, d = h.shape
    tm, tn = BIG_ROW_TILE, IN_COL_TILE
    first, skip_from, skip = col_tiles
    n_tiles = w.shape[2] // tn - first - skip
    wcol = lambda j, i: (layer, 0, first + j + jnp.where(j >= skip_from, skip, 0))
    return pl.pallas_call(
        functools.partial(_proj_kernel, n_silu=n_silu),
        grid=(n_tiles, r // tm),
        in_specs=[pl.BlockSpec((tm, d), lambda j, i: (i, 0)), pl.BlockSpec((None, d, tn), wcol)],
        out_specs=pl.BlockSpec((tm, tn), lambda j, i: (i, j)),
        out_shape=jax.ShapeDtypeStruct((r, n_tiles * tn), BF16),
        scratch_shapes=[pltpu.VMEM((d, tn), BF16)],
        compiler_params=_params("arbitrary", "arbitrary"),
        name="in_proj",
    )(h, w)


def _proj_gates_kernel(h_ref, w_ref, lb_ref, g_ref, key_ref, wb_ref):
    _cast_weight_tile(w_ref, wb_ref)
    h = h_ref[...]
    for cols in _col_tiles():
        _, g, key = _hg_gates(_dot(h, wb_ref[:, cols]), lb_ref[:, cols])
        g_ref[:, cols] = g
        key_ref[:, cols] = key.astype(key_ref.dtype)


def _proj_gates(h, w, layer, lower_bound, first_tile, n_tiles):
    r, d = h.shape
    tm, tn = BIG_ROW_TILE, IN_COL_TILE
    lb_tiles = lower_bound.shape[0] // tn
    out = pl.BlockSpec((tm, tn), lambda j, i: (i, j))
    return pl.pallas_call(
        _proj_gates_kernel,
        grid=(n_tiles, r // tm),
        in_specs=[pl.BlockSpec((tm, d), lambda j, i: (i, 0)),
                  pl.BlockSpec((None, d, tn), lambda j, i: (layer, 0, first_tile + j)),
                  pl.BlockSpec((1, tn), lambda j, i: (0, j % lb_tiles))],
        out_specs=[out, out],
        out_shape=[jax.ShapeDtypeStruct((r, n_tiles * tn), F32),
                   jax.ShapeDtypeStruct((r, n_tiles * tn), BF16)],
        scratch_shapes=[pltpu.VMEM((d, tn), BF16)],
        compiler_params=_params("arbitrary", "arbitrary"),
        name="in_proj_gates",
    )(h, w, lower_bound.reshape(1, -1))


def _outproj_kernel(ol_ref, oc_ref, g_ref, ng_ref, w_ref, x_ref, gt_ref, n2_ref, sh_ref, sc_ref,
                    xo_ref, h_ref, y_ref, *, hgrn, n_lat_tiles):
    def gate_from(o_ref):
        def gate(rows):
            o = o_ref[rows, :].astype(F32)
            g = g_ref[rows, :].astype(F32)
            if hgrn:
                o = o * lax.rsqrt(jnp.mean(o * o, axis=-1, keepdims=True) + EPS) * ng_ref[...]
                y = o * _sigmoid(g)
            else:
                y = o * (g * _sigmoid(g))
            y_ref[rows, :] = y.astype(BF16)
        _row_loop(ROW_TILE, gate)

    tile = pl.program_id(0)
    pl.when(tile < n_lat_tiles)(functools.partial(gate_from, ol_ref))
    pl.when(tile >= n_lat_tiles)(functools.partial(gate_from, oc_ref))

    y = y_ref[...]
    for t in range(w_ref.shape[1] // MXU_COLS):
        cols = slice(t * MXU_COLS, (t + 1) * MXU_COLS)
        xo_ref[:, cols] = x_ref[:, cols] + gt_ref[:, cols] * _dot(y, w_ref[:, cols])

    def norm(rows):
        h_ref[rows, :] = _norm_mod(xo_ref[rows, :], n2_ref[...], sh_ref[...], sc_ref[...]).astype(BF16)
    _row_loop(ROW_TILE, norm, NORM_STEP)


def _out_proj(o_lat, o_ctx, proj, gate_block, norm_gain, w, layer, x, mod, gain2, rows_per_batch, n_batch,
              n_rows, hgrn):
    d = x.shape[1]
    tm = ROW_TILE
    n_lat_tiles = o_lat.shape[0] // tm
    row = lambda i: (i, 0)
    const = lambda i: (0, 0)
    return pl.pallas_call(
        functools.partial(_outproj_kernel, hgrn=hgrn, n_lat_tiles=n_lat_tiles),
        grid=(n_rows // tm,),
        in_specs=[
            pl.BlockSpec((tm, d), lambda i: (jnp.minimum(i, n_lat_tiles - 1), 0)),
            pl.BlockSpec((tm, d), lambda i: (jnp.maximum(i - n_lat_tiles, 0), 0)),
            pl.BlockSpec((tm, d), lambda i: (i, gate_block)),
            pl.BlockSpec((1, d), const),
            pl.BlockSpec((None, d, d), lambda i: (layer, 0, 0)),
            pl.BlockSpec((tm, d), row),
            _mod_spec(d, 2, tm, rows_per_batch, n_batch),
            pl.BlockSpec((1, d), const),
            _mod_spec(d, 3, tm, rows_per_batch, n_batch),
            _mod_spec(d, 4, tm, rows_per_batch, n_batch),
        ],
        out_specs=[pl.BlockSpec((tm, d), row), pl.BlockSpec((tm, d), row)],
        out_shape=[jax.ShapeDtypeStruct((n_rows, d), F32), jax.ShapeDtypeStruct((n_rows, d), BF16)],
        scratch_shapes=[pltpu.VMEM((tm, d), BF16)],
        compiler_params=_params("parallel"),
        name="out_proj",
    )(o_lat, o_ctx, proj, norm_gain.reshape(1, d), w, x, mod, gain2.reshape(1, d), mod, mod)


def _gateup_kernel(h_ref, wg_ref, wu_ref, p_ref, wgb_ref, wub_ref):
    _cast_weight_tile(wg_ref, wgb_ref)
    _cast_weight_tile(wu_ref, wub_ref)
    h = h_ref[...]
    for t in range(FFN_HID_TILE // MXU_COLS):
        cols = slice(t * MXU_COLS, (t + 1) * MXU_COLS)
        a = _dot(h, wgb_ref[:, cols])
        b = _dot(h, wub_ref[:, cols])
        p_ref[:, cols] = (a * _sigmoid(a) * b).astype(BF16)


def _down_kernel(p_ref, wd_ref, x_ref, gt_ref, ng_ref, sh_ref, sc_ref, *out_refs, final):
    xo_ref, h_ref = (out_refs[1], out_refs[0]) if final else out_refs
    p = p_ref[...]
    for t in range(wd_ref.shape[1] // MXU_COLS):
        cols = slice(t * MXU_COLS, (t + 1) * MXU_COLS)
        xo_ref[:, cols] = x_ref[:, cols] + gt_ref[:, cols] * _dot(p, wd_ref[:, cols])

    def norm(rows):
        x = xo_ref[rows, :]
        if final:
            h_ref[rows, :] = x * lax.rsqrt(jnp.mean(x * x, axis=-1, keepdims=True) + EPS) * ng_ref[...]
        else:
            h_ref[rows, :] = _norm_mod(x, ng_ref[...], sh_ref[...], sc_ref[...]).astype(BF16)
    _row_loop(ROW_TILE, norm, NORM_STEP)


def _ffn(h2, w_gate_up, w_down, layer, x, mod, next_gain, next_mod, rows_per_batch, n_batch, n_rows):
    d = x.shape[1]
    hidden = w_down.shape[1]
    tm = BIG_ROW_TILE
    n_hid = hidden // FFN_HID_TILE
    p = pl.pallas_call(
        _gateup_kernel,
        grid=(n_hid, n_rows // tm),
        in_specs=[
            pl.BlockSpec((tm, d), lambda j, i: (i, 0)),
            pl.BlockSpec((None, d, FFN_HID_TILE), lambda j, i: (layer, 0, j)),
            pl.BlockSpec((None, d, FFN_HID_TILE), lambda j, i: (layer, 0, j + n_hid)),
        ],
        out_specs=pl.BlockSpec((tm, FFN_HID_TILE), lambda j, i: (i, j)),
        out_shape=jax.ShapeDtypeStruct((n_rows, hidden), BF16),
        scratch_shapes=[pltpu.VMEM((d, FFN_HID_TILE), BF16), pltpu.VMEM((d, FFN_HID_TILE), BF16)],
        compiler_params=_params("arbitrary", "arbitrary"),
        name="ffn_gate_up",
    )(h2, w_gate_up, w_gate_up)
    tm = ROW_TILE
    final = next_mod is None
    row = lambda i: (i, 0)
    const = lambda i: (0, 0)
    norm_mod = mod if final else next_mod
    x_out = jax.ShapeDtypeStruct((n_rows, d), F32)
    outs = pl.pallas_call(
        functools.partial(_down_kernel, final=final),
        grid=(n_rows // tm,),
        in_specs=[
            pl.BlockSpec((tm, hidden), row),
            pl.BlockSpec((None, hidden, d), lambda i: (layer, 0, 0)),
            pl.BlockSpec((tm, d), row),
            _mod_spec(d, 5, tm, rows_per_batch, n_batch),
            pl.BlockSpec((1, d), const),
            _mod_spec(d, 0, tm, rows_per_batch, n_batch),
            _mod_spec(d, 1, tm, rows_per_batch, n_batch),
        ],
        out_specs=[pl.BlockSpec((tm, d), row)] * (1 if final else 2),
        out_shape=[x_out] if final else [x_out, jax.ShapeDtypeStruct((n_rows, d), BF16)],
        scratch_shapes=[pltpu.VMEM((tm, d), F32)] if final else [],
        compiler_params=_params("parallel"),
        name="ffn_down",
    )(p, w_down, x, mod, next_gain.reshape(1, d), norm_mod, norm_mod)
    return (outs[0], None) if final else outs


def _ret_kernel(lg_ref, q_ref, k_ref, v_ref, qc_ref, kc_ref, vc_ref, cos_ref, sin_ref,
                o_ref, oc_ref, dec_ref, kr_ref, sbs_ref, sf_ref, sb_ref, *, k_scale):
    c = RET_CHUNK
    dk = q_ref.shape[1]
    half = dk // 2
    n_chunks = q_ref.shape[0] // c
    head = pl.program_id(1)
    lgf = lg_ref[0, head]
    lgb = lg_ref[1, head]

    n_i = lax.broadcasted_iota(jnp.int32, (c, c), 0).astype(F32)
    m_i = lax.broadcasted_iota(jnp.int32, (c, c), 1).astype(F32)
    diff = n_i - m_i
    dec_ref[0] = (jnp.where(diff >= 0, jnp.exp(lgf * jnp.maximum(diff, 0.0)), 0.0)
                  + jnp.where(diff <= 0, jnp.exp(lgb * jnp.maximum(-diff, 0.0)), 0.0))
    t_i = lax.broadcasted_iota(jnp.int32, (c, dk), 0).astype(F32)
    dec_ref[1] = jnp.exp(lgf * (t_i + 1.0))
    dec_ref[2] = jnp.exp(lgb * (c - t_i))
    dec_ref[3] = jnp.exp(lgf * (c - 1.0 - t_i))
    dec_ref[4] = jnp.exp(lgb * t_i)
    cf = jnp.exp(lgf * c)
    cb = jnp.exp(lgb * c)

    def head_norm(o):
        return o * lax.rsqrt(jnp.mean(o * o, axis=-1, keepdims=True) + EPS)

    def rope(t, rows):
        cos = cos_ref[rows, :]
        sin = sin_ref[rows, :]
        t1 = t[:, :half]
        t2 = t[:, half:]
        return jnp.concatenate([t1 * cos - t2 * sin, t1 * sin + t2 * cos], axis=-1)

    qc = qc_ref[...]
    kc = (kc_ref[...].astype(F32) * k_scale)
    vc = vc_ref[...]
    sc = _dot_nt(qc, kc.astype(BF16)) * dec_ref[0]
    oc_ref[...] = head_norm(_dot(sc.astype(BF16), vc)).astype(oc_ref.dtype)
    sf_ref[...] = _dot_tn((kc * dec_ref[3]).astype(BF16), vc)
    sb_ref[...] = _dot_tn((kc * dec_ref[4]).astype(BF16), vc)

    unroll = math.gcd(RET_UNROLL, n_chunks)
    n_steps = n_chunks // unroll

    def chunk_rows(ci):
        return pl.ds(pl.multiple_of(ci * c, c), c)

    def bwd_local(i, carry):
        cis = [i * unroll + u for u in range(unroll)]
        rows = [chunk_rows(ci) for ci in cis]
        krs = [rope(k_ref[r, :].astype(F32), r) * k_scale for r in rows]
        kvs = [_dot_tn((kr * dec_ref[4]).astype(BF16), v_ref[r, :]) for kr, r in zip(krs, rows)]
        for ci, r, kr, kv in zip(cis, rows, krs, kvs):
            kr_ref[r, :] = kr.astype(BF16)
            sbs_ref[ci] = kv
        return carry
    lax.fori_loop(0, n_steps, bwd_local, 0)

    def bwd_state(i, carry):
        ci = n_chunks - 1 - i
        s = sb_ref[...]
        kv = sbs_ref[ci]
        sbs_ref[ci] = s
        sb_ref[...] = s * cb + kv
        return carry
    lax.fori_loop(0, n_chunks, bwd_state, 0)

    def fwd(i, carry):
        cis = [i * unroll + u for u in range(unroll)]
        rows = [chunk_rows(ci) for ci in cis]
        qrs = [rope(q_ref[r, :].astype(F32), r).astype(BF16) for r in rows]
        krs = [kr_ref[r, :] for r in rows]
        vs = [v_ref[r, :] for r in rows]
        scs = [_dot_nt(qr, kr) for qr, kr in zip(qrs, krs)]
        kvs = [_dot_tn((kr.astype(F32) * dec_ref[3]).astype(BF16), v) for kr, v in zip(krs, vs)]
        sfs = [sf_ref[...]]
        for kv in kvs:
            sfs.append(sfs[-1] * cf + kv)
        sf_ref[...] = sfs[-1]
        intra = [_dot((sc * dec_ref[0]).astype(BF16), v) for sc, v in zip(scs, vs)]
        inter_f = [_dot(qr, s.astype(BF16)) for qr, s in zip(qrs, sfs)]
        inter_b = [_dot(qr, sbs_ref[ci].astype(BF16)) for qr, ci in zip(qrs, cis)]
        for r, o1, o2, o3 in zip(rows, intra, inter_f, inter_b):
            o = o1 + dec_ref[1] * o2 + dec_ref[2] * o3
            o_ref[r, :] = head_norm(o).astype(o_ref.dtype)
        return carry
    lax.fori_loop(0, n_steps, fwd, 0)


def _retention(proj, log_gamma, cos, sin, n_batch, seq, ctx_len):
    heads = RET_HEADS
    dk = proj.shape[1] // (4 * heads)
    d = heads * dk
    assert dk == RET_CHUNK and ctx_len == RET_CHUNK and seq % RET_CHUNK == 0
    ctx_row0 = (n_batch * seq) // ctx_len
    lat = lambda part: pl.BlockSpec((seq, dk), lambda b, h: (b, part * heads + h))
    ctx = lambda part: pl.BlockSpec((ctx_len, dk), lambda b, h: (ctx_row0 + b, part * heads + h))
    tab = pl.BlockSpec((seq, dk // 2), lambda b, h: (0, 0))
    n_chunks = seq // RET_CHUNK
    return pl.pallas_call(
        functools.partial(_ret_kernel, k_scale=dk ** -0.5),
        grid=(n_batch, heads),
        in_specs=[pl.BlockSpec(memory_space=pltpu.SMEM),
                  lat(0), lat(1), lat(2), ctx(0), ctx(1), ctx(2), tab, tab],
        out_specs=[pl.BlockSpec((seq, dk), lambda b, h: (b, h)),
                   pl.BlockSpec((ctx_len, dk), lambda b, h: (b, h))],
        out_shape=[jax.ShapeDtypeStruct((n_batch * seq, d), BF16),
                   jax.ShapeDtypeStruct((n_batch * ctx_len, d), BF16)],
        scratch_shapes=[
            pltpu.VMEM((5, RET_CHUNK, RET_CHUNK), F32),
            pltpu.VMEM((seq, dk), BF16),
            pltpu.VMEM((n_chunks, dk, dk), F32),
            pltpu.VMEM((dk, dk), F32),
            pltpu.VMEM((dk, dk), F32),
        ],
        compiler_params=_params("parallel", "arbitrary"),
        name="retention",
    )(log_gamma, proj, proj, proj, proj, proj, proj, cos, sin)


def _hg_gates(z, lb):
    f = lb + (1.0 - lb) * _sigmoid(z)
    return f, jnp.log2(f), 1.0 - f


def _hg_cumsum(tri_ref, g):
    dk = g.shape[1]
    hi = g.astype(BF16)
    lo = (g - hi.astype(F32)).astype(BF16)
    r = _dot(tri_ref[...], jnp.concatenate([hi, lo], axis=1))
    return r[:, :dk] + r[:, dk:]


def _tiles(x):
    return [x[j * V7X_SUBLANES:(j + 1) * V7X_SUBLANES] for j in range(x.shape[0] // V7X_SUBLANES)]


def _row_of_tile(tile, r):
    return jnp.broadcast_to(tile[r:r + 1, :], tile.shape)


def _hg_levels(up, lo, cv, cin):
    upt, lot, cvt, cint = _tiles(up), _tiles(lo), _tiles(cv), _tiles(cin)
    nt = len(cvt)
    zero = jnp.zeros_like(cvt[0])
    lastt = [_row_of_tile(t, V7X_SUBLANES - 1) for t in cint]
    out = []
    bt = nt
    while bt >= 2:
        ht = bt // 2
        hi_rows, lo_rows = [], []
        for j in range(nt):
            b0 = (j // bt) * bt
            ref = lastt[b0 + ht - 1]
            if j - b0 >= ht:
                hi_rows.append(jnp.exp2(cvt[j] - ref) * upt[j])
                lo_rows.append(zero)
            else:
                hi_rows.append(zero)
                lo_rows.append(jnp.exp2(ref - cvt[j]) * lot[j])
        out.append((jnp.concatenate(hi_rows, axis=0), jnp.concatenate(lo_rows, axis=0)))
        bt = ht
    sub = lax.broadcasted_iota(jnp.int32, zero.shape, 0)
    for size in (8, 4):
        upper = (sub & (size // 2)) != 0
        sign = jnp.where(upper, 1.0, -1.0)
        hi_rows, lo_rows = [], []
        for j in range(nt):
            if size == 8:
                ref = _row_of_tile(cint[j], 3)
            else:
                ref = jnp.where(sub < 4, _row_of_tile(cint[j], 1), _row_of_tile(cint[j], 5))
            z = jnp.exp2((cvt[j] - ref) * sign) * jnp.where(upper, upt[j], lot[j])
            hi_rows.append(jnp.where(upper, z, 0.0))
            lo_rows.append(jnp.where(upper, 0.0, z))
        out.append((jnp.concatenate(hi_rows, axis=0), jnp.concatenate(lo_rows, axis=0)))
    return out


def _hg_bwd_local(rows, ci, g_ref, key_ref, v_ref, tri_ref, cumb_ref, kv_ref):
    g = g_ref[rows, :]
    cum = _hg_cumsum(tri_ref, g)
    cumb_ref[rows, :] = cum
    k_dec = key_ref[rows, :].astype(F32) * jnp.exp2(cum - g)
    kv_ref[ci] = _dot_tn(v_ref[rows, :], k_dec.astype(BF16))


def _hg_step_out(rows_list, cis, q_ref, gf_ref, kf_ref, gb_ref, kb_ref, v_ref, tri_ref, cumb_ref,
                 sbs_ref, sf_ref, out_ref, pair_xor):
    n = len(cis)
    q = [q_ref[r, :].astype(F32) for r in rows_list]
    kf = [kf_ref[r, :].astype(F32) for r in rows_list]
    kb = [kb_ref[r, :].astype(F32) for r in rows_list]
    v = [v_ref[r, :] for r in rows_list]
    gf = [gf_ref[r, :] for r in rows_list]
    gb = [gb_ref[r, :] for r in rows_list]
    c = q[0].shape[0]

    cumf = [_hg_cumsum(tri_ref, g) for g in gf]
    cumb = [cumb_ref[r, :] for r in rows_list]
    cumxb = [cb - g for cb, g in zip(cumb, gb)]
    totf = [cf[c - 1:, :] for cf in cumf]
    totb = [cb[c - 1:, :] for cb in cumb]

    kvs = [_dot_tn(v[u], (kf[u] * jnp.exp2(totf[u] - cumf[u])).astype(BF16)) for u in range(n)]
    sf = [sf_ref[...]]
    for u in range(n):
        sf.append(sf[u] * jnp.exp2(totf[u]) + kvs[u])
    sf_ref[...] = sf[n]

    odd = (lax.broadcasted_iota(jnp.int32, q[0].shape, 0) & 1) != 0
    scores = []
    for u in range(n):
        lev_f = _hg_levels(q[u], kf[u], cumf[u], cumf[u])
        lev_b = _hg_levels(kb[u], q[u], cumxb[u], cumb[u])
        lev_f.append((jnp.where(odd, q[u] * jnp.exp2(gf[u]), 0.0), jnp.where(odd, 0.0, kf[u])))
        lev_b.append((jnp.where(odd, kb[u], 0.0), jnp.where(odd, 0.0, q[u] * jnp.exp2(gb[u]))))
        a = None
        size = c
        for (xf, yf), (yb, xb) in zip(lev_f, lev_b):
            x = jnp.concatenate([xf.astype(BF16), xb.astype(BF16)], axis=1)
            y = jnp.concatenate([yf.astype(BF16), yb.astype(BF16)], axis=1)
            p = _dot_nt(x, y)
            a = p if a is None else jnp.where(pair_xor < size, p, a)
            size //= 2
        scores.append(a.astype(BF16))

    inter = []
    for u in range(n):
        q_dec = jnp.concatenate([(q[u] * jnp.exp2(cumf[u])).astype(BF16),
                                 (q[u] * jnp.exp2(totb[u] - cumxb[u])).astype(BF16)], axis=1)
        states = jnp.concatenate([sf[u].astype(BF16), sbs_ref[cis[u]].astype(BF16)], axis=1)
        inter.append(_dot_nt(q_dec, states))

    for u in range(n):
        o = _dot(scores[u], v[u]) + inter[u]
        o += jnp.sum(q[u] * (kf[u] + kb[u]), axis=-1, keepdims=True) * v[u].astype(F32)
        out_ref[rows_list[u], :] = o.astype(out_ref.dtype)


def _hg_kernel(q_ref, gf_ref, kf_ref, gb_ref, kb_ref, v_ref,
               qc_ref, gfc_ref, kfc_ref, gbc_ref, kbc_ref, vc_ref,
               o_ref, oc_ref, tri_ref, cumb_ref, sbs_ref, sf_ref, sb_ref):
    c = HG_CHUNK
    n_i = lax.broadcasted_iota(jnp.int32, (c, c), 0)
    m_i = lax.broadcasted_iota(jnp.int32, (c, c), 1)
    pair_xor = n_i ^ m_i
    tri_ref[...] = jnp.where(m_i <= n_i, 1.0, 0.0).astype(BF16)

    def run(qr_ref, gfr_ref, kfr_ref, gbr_ref, kbr_ref, vr_ref, out_ref):
        n_chunks = qr_ref.shape[0] // c
        unroll = math.gcd(HG_UNROLL, n_chunks)
        n_steps = n_chunks // unroll

        def chunk_rows(ci):
            return pl.ds(pl.multiple_of(ci * c, c), c)

        local_unroll = math.gcd(HG_LOCAL_UNROLL, n_chunks)

        def bwd_local(i, carry):
            cis = [i * local_unroll + u for u in range(local_unroll)]
            gs = [gbr_ref[chunk_rows(ci), :] for ci in cis]
            cums = [_hg_cumsum(tri_ref, g) for g in gs]
            k_decs = [(kbr_ref[chunk_rows(ci), :].astype(F32) * jnp.exp2(cum - g)).astype(BF16)
                      for ci, g, cum in zip(cis, gs, cums)]
            kvs = [_dot_tn(vr_ref[chunk_rows(ci), :], k_dec) for ci, k_dec in zip(cis, k_decs)]
            for ci, cum, kv in zip(cis, cums, kvs):
                cumb_ref[chunk_rows(ci), :] = cum
                sbs_ref[ci] = kv
            return carry
        lax.fori_loop(0, n_chunks // local_unroll, bwd_local, 0)

        def bwd_state(i, carry):
            s = sb_ref[...]
            for u in range(unroll):
                ci = n_chunks - 1 - (i * unroll + u)
                kv = sbs_ref[ci]
                sbs_ref[ci] = s
                last_tile = cumb_ref[pl.ds(pl.multiple_of(ci * c + c - V7X_SUBLANES, V7X_SUBLANES),
                                           V7X_SUBLANES), :]
                s = s * jnp.exp2(last_tile[V7X_SUBLANES - 1:, :]) + kv
            sb_ref[...] = s
            return carry
        lax.fori_loop(0, n_steps, bwd_state, 0)

        def fwd(i, carry):
            cis = [i * unroll + u for u in range(unroll)]
            _hg_step_out([chunk_rows(ci) for ci in cis], cis, qr_ref, gfr_ref, kfr_ref, gbr_ref, kbr_ref,
                         vr_ref, tri_ref, cumb_ref, sbs_ref, sf_ref, out_ref, pair_xor)
            return carry
        lax.fori_loop(0, n_steps, fwd, 0)

    sf_ref[...] = jnp.zeros_like(sf_ref)
    sb_ref[...] = jnp.zeros_like(sb_ref)
    run(qc_ref, gfc_ref, kfc_ref, gbc_ref, kbc_ref, vc_ref, oc_ref)
    run(q_ref, gf_ref, kf_ref, gb_ref, kb_ref, v_ref, o_ref)


def _hgrn(qiv, log_f, key, n_batch, seq, ctx_len):
    dk = HG_EXPAND
    heads = log_f.shape[1] // (2 * dk)
    d = heads * dk
    assert seq % HG_CHUNK == 0 and ctx_len % HG_CHUNK == 0 and ctx_len <= seq
    ctx_row0 = (n_batch * seq) // ctx_len
    lat = lambda part: pl.BlockSpec((seq, dk), lambda b, h: (b, part * heads + h))
    ctx = lambda part: pl.BlockSpec((ctx_len, dk), lambda b, h: (ctx_row0 + b, part * heads + h))
    return pl.pallas_call(
        _hg_kernel,
        grid=(n_batch, heads),
        in_specs=[lat(0), lat(0), lat(0), lat(1), lat(1), lat(1),
                  ctx(0), ctx(0), ctx(0), ctx(1), ctx(1), ctx(1)],
        out_specs=[pl.BlockSpec((seq, dk), lambda b, h: (b, h)),
                   pl.BlockSpec((ctx_len, dk), lambda b, h: (b, h))],
        out_shape=[jax.ShapeDtypeStruct((n_batch * seq, d), BF16),
                   jax.ShapeDtypeStruct((n_batch * ctx_len, d), BF16)],
        scratch_shapes=[
            pltpu.VMEM((HG_CHUNK, HG_CHUNK), BF16),
            pltpu.VMEM((seq, dk), F32),
            pltpu.VMEM((seq // HG_CHUNK, dk, dk), F32),
            pltpu.VMEM((dk, dk), F32),
            pltpu.VMEM((dk, dk), F32),
        ],
        compiler_params=_params("parallel", "arbitrary"),
        name="hgrn",
    )(qiv, log_f, key, log_f, key, qiv, qiv, log_f, key, log_f, key, qiv)


def _rope_tables(seq, head_dim):
    quarter = head_dim // 4
    rows = jnp.repeat(jnp.arange(seq // GRID_W, dtype=F32), GRID_W)
    cols = jnp.tile(jnp.arange(GRID_W, dtype=F32), seq // GRID_W)
    inv_freq = ROPE_BASE ** (-jnp.arange(quarter, dtype=F32) / quarter)
    ang = jnp.concatenate([rows[:, None] * inv_freq, cols[:, None] * inv_freq], axis=-1)
    return jnp.cos(ang), jnp.sin(ang)


def kernel(x, c, ctx, c_ctx, ada_w, ada_b, norm1_g, norm2_g, ret_w_in, ret_w_out, ret_decay_logits,
           hg_w_in, hg_w_out, hg_norm_g, hg_lower_bounds, ffn_w_gate_up, ffn_w_down, final_norm_g):
    n_batch, seq, d = x.shape
    ctx_len = ctx.shape[1]
    depth = ada_w.shape[0]
    n_lat = n_batch * seq
    n_all = n_lat + n_batch * ctx_len
    assert seq % BIG_ROW_TILE == 0 and (n_batch * ctx_len) % BIG_ROW_TILE == 0

    cond_rows = -(-(n_batch + 1) // V7X_SUBLANES) * V7X_SUBLANES
    cond = jnp.zeros((cond_rows, d), F32).at[:n_batch].set(c).at[n_batch].set(c_ctx)
    mods = _ada_mod(cond, ada_w, ada_b).reshape(depth, cond_rows, 1, 6 * d)

    lb_p = jax.nn.softmax(hg_lower_bounds.astype(F32), axis=0)
    lower_bounds = jnp.cumsum(lb_p, axis=0) - lb_p[0]
    log_gamma = jax.nn.log_sigmoid(ret_decay_logits.astype(F32))
    cos, sin = _rope_tables(seq, d // RET_HEADS)

    ret_in, ret_out = ret_w_in, ret_w_out.astype(BF16)
    hg_in, hg_out = hg_w_in, hg_w_out.astype(BF16)
    ffn_gate_up, ffn_down = ffn_w_gate_up, ffn_w_down.astype(BF16)

    xs, h1 = _prenorm(x.reshape(n_lat, d), ctx.reshape(n_batch * ctx_len, d), norm1_g[0], mods[0],
                      seq, n_batch)
    for layer in range(depth):
        last = layer == depth - 1
        j = layer // N_MIXERS
        mod = mods[layer]
        n_rows = n_lat if last else n_all
        retention = layer % N_MIXERS == 0
        w_in, w_out = (ret_in, ret_out) if retention else (hg_in, hg_out)
        tiles_per_part = d // IN_COL_TILE
        if retention:
            proj = _proj(h1, w_in, j, (0, w_in.shape[2], 0), 0)
            o_lat, o_ctx = _retention(proj, log_gamma[j], cos, sin, n_batch, seq, ctx_len)
            norm_gain = jnp.ones((d,), F32)
        else:
            proj = _proj(h1, w_in, j, (0, tiles_per_part, 2 * tiles_per_part), tiles_per_part)
            log_f, key = _proj_gates(h1, w_in, j, lower_bounds[j], tiles_per_part, 2 * tiles_per_part)
            o_lat, o_ctx = _hgrn(proj, log_f, key, n_batch, seq, ctx_len)
            norm_gain = hg_norm_g[j]
        gate_block = proj.shape[1] // d - 1
        xs, h2 = _out_proj(o_lat, o_ctx, proj, gate_block, norm_gain, w_out, j, xs, mod, norm2_g[layer],
                           seq, n_batch, n_rows, not retention)
        next_gain, next_mod = (final_norm_g, None) if last else (norm1_g[layer + 1], mods[layer + 1])
        xs, h1 = _ffn(h2, ffn_gate_up, ffn_down, layer, xs, mod, next_gain, next_mod, seq, n_batch, n_rows)
    return xs.reshape(n_batch, seq, d)
```

```python
import functools
import math

import jax
import jax.numpy as jnp
from jax import lax
from jax.experimental import pallas as pl
from jax.experimental.pallas import tpu as pltpu

F32 = jnp.float32
BF16 = jnp.bfloat16

EPS = 1e-6
LOG2E = 1.4426950408889634
ROPE_BASE = 10000.0
GRID_W = 64
N_MIXERS = 2
RET_HEADS = 8
HG_EXPAND = 128

V7X_LANES = 128
V7X_SUBLANES = 8
V7X_VMEM_BYTES = 64 * 1024 * 1024

ROW_TILE = 512
BIG_ROW_TILE = 1024
IN_COL_TILE = 1024
FFN_HID_TILE = 512
MXU_COLS = 256
ADA_COL_TILE = 1024
RET_CHUNK = 256
RET_UNROLL = 8
HG_CHUNK = 128
HG_UNROLL = 16
HG_LOCAL_UNROLL = 16
ROW_STEP = 64
NORM_STEP = 128
CAST_STEP = 256
VMEM_LIMIT = 56 * 1024 * 1024


def _params(*semantics):
    return pltpu.CompilerParams(dimension_semantics=semantics, vmem_limit_bytes=VMEM_LIMIT)


def _sigmoid(x):
    return 1.0 / (1.0 + jnp.exp(-x))


def _dot(a, b):
    return jnp.dot(a, b, preferred_element_type=F32)


def _dot_nt(a, b):
    return lax.dot_general(a, b, (((1,), (1,)), ((), ())), preferred_element_type=F32)


def _dot_tn(a, b):
    return lax.dot_general(a, b, (((0,), (0,)), ((), ())), preferred_element_type=F32)


def _ada_kernel(c_ref, w_ref, b_ref, o_ref):
    c = c_ref[...]
    a = (c * _sigmoid(c)).astype(BF16)
    o_ref[...] = _dot(a, w_ref[...].astype(BF16)) + b_ref[...]


def _ada_mod(cond, ada_w, ada_b):
    depth, d, n = ada_w.shape
    rows = cond.shape[0]
    return pl.pallas_call(
        _ada_kernel,
        grid=(depth, n // ADA_COL_TILE),
        in_specs=[
            pl.BlockSpec((rows, d), lambda l, j: (0, 0)),
            pl.BlockSpec((None, d, ADA_COL_TILE), lambda l, j: (l, 0, j)),
            pl.BlockSpec((None, 1, ADA_COL_TILE), lambda l, j: (l, 0, j)),
        ],
        out_specs=pl.BlockSpec((None, rows, ADA_COL_TILE), lambda l, j: (l, 0, j)),
        out_shape=jax.ShapeDtypeStruct((depth, rows, n), F32),
        compiler_params=_params("parallel", "parallel"),
        name="ada_mod",
    )(cond, ada_w, ada_b.reshape(depth, 1, n))


def _norm_mod(x, gain, shift, scale):
    return x * lax.rsqrt(jnp.mean(x * x, axis=-1, keepdims=True) + EPS) * (gain * (1.0 + scale)) + shift


def _row_loop(rows, body, step_rows=ROW_STEP):
    def step(i, carry):
        body(pl.ds(pl.multiple_of(i * step_rows, step_rows), step_rows))
        return carry
    lax.fori_loop(0, rows // step_rows, step, 0)


def _mod_spec(width, slot, row_tile, rows_per_batch, n_batch, col_axis=None):
    def index(*ids):
        col = slot if col_axis is None else slot + ids[col_axis]
        return (jnp.minimum((ids[0] * row_tile) // rows_per_batch, n_batch), 0, col)
    return pl.BlockSpec((None, 1, width), index)


def _prenorm_kernel(xl_ref, xc_ref, g_ref, sh_ref, sc_ref, xs_ref, h_ref, *, n_lat_tiles):
    def copy_norm(x_ref):
        def body(rows):
            x = x_ref[rows, :]
            xs_ref[rows, :] = x
            h_ref[rows, :] = _norm_mod(x, g_ref[...], sh_ref[...], sc_ref[...]).astype(BF16)
        _row_loop(ROW_TILE, body, NORM_STEP)

    tile = pl.program_id(0)
    pl.when(tile < n_lat_tiles)(functools.partial(copy_norm, xl_ref))
    pl.when(tile >= n_lat_tiles)(functools.partial(copy_norm, xc_ref))


def _prenorm(x_lat, x_ctx, gain, mod, rows_per_batch, n_batch):
    d = x_lat.shape[1]
    tm = ROW_TILE
    n_lat_tiles = x_lat.shape[0] // tm
    r = x_lat.shape[0] + x_ctx.shape[0]
    row = lambda i: (i, 0)
    return pl.pallas_call(
        functools.partial(_prenorm_kernel, n_lat_tiles=n_lat_tiles),
        grid=(r // tm,),
        in_specs=[
            pl.BlockSpec((tm, d), lambda i: (jnp.minimum(i, n_lat_tiles - 1), 0)),
            pl.BlockSpec((tm, d), lambda i: (jnp.maximum(i - n_lat_tiles, 0), 0)),
            pl.BlockSpec((1, d), lambda i: (0, 0)),
            _mod_spec(d, 0, tm, rows_per_batch, n_batch),
            _mod_spec(d, 1, tm, rows_per_batch, n_batch),
        ],
        out_specs=[pl.BlockSpec((tm, d), row), pl.BlockSpec((tm, d), row)],
        out_shape=[jax.ShapeDtypeStruct((r, d), F32), jax.ShapeDtypeStruct((r, d), BF16)],
        compiler_params=_params("parallel"),
        name="prenorm",
    )(x_lat, x_ctx, gain.reshape(1, d), mod, mod)


def _col_tiles():
    return [slice(t * MXU_COLS, (t + 1) * MXU_COLS) for t in range(IN_COL_TILE // MXU_COLS)]


def _cast_weight_tile(w_ref, wb_ref):
    @pl.when(pl.program_id(1) == 0)
    def _():
        def body(rows):
            wb_ref[rows, :] = w_ref[rows, :].astype(BF16)
        _row_loop(w_ref.shape[0], body, CAST_STEP)


def _proj_kernel(h_ref, w_ref, o_ref, wb_ref, *, n_silu):
    _cast_weight_tile(w_ref, wb_ref)
    h = h_ref[...]

    def run(silu):
        for cols in _col_tiles():
            a = _dot(h, wb_ref[:, cols])
            if silu:
                a = a * _sigmoid(a)
            o_ref[:, cols] = a.astype(o_ref.dtype)

    if n_silu == 0:
        run(False)
    else:
        j = pl.program_id(0)
        pl.when(j < n_silu)(functools.partial(run, True))
        pl.when(j >= n_silu)(functools.partial(run, False))


def _proj(h, w, layer, col_tiles, n_silu):
    r, d = h.shape
    tm, tn = BIG_ROW_TILE, IN_COL_TILE
    first, skip_from, skip = col_tiles
    n_tiles = w.shape[2] // tn - first - skip
    wcol = lambda j, i: (layer, 0, first + j + jnp.where(j >= skip_from, skip, 0))
    return pl.pallas_call(
        functools.partial(_proj_kernel, n_silu=n_silu),
        grid=(n_tiles, r // tm),
        in_specs=[pl.BlockSpec((tm, d), lambda j, i: (i, 0)), pl.BlockSpec((None, d, tn), wcol)],
        out_specs=pl.BlockSpec((tm, tn), lambda j, i: (i, j)),
        out_shape=jax.ShapeDtypeStruct((r, n_tiles * tn), BF16),
        scratch_shapes=[pltpu.VMEM((d, tn), BF16)],
        compiler_params=_params("arbitrary", "arbitrary"),
        name="in_proj",
    )(h, w)


def _proj_gates_kernel(h_ref, w_ref, lb_ref, g_ref, key_ref, wb_ref):
    _cast_weight_tile(w_ref, wb_ref)
    h = h_ref[...]
    for cols in _col_tiles():
        _, g, key = _hg_gates(_dot(h, wb_ref[:, cols]), lb_ref[:, cols])
        g_ref[:, cols] = g
        key_ref[:, cols] = key.astype(key_ref.dtype)


def _proj_gates(h, w, layer, lower_bound, first_tile, n_tiles):
    r, d = h.shape
    tm, tn = BIG_ROW_TILE, IN_COL_TILE
    lb_tiles = lower_bound.shape[0] // tn
    out = pl.BlockSpec((tm, tn), lambda j, i: (i, j))
    return pl.pallas_call(
        _proj_gates_kernel,
        grid=(n_tiles, r // tm),
        in_specs=[pl.BlockSpec((tm, d), lambda j, i: (i, 0)),
                  pl.BlockSpec((None, d, tn), lambda j, i: (layer, 0, first_tile + j)),
                  pl.BlockSpec((1, tn), lambda j, i: (0, j % lb_tiles))],
        out_specs=[out, out],
        out_shape=[jax.ShapeDtypeStruct((r, n_tiles * tn), F32),
                   jax.ShapeDtypeStruct((r, n_tiles * tn), BF16)],
        scratch_shapes=[pltpu.VMEM((d, tn), BF16)],
        compiler_params=_params("arbitrary", "arbitrary"),
        name="in_proj_gates",
    )(h, w, lower_bound.reshape(1, -1))


def _outproj_kernel(ol_ref, oc_ref, g_ref, ng_ref, w_ref, x_ref, gt_ref, n2_ref, sh_ref, sc_ref,
                    xo_ref, h_ref, y_ref, *, hgrn, n_lat_tiles):
    def gate_from(o_ref):
        def gate(rows):
            o = o_ref[rows, :].astype(F32)
            g = g_ref[rows, :].astype(F32)
            if hgrn:
                o = o * lax.rsqrt(jnp.mean(o * o, axis=-1, keepdims=True) + EPS) * ng_ref[...]
                y = o * _sigmoid(g)
            else:
                y = o * (g * _sigmoid(g))
            y_ref[rows, :] = y.astype(BF16)
        _row_loop(ROW_TILE, gate)

    tile = pl.program_id(0)
    pl.when(tile < n_lat_tiles)(functools.partial(gate_from, ol_ref))
    pl.when(tile >= n_lat_tiles)(functools.partial(gate_from, oc_ref))

    y = y_ref[...]
    for t in range(w_ref.shape[1] // MXU_COLS):
        cols = slice(t * MXU_COLS, (t + 1) * MXU_COLS)
        xo_ref[:, cols] = x_ref[:, cols] + gt_ref[:, cols] * _dot(y, w_ref[:, cols])

    def norm(rows):
        h_ref[rows, :] = _norm_mod(xo_ref[rows, :], n2_ref[...], sh_ref[...], sc_ref[...]).astype(BF16)
    _row_loop(ROW_TILE, norm, NORM_STEP)


def _out_proj(o_lat, o_ctx, proj, gate_block, norm_gain, w, layer, x, mod, gain2, rows_per_batch, n_batch,
              n_rows, hgrn):
    d = x.shape[1]
    tm = ROW_TILE
    n_lat_tiles = o_lat.shape[0] // tm
    row = lambda i: (i, 0)
    const = lambda i: (0, 0)
    return pl.pallas_call(
        functools.partial(_outproj_kernel, hgrn=hgrn, n_lat_tiles=n_lat_tiles),
        grid=(n_rows // tm,),
        in_specs=[
            pl.BlockSpec((tm, d), lambda i: (jnp.minimum(i, n_lat_tiles - 1), 0)),
            pl.BlockSpec((tm, d), lambda i: (jnp.maximum(i - n_lat_tiles, 0), 0)),
            pl.BlockSpec((tm, d), lambda i: (i, gate_block)),
            pl.BlockSpec((1, d), const),
            pl.BlockSpec((None, d, d), lambda i: (layer, 0, 0)),
            pl.BlockSpec((tm, d), row),
            _mod_spec(d, 2, tm, rows_per_batch, n_batch),
            pl.BlockSpec((1, d), const),
            _mod_spec(d, 3, tm, rows_per_batch, n_batch),
            _mod_spec(d, 4, tm, rows_per_batch, n_batch),
        ],
        out_specs=[pl.BlockSpec((tm, d), row), pl.BlockSpec((tm, d), row)],
        out_shape=[jax.ShapeDtypeStruct((n_rows, d), F32), jax.ShapeDtypeStruct((n_rows, d), BF16)],
        scratch_shapes=[pltpu.VMEM((tm, d), BF16)],
        compiler_params=_params("parallel"),
        name="out_proj",
    )(o_lat, o_ctx, proj, norm_gain.reshape(1, d), w, x, mod, gain2.reshape(1, d), mod, mod)


def _gateup_kernel(h_ref, wg_ref, wu_ref, p_ref, wgb_ref, wub_ref):
    _cast_weight_tile(wg_ref, wgb_ref)
    _cast_weight_tile(wu_ref, wub_ref)
    h = h_ref[...]
    for t in range(FFN_HID_TILE // MXU_COLS):
        cols = slice(t * MXU_COLS, (t + 1) * MXU_COLS)
        a = _dot(h, wgb_ref[:, cols])
        b = _dot(h, wub_ref[:, cols])
        p_ref[:, cols] = (a * _sigmoid(a) * b).astype(BF16)


def _down_kernel(p_ref, wd_ref, x_ref, gt_ref, ng_ref, sh_ref, sc_ref, *out_refs, final):
    xo_ref, h_ref = (out_refs[1], out_refs[0]) if final else out_refs
    p = p_ref[...]
    for t in range(wd_ref.shape[1] // MXU_COLS):
        cols = slice(t * MXU_COLS, (t + 1) * MXU_COLS)
        xo_ref[:, cols] = x_ref[:, cols] + gt_ref[:, cols] * _dot(p, wd_ref[:, cols])

    def norm(rows):
        x = xo_ref[rows, :]
        if final:
            h_ref[rows, :] = x * lax.rsqrt(jnp.mean(x * x, axis=-1, keepdims=True) + EPS) * ng_ref[...]
        else:
            h_ref[rows, :] = _norm_mod(x, ng_ref[...], sh_ref[...], sc_ref[...]).astype(BF16)
    _row_loop(ROW_TILE, norm, NORM_STEP)


def _ffn(h2, w_gate_up, w_down, layer, x, mod, next_gain, next_mod, rows_per_batch, n_batch, n_rows):
    d = x.shape[1]
    hidden = w_down.shape[1]
    tm = BIG_ROW_TILE
    n_hid = hidden // FFN_HID_TILE
    p = pl.pallas_call(
        _gateup_kernel,
        grid=(n_hid, n_rows // tm),
        in_specs=[
            pl.BlockSpec((tm, d), lambda j, i: (i, 0)),
            pl.BlockSpec((None, d, FFN_HID_TILE), lambda j, i: (layer, 0, j)),
            pl.BlockSpec((None, d, FFN_HID_TILE), lambda j, i: (layer, 0, j + n_hid)),
        ],
        out_specs=pl.BlockSpec((tm, FFN_HID_TILE), lambda j, i: (i, j)),
        out_shape=jax.ShapeDtypeStruct((n_rows, hidden), BF16),
        scratch_shapes=[pltpu.VMEM((d, FFN_HID_TILE), BF16), pltpu.VMEM((d, FFN_HID_TILE), BF16)],
        compiler_params=_params("arbitrary", "arbitrary"),
        name="ffn_gate_up",
    )(h2, w_gate_up, w_gate_up)
    tm = ROW_TILE
    final = next_mod is None
    row = lambda i: (i, 0)
    const = lambda i: (0, 0)
    norm_mod = mod if final else next_mod
    x_out = jax.ShapeDtypeStruct((n_rows, d), F32)
    outs = pl.pallas_call(
        functools.partial(_down_kernel, final=final),
        grid=(n_rows // tm,),
        in_specs=[
            pl.BlockSpec((tm, hidden), row),
            pl.BlockSpec((None, hidden, d), lambda i: (layer, 0, 0)),
            pl.BlockSpec((tm, d), row),
            _mod_spec(d, 5, tm, rows_per_batch, n_batch),
            pl.BlockSpec((1, d), const),
            _mod_spec(d, 0, tm, rows_per_batch, n_batch),
            _mod_spec(d, 1, tm, rows_per_batch, n_batch),
        ],
        out_specs=[pl.BlockSpec((tm, d), row)] * (1 if final else 2),
        out_shape=[x_out] if final else [x_out, jax.ShapeDtypeStruct((n_rows, d), BF16)],
        scratch_shapes=[pltpu.VMEM((tm, d), F32)] if final else [],
        compiler_params=_params("parallel"),
        name="ffn_down",
    )(p, w_down, x, mod, next_gain.reshape(1, d), norm_mod, norm_mod)
    return (outs[0], None) if final else outs


def _ret_kernel(lg_ref, q_ref, k_ref, v_ref, qc_ref, kc_ref, vc_ref, cos_ref, sin_ref,
                o_ref, oc_ref, dec_ref, kr_ref, sbs_ref, sf_ref, sb_ref, *, k_scale):
    c = RET_CHUNK
    dk = q_ref.shape[1]
    half = dk // 2
    n_chunks = q_ref.shape[0] // c
    head = pl.program_id(1)
    lgf = lg_ref[0, head]
    lgb = lg_ref[1, head]

    n_i = lax.broadcasted_iota(jnp.int32, (c, c), 0).astype(F32)
    m_i = lax.broadcasted_iota(jnp.int32, (c, c), 1).astype(F32)
    diff = n_i - m_i
    dec_ref[0] = (jnp.where(diff >= 0, jnp.exp(lgf * jnp.maximum(diff, 0.0)), 0.0)
                  + jnp.where(diff <= 0, jnp.exp(lgb * jnp.maximum(-diff, 0.0)), 0.0))
    t_i = lax.broadcasted_iota(jnp.int32, (c, dk), 0).astype(F32)
    dec_ref[1] = jnp.exp(lgf * (t_i + 1.0))
    dec_ref[2] = jnp.exp(lgb * (c - t_i))
    dec_ref[3] = jnp.exp(lgf * (c - 1.0 - t_i))
    dec_ref[4] = jnp.exp(lgb * t_i)
    cf = jnp.exp(lgf * c)
    cb = jnp.exp(lgb * c)

    def head_norm(o):
        return o * lax.rsqrt(jnp.mean(o * o, axis=-1, keepdims=True) + EPS)

    def rope(t, rows):
        cos = cos_ref[rows, :]
        sin = sin_ref[rows, :]
        t1 = t[:, :half]
        t2 = t[:, half:]
        return jnp.concatenate([t1 * cos - t2 * sin, t1 * sin + t2 * cos], axis=-1)

    qc = qc_ref[...]
    kc = (kc_ref[...].astype(F32) * k_scale)
    vc = vc_ref[...]
    sc = _dot_nt(qc, kc.astype(BF16)) * dec_ref[0]
    oc_ref[...] = head_norm(_dot(sc.astype(BF16), vc)).astype(oc_ref.dtype)
    sf_ref[...] = _dot_tn((kc * dec_ref[3]).astype(BF16), vc)
    sb_ref[...] = _dot_tn((kc * dec_ref[4]).astype(BF16), vc)

    unroll = math.gcd(RET_UNROLL, n_chunks)
    n_steps = n_chunks // unroll

    def chunk_rows(ci):
        return pl.ds(pl.multiple_of(ci * c, c), c)

    def bwd_local(i, carry):
        cis = [i * unroll + u for u in range(unroll)]
        rows = [chunk_rows(ci) for ci in cis]
        krs = [rope(k_ref[r, :].astype(F32), r) * k_scale for r in rows]
        kvs = [_dot_tn((kr * dec_ref[4]).astype(BF16), v_ref[r, :]) for kr, r in zip(krs, rows)]
        for ci, r, kr, kv in zip(cis, rows, krs, kvs):
            kr_ref[r, :] = kr.astype(BF16)
            sbs_ref[ci] = kv
        return carry
    lax.fori_loop(0, n_steps, bwd_local, 0)

    def bwd_state(i, carry):
        ci = n_chunks - 1 - i
        s = sb_ref[...]
        kv = sbs_ref[ci]
        sbs_ref[ci] = s
        sb_ref[...] = s * cb + kv
        return carry
    lax.fori_loop(0, n_chunks, bwd_state, 0)

    def fwd(i, carry):
        cis = [i * unroll + u for u in range(unroll)]
        rows = [chunk_rows(ci) for ci in cis]
        qrs = [rope(q_ref[r, :].astype(F32), r).astype(BF16) for r in rows]
        krs = [kr_ref[r, :] for r in rows]
        vs = [v_ref[r, :] for r in rows]
        scs = [_dot_nt(qr, kr) for qr, kr in zip(qrs, krs)]
        kvs = [_dot_tn((kr.astype(F32) * dec_ref[3]).astype(BF16), v) for kr, v in zip(krs, vs)]
        sfs = [sf_ref[...]]
        for kv in kvs:
            sfs.append(sfs[-1] * cf + kv)
        sf_ref[...] = sfs[-1]
        intra = [_dot((sc * dec_ref[0]).astype(BF16), v) for sc, v in zip(scs, vs)]
        inter_f = [_dot(qr, s.astype(BF16)) for qr, s in zip(qrs, sfs)]
        inter_b = [_dot(qr, sbs_ref[ci].astype(BF16)) for qr, ci in zip(qrs, cis)]
        for r, o1, o2, o3 in zip(rows, intra, inter_f, inter_b):
            o = o1 + dec_ref[1] * o2 + dec_ref[2] * o3
            o_ref[r, :] = head_norm(o).astype(o_ref.dtype)
        return carry
    lax.fori_loop(0, n_steps, fwd, 0)


def _retention(proj, log_gamma, cos, sin, n_batch, seq, ctx_len):
    heads = RET_HEADS
    dk = proj.shape[1] // (4 * heads)
    d = heads * dk
    assert dk == RET_CHUNK and ctx_len == RET_CHUNK and seq % RET_CHUNK == 0
    ctx_row0 = (n_batch * seq) // ctx_len
    lat = lambda part: pl.BlockSpec((seq, dk), lambda b, h: (b, part * heads + h))
    ctx = lambda part: pl.BlockSpec((ctx_len, dk), lambda b, h: (ctx_row0 + b, part * heads + h))
    tab = pl.BlockSpec((seq, dk // 2), lambda b, h: (0, 0))
    n_chunks = seq // RET_CHUNK
    return pl.pallas_call(
        functools.partial(_ret_kernel, k_scale=dk ** -0.5),
        grid=(n_batch, heads),
        in_specs=[pl.BlockSpec(memory_space=pltpu.SMEM),
                  lat(0), lat(1), lat(2), ctx(0), ctx(1), ctx(2), tab, tab],
        out_specs=[pl.BlockSpec((seq, dk), lambda b, h: (b, h)),
                   pl.BlockSpec((ctx_len, dk), lambda b, h: (b, h))],
        out_shape=[jax.ShapeDtypeStruct((n_batch * seq, d), BF16),
                   jax.ShapeDtypeStruct((n_batch * ctx_len, d), BF16)],
        scratch_shapes=[
            pltpu.VMEM((5, RET_CHUNK, RET_CHUNK), F32),
            pltpu.VMEM((seq, dk), BF16),
            pltpu.VMEM((n_chunks, dk, dk), F32),
            pltpu.VMEM((dk, dk), F32),
            pltpu.VMEM((dk, dk), F32),
        ],
        compiler_params=_params("parallel", "arbitrary"),
        name="retention",
    )(log_gamma, proj, proj, proj, proj, proj, proj, cos, sin)


def _hg_gates(z, lb):
    f = lb + (1.0 - lb) * _sigmoid(z)
    return f, jnp.log2(f), 1.0 - f


def _hg_cumsum(tri_ref, g):
    dk = g.shape[1]
    hi = g.astype(BF16)
    lo = (g - hi.astype(F32)).astype(BF16)
    r = _dot(tri_ref[...], jnp.concatenate([hi, lo], axis=1))
    return r[:, :dk] + r[:, dk:]


def _tiles(x):
    return [x[j * V7X_SUBLANES:(j + 1) * V7X_SUBLANES] for j in range(x.shape[0] // V7X_SUBLANES)]


def _row_of_tile(tile, r):
    return jnp.broadcast_to(tile[r:r + 1, :], tile.shape)


def _hg_levels(up, lo, cv, cin):
    upt, lot, cvt, cint = _tiles(up), _tiles(lo), _tiles(cv), _tiles(cin)
    nt = len(cvt)
    zero = jnp.zeros_like(cvt[0])
    lastt = [_row_of_tile(t, V7X_SUBLANES - 1) for t in cint]
    out = []
    bt = nt
    while bt >= 2:
        ht = bt // 2
        hi_rows, lo_rows = [], []
        for j in range(nt):
            b0 = (j // bt) * bt
            ref = lastt[b0 + ht - 1]
            if j - b0 >= ht:
                hi_rows.append(jnp.exp2(cvt[j] - ref) * upt[j])
                lo_rows.append(zero)
            else:
                hi_rows.append(zero)
                lo_rows.append(jnp.exp2(ref - cvt[j]) * lot[j])
        out.append((jnp.concatenate(hi_rows, axis=0), jnp.concatenate(lo_rows, axis=0)))
        bt = ht
    sub = lax.broadcasted_iota(jnp.int32, zero.shape, 0)
    for size in (8, 4):
        upper = (sub & (size // 2)) != 0
        sign = jnp.where(upper, 1.0, -1.0)
        hi_rows, lo_rows = [], []
        for j in range(nt):
            if size == 8:
                ref = _row_of_tile(cint[j], 3)
            else:
                ref = jnp.where(sub < 4, _row_of_tile(cint[j], 1), _row_of_tile(cint[j], 5))
            z = jnp.exp2((cvt[j] - ref) * sign) * jnp.where(upper, upt[j], lot[j])
            hi_rows.append(jnp.where(upper, z, 0.0))
            lo_rows.append(jnp.where(upper, 0.0, z))
        out.append((jnp.concatenate(hi_rows, axis=0), jnp.concatenate(lo_rows, axis=0)))
    return out


def _hg_bwd_local(rows, ci, g_ref, key_ref, v_ref, tri_ref, cumb_ref, kv_ref):
    g = g_ref[rows, :]
    cum = _hg_cumsum(tri_ref, g)
    cumb_ref[rows, :] = cum
    k_dec = key_ref[rows, :].astype(F32) * jnp.exp2(cum - g)
    kv_ref[ci] = _dot_tn(v_ref[rows, :], k_dec.astype(BF16))


def _hg_step_out(rows_list, cis, q_ref, gf_ref, kf_ref, gb_ref, kb_ref, v_ref, tri_ref, cumb_ref,
                 sbs_ref, sf_ref, out_ref, pair_xor):
    n = len(cis)
    q = [q_ref[r, :].astype(F32) for r in rows_list]
    kf = [kf_ref[r, :].astype(F32) for r in rows_list]
    kb = [kb_ref[r, :].astype(F32) for r in rows_list]
    v = [v_ref[r, :] for r in rows_list]
    gf = [gf_ref[r, :] for r in rows_list]
    gb = [gb_ref[r, :] for r in rows_list]
    c = q[0].shape[0]

    cumf = [_hg_cumsum(tri_ref, g) for g in gf]
    cumb = [cumb_ref[r, :] for r in rows_list]
    cumxb = [cb - g for cb, g in zip(cumb, gb)]
    totf = [cf[c - 1:, :] for cf in cumf]
    totb = [cb[c - 1:, :] for cb in cumb]

    kvs = [_dot_tn(v[u], (kf[u] * jnp.exp2(totf[u] - cumf[u])).astype(BF16)) for u in range(n)]
    sf = [sf_ref[...]]
    for u in range(n):
        sf.append(sf[u] * jnp.exp2(totf[u]) + kvs[u])
    sf_ref[...] = sf[n]

    odd = (lax.broadcasted_iota(jnp.int32, q[0].shape, 0) & 1) != 0
    scores = []
    for u in range(n):
        lev_f = _hg_levels(q[u], kf[u], cumf[u], cumf[u])
        lev_b = _hg_levels(kb[u], q[u], cumxb[u], cumb[u])
        lev_f.append((jnp.where(odd, q[u] * jnp.exp2(gf[u]), 0.0), jnp.where(odd, 0.0, kf[u])))
        lev_b.append((jnp.where(odd, kb[u], 0.0), jnp.where(odd, 0.0, q[u] * jnp.exp2(gb[u]))))
        a = None
        size = c
        for (xf, yf), (yb, xb) in zip(lev_f, lev_b):
            x = jnp.concatenate([xf.astype(BF16), xb.astype(BF16)], axis=1)
            y = jnp.concatenate([yf.astype(BF16), yb.astype(BF16)], axis=1)
            p = _dot_nt(x, y)
            a = p if a is None else jnp.where(pair_xor < size, p, a)
            size //= 2
        scores.append(a.astype(BF16))

    inter = []
    for u in range(n):
        q_dec = jnp.concatenate([(q[u] * jnp.exp2(cumf[u])).astype(BF16),
                                 (q[u] * jnp.exp2(totb[u] - cumxb[u])).astype(BF16)], axis=1)
        states = jnp.concatenate([sf[u].astype(BF16), sbs_ref[cis[u]].astype(BF16)], axis=1)
        inter.append(_dot_nt(q_dec, states))

    for u in range(n):
        o = _dot(scores[u], v[u]) + inter[u]
        o += jnp.sum(q[u] * (kf[u] + kb[u]), axis=-1, keepdims=True) * v[u].astype(F32)
        out_ref[rows_list[u], :] = o.astype(out_ref.dtype)


def _hg_kernel(q_ref, gf_ref, kf_ref, gb_ref, kb_ref, v_ref,
               qc_ref, gfc_ref, kfc_ref, gbc_ref, kbc_ref, vc_ref,
               o_ref, oc_ref, tri_ref, cumb_ref, sbs_ref, sf_ref, sb_ref):
    c = HG_CHUNK
    n_i = lax.broadcasted_iota(jnp.int32, (c, c), 0)
    m_i = lax.broadcasted_iota(jnp.int32, (c, c), 1)
    pair_xor = n_i ^ m_i
    tri_ref[...] = jnp.where(m_i <= n_i, 1.0, 0.0).astype(BF16)

    def run(qr_ref, gfr_ref, kfr_ref, gbr_ref, kbr_ref, vr_ref, out_ref):
        n_chunks = qr_ref.shape[0] // c
        unroll = math.gcd(HG_UNROLL, n_chunks)
        n_steps = n_chunks // unroll

        def chunk_rows(ci):
            return pl.ds(pl.multiple_of(ci * c, c), c)

        local_unroll = math.gcd(HG_LOCAL_UNROLL, n_chunks)

        def bwd_local(i, carry):
            cis = [i * local_unroll + u for u in range(local_unroll)]
            gs = [gbr_ref[chunk_rows(ci), :] for ci in cis]
            cums = [_hg_cumsum(tri_ref, g) for g in gs]
            k_decs = [(kbr_ref[chunk_rows(ci), :].astype(F32) * jnp.exp2(cum - g)).astype(BF16)
                      for ci, g, cum in zip(cis, gs, cums)]
            kvs = [_dot_tn(vr_ref[chunk_rows(ci), :], k_dec) for ci, k_dec in zip(cis, k_decs)]
            for ci, cum, kv in zip(cis, cums, kvs):
                cumb_ref[chunk_rows(ci), :] = cum
                sbs_ref[ci] = kv
            return carry
        lax.fori_loop(0, n_chunks // local_unroll, bwd_local, 0)

        def bwd_state(i, carry):
            s = sb_ref[...]
            for u in range(unroll):
                ci = n_chunks - 1 - (i * unroll + u)
                kv = sbs_ref[ci]
                sbs_ref[ci] = s
                last_tile = cumb_ref[pl.ds(pl.multiple_of(ci * c + c - V7X_SUBLANES, V7X_SUBLANES),
                                           V7X_SUBLANES), :]
                s = s * jnp.exp2(last_tile[V7X_SUBLANES - 1:, :]) + kv
            sb_ref[...] = s
            return carry
        lax.fori_loop(0, n_steps, bwd_state, 0)

        def fwd(i, carry):
            cis = [i * unroll + u for u in range(unroll)]
            _hg_step_out([chunk_rows(ci) for ci in cis], cis, qr_ref, gfr_ref, kfr_ref, gbr_ref, kbr_ref,
                         vr_ref, tri_ref, cumb_ref, sbs_ref, sf_ref, out_ref, pair_xor)
            return carry
        lax.fori_loop(0, n_steps, fwd, 0)

    sf_ref[...] = jnp.zeros_like(sf_ref)
    sb_ref[...] = jnp.zeros_like(sb_ref)
    run(qc_ref, gfc_ref, kfc_ref, gbc_ref, kbc_ref, vc_ref, oc_ref)
    run(q_ref, gf_ref, kf_ref, gb_ref, kb_ref, v_ref, o_ref)


def _hgrn(qiv, log_f, key, n_batch, seq, ctx_len):
    dk = HG_EXPAND
    heads = log_f.shape[1] // (2 * dk)
    d = heads * dk
    assert seq % HG_CHUNK == 0 and ctx_len % HG_CHUNK == 0 and ctx_len <= seq
    ctx_row0 = (n_batch * seq) // ctx_len
    lat = lambda part: pl.BlockSpec((seq, dk), lambda b, h: (b, part * heads + h))
    ctx = lambda part: pl.BlockSpec((ctx_len, dk), lambda b, h: (ctx_row0 + b, part * heads + h))
    return pl.pallas_call(
        _hg_kernel,
        grid=(n_batch, heads),
        in_specs=[lat(0), lat(0), lat(0), lat(1), lat(1), lat(1),
                  ctx(0), ctx(0), ctx(0), ctx(1), ctx(1), ctx(1)],
        out_specs=[pl.BlockSpec((seq, dk), lambda b, h: (b, h)),
                   pl.BlockSpec((ctx_len, dk), lambda b, h: (b, h))],
        out_shape=[jax.ShapeDtypeStruct((n_batch * seq, d), BF16),
                   jax.ShapeDtypeStruct((n_batch * ctx_len, d), BF16)],
        scratch_shapes=[
            pltpu.VMEM((HG_CHUNK, HG_CHUNK), BF16),
            pltpu.VMEM((seq, dk), F32),
            pltpu.VMEM((seq // HG_CHUNK, dk, dk), F32),
            pltpu.VMEM((dk, dk), F32),
            pltpu.VMEM((dk, dk), F32),
        ],
        compiler_params=_params("parallel", "arbitrary"),
        name="hgrn",
    )(qiv, log_f, key, log_f, key, qiv, qiv, log_f, key, log_f, key, qiv)


def _rope_tables(seq, head_dim):
    quarter = head_dim // 4
    rows = jnp.repeat(jnp.arange(seq // GRID_W, dtype=F32), GRID_W)
    cols = jnp.tile(jnp.arange(GRID_W, dtype=F32), seq // GRID_W)
    inv_freq = ROPE_BASE ** (-jnp.arange(quarter, dtype=F32) / quarter)
    ang = jnp.concatenate([rows[:, None] * inv_freq, cols[:, None] * inv_freq], axis=-1)
    return jnp.cos(ang), jnp.sin(ang)


def kernel(x, c, ctx, c_ctx, ada_w, ada_b, norm1_g, norm2_g, ret_w_in, ret_w_out, ret_decay_logits,
           hg_w_in, hg_w_out, hg_norm_g, hg_lower_bounds, ffn_w_gate_up, ffn_w_down, final_norm_g):
    n_batch, seq, d = x.shape
    ctx_len = ctx.shape[1]
    depth = ada_w.shape[0]
    n_lat = n_batch * seq
    n_all = n_lat + n_batch * ctx_len
    assert seq % BIG_ROW_TILE == 0 and (n_batch * ctx_len) % BIG_ROW_TILE == 0

    cond_rows = -(-(n_batch + 1) // V7X_SUBLANES) * V7X_SUBLANES
    cond = jnp.zeros((cond_rows, d), F32).at[:n_batch].set(c).at[n_batch].set(c_ctx)
    mods = _ada_mod(cond, ada_w, ada_b).reshape(depth, cond_rows, 1, 6 * d)

    lb_p = jax.nn.softmax(hg_lower_bounds.astype(F32), axis=0)
    lower_bounds = jnp.cumsum(lb_p, axis=0) - lb_p[0]
    log_gamma = jax.nn.log_sigmoid(ret_decay_logits.astype(F32))
    cos, sin = _rope_tables(seq, d // RET_HEADS)

    ret_in, ret_out = ret_w_in, ret_w_out.astype(BF16)
    hg_in, hg_out = hg_w_in, hg_w_out.astype(BF16)
    ffn_gate_up, ffn_down = ffn_w_gate_up, ffn_w_down.astype(BF16)

    xs, h1 = _prenorm(x.reshape(n_lat, d), ctx.reshape(n_batch * ctx_len, d), norm1_g[0], mods[0],
                      seq, n_batch)
    for layer in range(depth):
        last = layer == depth - 1
        j = layer // N_MIXERS
        mod = mods[layer]
        n_rows = n_lat if last else n_all
        retention = layer % N_MIXERS == 0
        w_in, w_out = (ret_in, ret_out) if retention else (hg_in, hg_out)
        tiles_per_part = d // IN_COL_TILE
        if retention:
            proj = _proj(h1, w_in, j, (0, w_in.shape[2], 0), 0)
            o_lat, o_ctx = _retention(proj, log_gamma[j], cos, sin, n_batch, seq, ctx_len)
            norm_gain = jnp.ones((d,), F32)
        else:
            proj = _proj(h1, w_in, j, (0, tiles_per_part, 2 * tiles_per_part), tiles_per_part)
            log_f, key = _proj_gates(h1, w_in, j, lower_bounds[j], tiles_per_part, 2 * tiles_per_part)
            o_lat, o_ctx = _hgrn(proj, log_f, key, n_batch, seq, ctx_len)
            norm_gain = hg_norm_g[j]
        gate_block = proj.shape[1] // d - 1
        xs, h2 = _out_proj(o_lat, o_ctx, proj, gate_block, norm_gain, w_out, j, xs, mod, norm2_g[layer],
                           seq, n_batch, n_rows, not retention)
        next_gain, next_mod = (final_norm_g, None) if last else (norm1_g[layer + 1], mods[layer + 1])
        xs, h1 = _ffn(h2, ffn_gate_up, ffn_down, layer, xs, mod, next_gain, next_mod, seq, n_batch, n_rows)
    return xs.reshape(n_batch, seq, d)
```

```python
import functools
import math

import jax
import jax.numpy as jnp
from jax import lax
from jax.experimental import pallas as pl
from jax.experimental.pallas import tpu as pltpu

F32 = jnp.float32
BF16 = jnp.bfloat16

EPS = 1e-6
LOG2E = 1.4426950408889634
ROPE_BASE = 10000.0
GRID_W = 64
N_MIXERS = 2
RET_HEADS = 8
HG_EXPAND = 128

V7X_LANES = 128
V7X_SUBLANES = 8
V7X_VMEM_BYTES = 64 * 1024 * 1024

ROW_TILE = 512
BIG_ROW_TILE = 1024
IN_COL_TILE = 1024
FFN_HID_TILE = 512
MXU_COLS = 256
ADA_COL_TILE = 1024
RET_CHUNK = 256
RET_UNROLL = 8
HG_CHUNK = 128
HG_UNROLL = 16
HG_LOCAL_UNROLL = 16
ROW_STEP = 64
NORM_STEP = 128
CAST_STEP = 256
VMEM_LIMIT = 56 * 1024 * 1024


def _params(*semantics):
    return pltpu.CompilerParams(dimension_semantics=semantics, vmem_limit_bytes=VMEM_LIMIT)


def _sigmoid(x):
    return 1.0 / (1.0 + jnp.exp(-x))


def _dot(a, b):
    return jnp.dot(a, b, preferred_element_type=F32)


def _dot_nt(a, b):
    return lax.dot_general(a, b, (((1,), (1,)), ((), ())), preferred_element_type=F32)


def _dot_tn(a, b):
    return lax.dot_general(a, b, (((0,), (0,)), ((), ())), preferred_element_type=F32)


def _ada_kernel(c_ref, w_ref, b_ref, o_ref):
    c = c_ref[...]
    a = (c * _sigmoid(c)).astype(BF16)
    o_ref[...] = _dot(a, w_ref[...].astype(BF16)) + b_ref[...]


def _ada_mod(cond, ada_w, ada_b):
    depth, d, n = ada_w.shape
    rows = cond.shape[0]
    return pl.pallas_call(
        _ada_kernel,
        grid=(depth, n // ADA_COL_TILE),
        in_specs=[
            pl.BlockSpec((rows, d), lambda l, j: (0, 0)),
            pl.BlockSpec((None, d, ADA_COL_TILE), lambda l, j: (l, 0, j)),
            pl.BlockSpec((None, 1, ADA_COL_TILE), lambda l, j: (l, 0, j)),
        ],
        out_specs=pl.BlockSpec((None, rows, ADA_COL_TILE), lambda l, j: (l, 0, j)),
        out_shape=jax.ShapeDtypeStruct((depth, rows, n), F32),
        compiler_params=_params("parallel", "parallel"),
        name="ada_mod",
    )(cond, ada_w, ada_b.reshape(depth, 1, n))


def _norm_mod(x, gain, shift, scale):
    return x * lax.rsqrt(jnp.mean(x * x, axis=-1, keepdims=True) + EPS) * (gain * (1.0 + scale)) + shift


def _row_loop(rows, body, step_rows=ROW_STEP):
    def step(i, carry):
        body(pl.ds(pl.multiple_of(i * step_rows, step_rows), step_rows))
        return carry
    lax.fori_loop(0, rows // step_rows, step, 0)


def _mod_spec(width, slot, row_tile, rows_per_batch, n_batch, col_axis=None):
    def index(*ids):
        col = slot if col_axis is None else slot + ids[col_axis]
        return (jnp.minimum((ids[0] * row_tile) // rows_per_batch, n_batch), 0, col)
    return pl.BlockSpec((None, 1, width), index)


def _prenorm_kernel(xl_ref, xc_ref, g_ref, sh_ref, sc_ref, xs_ref, h_ref, *, n_lat_tiles):
    def copy_norm(x_ref):
        def body(rows):
            x = x_ref[rows, :]
            xs_ref[rows, :] = x
            h_ref[rows, :] = _norm_mod(x, g_ref[...], sh_ref[...], sc_ref[...]).astype(BF16)
        _row_loop(ROW_TILE, body, NORM_STEP)

    tile = pl.program_id(0)
    pl.when(tile < n_lat_tiles)(functools.partial(copy_norm, xl_ref))
    pl.when(tile >= n_lat_tiles)(functools.partial(copy_norm, xc_ref))


def _prenorm(x_lat, x_ctx, gain, mod, rows_per_batch, n_batch):
    d = x_lat.shape[1]
    tm = ROW_TILE
    n_lat_tiles = x_lat.shape[0] // tm
    r = x_lat.shape[0] + x_ctx.shape[0]
    row = lambda i: (i, 0)
    return pl.pallas_call(
        functools.partial(_prenorm_kernel, n_lat_tiles=n_lat_tiles),
        grid=(r // tm,),
        in_specs=[
            pl.BlockSpec((tm, d), lambda i: (jnp.minimum(i, n_lat_tiles - 1), 0)),
            pl.BlockSpec((tm, d), lambda i: (jnp.maximum(i - n_lat_tiles, 0), 0)),
            pl.BlockSpec((1, d), lambda i: (0, 0)),
            _mod_spec(d, 0, tm, rows_per_batch, n_batch),
            _mod_spec(d, 1, tm, rows_per_batch, n_batch),
        ],
        out_specs=[pl.BlockSpec((tm, d), row), pl.BlockSpec((tm, d), row)],
        out_shape=[jax.ShapeDtypeStruct((r, d), F32), jax.ShapeDtypeStruct((r, d), BF16)],
        compiler_params=_params("parallel"),
        name="prenorm",
    )(x_lat, x_ctx, gain.reshape(1, d), mod, mod)


def _col_tiles():
    return [slice(t * MXU_COLS, (t + 1) * MXU_COLS) for t in range(IN_COL_TILE // MXU_COLS)]


def _cast_weight_tile(w_ref, wb_ref):
    @pl.when(pl.program_id(1) == 0)
    def _():
        def body(rows):
            wb_ref[rows, :] = w_ref[rows, :].astype(BF16)
        _row_loop(w_ref.shape[0], body, CAST_STEP)


def _proj_kernel(h_ref, w_ref, o_ref, wb_ref, *, acts):
    _cast_weight_tile(w_ref, wb_ref)
    h = h_ref[...]

    def run(act):
        for cols in _col_tiles():
            a = _dot(h, wb_ref[:, cols])
            if act == "silu":
                a = a * _sigmoid(a)
            elif act == "sigmoid":
                a = _sigmoid(a)
            o_ref[:, cols] = a.astype(o_ref.dtype)

    j = pl.program_id(0)
    for act in sorted(set(acts), key=str):
        tiles = [t for t, a in enumerate(acts) if a == act]
        lo, hi = tiles[0], tiles[-1] + 1
        assert tiles == list(range(lo, hi))
        pl.when(jnp.logical_and(j >= lo, j < hi))(functools.partial(run, act))


def _proj(h, w, layer, col_tiles, acts):
    r, d = h.shape
    tm, tn = BIG_ROW_TILE, IN_COL_TILE
    first, skip_from, skip = col_tiles
    n_tiles = w.shape[2] // tn - first - skip
    assert len(acts) == n_tiles
    wcol = lambda j, i: (layer, 0, first + j + jnp.where(j >= skip_from, skip, 0))
    return pl.pallas_call(
        functools.partial(_proj_kernel, acts=tuple(acts)),
        grid=(n_tiles, r // tm),
        in_specs=[pl.BlockSpec((tm, d), lambda j, i: (i, 0)), pl.BlockSpec((None, d, tn), wcol)],
        out_specs=pl.BlockSpec((tm, tn), lambda j, i: (i, j)),
        out_shape=jax.ShapeDtypeStruct((r, n_tiles * tn), BF16),
        scratch_shapes=[pltpu.VMEM((d, tn), BF16)],
        compiler_params=_params("arbitrary", "arbitrary"),
        name="in_proj",
    )(h, w)


def _proj_gates_kernel(h_ref, w_ref, lb_ref, g_ref, key_ref, wb_ref):
    _cast_weight_tile(w_ref, wb_ref)
    h = h_ref[...]
    for cols in _col_tiles():
        _, g, key = _hg_gates(_dot(h, wb_ref[:, cols]), lb_ref[:, cols])
        g_ref[:, cols] = g
        key_ref[:, cols] = key.astype(key_ref.dtype)


def _proj_gates(h, w, layer, lower_bound, first_tile, n_tiles):
    r, d = h.shape
    tm, tn = BIG_ROW_TILE, IN_COL_TILE
    lb_tiles = lower_bound.shape[0] // tn
    out = pl.BlockSpec((tm, tn), lambda j, i: (i, j))
    return pl.pallas_call(
        _proj_gates_kernel,
        grid=(n_tiles, r // tm),
        in_specs=[pl.BlockSpec((tm, d), lambda j, i: (i, 0)),
                  pl.BlockSpec((None, d, tn), lambda j, i: (layer, 0, first_tile + j)),
                  pl.BlockSpec((1, tn), lambda j, i: (0, j % lb_tiles))],
        out_specs=[out, out],
        out_shape=[jax.ShapeDtypeStruct((r, n_tiles * tn), F32),
                   jax.ShapeDtypeStruct((r, n_tiles * tn), BF16)],
        scratch_shapes=[pltpu.VMEM((d, tn), BF16)],
        compiler_params=_params("arbitrary", "arbitrary"),
        name="in_proj_gates",
    )(h, w, lower_bound.reshape(1, -1))


def _outproj_kernel(ol_ref, oc_ref, g_ref, ng_ref, w_ref, x_ref, gt_ref, n2_ref, sh_ref, sc_ref,
                    xo_ref, h_ref, y_ref, *, hgrn, n_lat_tiles):
    def gate_from(o_ref):
        def gate(rows):
            o = o_ref[rows, :].astype(F32)
            if hgrn:
                o = o * lax.rsqrt(jnp.mean(o * o, axis=-1, keepdims=True) + EPS) * ng_ref[...]
            y_ref[rows, :] = (o * g_ref[rows, :].astype(F32)).astype(BF16)
        _row_loop(ROW_TILE, gate)

    tile = pl.program_id(0)
    pl.when(tile < n_lat_tiles)(functools.partial(gate_from, ol_ref))
    pl.when(tile >= n_lat_tiles)(functools.partial(gate_from, oc_ref))

    y = y_ref[...]
    for t in range(w_ref.shape[1] // MXU_COLS):
        cols = slice(t * MXU_COLS, (t + 1) * MXU_COLS)
        xo_ref[:, cols] = x_ref[:, cols] + gt_ref[:, cols] * _dot(y, w_ref[:, cols])

    def norm(rows):
        h_ref[rows, :] = _norm_mod(xo_ref[rows, :], n2_ref[...], sh_ref[...], sc_ref[...]).astype(BF16)
    _row_loop(ROW_TILE, norm, NORM_STEP)


def _out_proj(o_lat, o_ctx, proj, gate_block, norm_gain, w, layer, x, mod, gain2, rows_per_batch, n_batch,
              n_rows, hgrn):
    d = x.shape[1]
    tm = ROW_TILE
    n_lat_tiles = o_lat.shape[0] // tm
    row = lambda i: (i, 0)
    const = lambda i: (0, 0)
    return pl.pallas_call(
        functools.partial(_outproj_kernel, hgrn=hgrn, n_lat_tiles=n_lat_tiles),
        grid=(n_rows // tm,),
        in_specs=[
            pl.BlockSpec((tm, d), lambda i: (jnp.minimum(i, n_lat_tiles - 1), 0)),
            pl.BlockSpec((tm, d), lambda i: (jnp.maximum(i - n_lat_tiles, 0), 0)),
            pl.BlockSpec((tm, d), lambda i: (i, gate_block)),
            pl.BlockSpec((1, d), const),
            pl.BlockSpec((None, d, d), lambda i: (layer, 0, 0)),
            pl.BlockSpec((tm, d), row),
            _mod_spec(d, 2, tm, rows_per_batch, n_batch),
            pl.BlockSpec((1, d), const),
            _mod_spec(d, 3, tm, rows_per_batch, n_batch),
            _mod_spec(d, 4, tm, rows_per_batch, n_batch),
        ],
        out_specs=[pl.BlockSpec((tm, d), row), pl.BlockSpec((tm, d), row)],
        out_shape=[jax.ShapeDtypeStruct((n_rows, d), F32), jax.ShapeDtypeStruct((n_rows, d), BF16)],
        scratch_shapes=[pltpu.VMEM((tm, d), BF16)],
        compiler_params=_params("parallel"),
        name="out_proj",
    )(o_lat, o_ctx, proj, norm_gain.reshape(1, d), w, x, mod, gain2.reshape(1, d), mod, mod)


def _gateup_kernel(h_ref, wg_ref, wu_ref, p_ref, wgb_ref, wub_ref):
    _cast_weight_tile(wg_ref, wgb_ref)
    _cast_weight_tile(wu_ref, wub_ref)
    h = h_ref[...]
    for t in range(FFN_HID_TILE // MXU_COLS):
        cols = slice(t * MXU_COLS, (t + 1) * MXU_COLS)
        a = _dot(h, wgb_ref[:, cols])
        b = _dot(h, wub_ref[:, cols])
        p_ref[:, cols] = (a * _sigmoid(a) * b).astype(BF16)


def _down_kernel(p_ref, wd_ref, x_ref, gt_ref, ng_ref, sh_ref, sc_ref, *out_refs, final):
    xo_ref, h_ref = (out_refs[1], out_refs[0]) if final else out_refs
    p = p_ref[...]
    for t in range(wd_ref.shape[1] // MXU_COLS):
        cols = slice(t * MXU_COLS, (t + 1) * MXU_COLS)
        xo_ref[:, cols] = x_ref[:, cols] + gt_ref[:, cols] * _dot(p, wd_ref[:, cols])

    def norm(rows):
        x = xo_ref[rows, :]
        if final:
            h_ref[rows, :] = x * lax.rsqrt(jnp.mean(x * x, axis=-1, keepdims=True) + EPS) * ng_ref[...]
        else:
            h_ref[rows, :] = _norm_mod(x, ng_ref[...], sh_ref[...], sc_ref[...]).astype(BF16)
    _row_loop(ROW_TILE, norm, NORM_STEP)


def _ffn(h2, w_gate_up, w_down, layer, x, mod, next_gain, next_mod, rows_per_batch, n_batch, n_rows):
    d = x.shape[1]
    hidden = w_down.shape[1]
    tm = BIG_ROW_TILE
    n_hid = hidden // FFN_HID_TILE
    p = pl.pallas_call(
        _gateup_kernel,
        grid=(n_hid, n_rows // tm),
        in_specs=[
            pl.BlockSpec((tm, d), lambda j, i: (i, 0)),
            pl.BlockSpec((None, d, FFN_HID_TILE), lambda j, i: (layer, 0, j)),
            pl.BlockSpec((None, d, FFN_HID_TILE), lambda j, i: (layer, 0, j + n_hid)),
        ],
        out_specs=pl.BlockSpec((tm, FFN_HID_TILE), lambda j, i: (i, j)),
        out_shape=jax.ShapeDtypeStruct((n_rows, hidden), BF16),
        scratch_shapes=[pltpu.VMEM((d, FFN_HID_TILE), BF16), pltpu.VMEM((d, FFN_HID_TILE), BF16)],
        compiler_params=_params("arbitrary", "arbitrary"),
        name="ffn_gate_up",
    )(h2, w_gate_up, w_gate_up)
    tm = ROW_TILE
    final = next_mod is None
    row = lambda i: (i, 0)
    const = lambda i: (0, 0)
    norm_mod = mod if final else next_mod
    x_out = jax.ShapeDtypeStruct((n_rows, d), F32)
    outs = pl.pallas_call(
        functools.partial(_down_kernel, final=final),
        grid=(n_rows // tm,),
        in_specs=[
            pl.BlockSpec((tm, hidden), row),
            pl.BlockSpec((None, hidden, d), lambda i: (layer, 0, 0)),
            pl.BlockSpec((tm, d), row),
            _mod_spec(d, 5, tm, rows_per_batch, n_batch),
            pl.BlockSpec((1, d), const),
            _mod_spec(d, 0, tm, rows_per_batch, n_batch),
            _mod_spec(d, 1, tm, rows_per_batch, n_batch),
        ],
        out_specs=[pl.BlockSpec((tm, d), row)] * (1 if final else 2),
        out_shape=[x_out] if final else [x_out, jax.ShapeDtypeStruct((n_rows, d), BF16)],
        scratch_shapes=[pltpu.VMEM((tm, d), F32)] if final else [],
        compiler_params=_params("parallel"),
        name="ffn_down",
    )(p, w_down, x, mod, next_gain.reshape(1, d), norm_mod, norm_mod)
    return (outs[0], None) if final else outs


def _ret_kernel(lg_ref, q_ref, k_ref, v_ref, qc_ref, kc_ref, vc_ref, cos_ref, sin_ref,
                o_ref, oc_ref, dec_ref, kr_ref, sbs_ref, sf_ref, sb_ref, *, k_scale):
    c = RET_CHUNK
    dk = q_ref.shape[1]
    half = dk // 2
    n_chunks = q_ref.shape[0] // c
    head = pl.program_id(1)
    lgf = lg_ref[0, head]
    lgb = lg_ref[1, head]

    n_i = lax.broadcasted_iota(jnp.int32, (c, c), 0).astype(F32)
    m_i = lax.broadcasted_iota(jnp.int32, (c, c), 1).astype(F32)
    diff = n_i - m_i
    dec_ref[0] = (jnp.where(diff >= 0, jnp.exp(lgf * jnp.maximum(diff, 0.0)), 0.0)
                  + jnp.where(diff <= 0, jnp.exp(lgb * jnp.maximum(-diff, 0.0)), 0.0))
    t_i = lax.broadcasted_iota(jnp.int32, (c, dk), 0).astype(F32)
    dec_ref[1] = jnp.exp(lgf * (t_i + 1.0))
    dec_ref[2] = jnp.exp(lgb * (c - t_i))
    dec_ref[3] = jnp.exp(lgf * (c - 1.0 - t_i))
    dec_ref[4] = jnp.exp(lgb * t_i)
    cf = jnp.exp(lgf * c)
    cb = jnp.exp(lgb * c)

    def head_norm(o):
        return o * lax.rsqrt(jnp.mean(o * o, axis=-1, keepdims=True) + EPS)

    def rope(t, rows):
        cos = cos_ref[rows, :]
        sin = sin_ref[rows, :]
        t1 = t[:, :half]
        t2 = t[:, half:]
        return jnp.concatenate([t1 * cos - t2 * sin, t1 * sin + t2 * cos], axis=-1)

    qc = qc_ref[...]
    kc = (kc_ref[...].astype(F32) * k_scale)
    vc = vc_ref[...]
    sc = _dot_nt(qc, kc.astype(BF16)) * dec_ref[0]
    oc_ref[...] = head_norm(_dot(sc.astype(BF16), vc)).astype(oc_ref.dtype)
    sf_ref[...] = _dot_tn((kc * dec_ref[3]).astype(BF16), vc)
    sb_ref[...] = _dot_tn((kc * dec_ref[4]).astype(BF16), vc)

    unroll = math.gcd(RET_UNROLL, n_chunks)
    n_steps = n_chunks // unroll

    def chunk_rows(ci):
        return pl.ds(pl.multiple_of(ci * c, c), c)

    def bwd_local(i, carry):
        cis = [i * unroll + u for u in range(unroll)]
        rows = [chunk_rows(ci) for ci in cis]
        krs = [rope(k_ref[r, :].astype(F32), r) * k_scale for r in rows]
        kvs = [_dot_tn((kr * dec_ref[4]).astype(BF16), v_ref[r, :]) for kr, r in zip(krs, rows)]
        for ci, r, kr, kv in zip(cis, rows, krs, kvs):
            kr_ref[r, :] = kr.astype(BF16)
            sbs_ref[ci] = kv
        return carry
    lax.fori_loop(0, n_steps, bwd_local, 0)

    def bwd_state(i, carry):
        ci = n_chunks - 1 - i
        s = sb_ref[...]
        kv = sbs_ref[ci]
        sbs_ref[ci] = s
        sb_ref[...] = s * cb + kv
        return carry
    lax.fori_loop(0, n_chunks, bwd_state, 0)

    def fwd(i, carry):
        cis = [i * unroll + u for u in range(unroll)]
        rows = [chunk_rows(ci) for ci in cis]
        qrs = [rope(q_ref[r, :].astype(F32), r).astype(BF16) for r in rows]
        krs = [kr_ref[r, :] for r in rows]
        vs = [v_ref[r, :] for r in rows]
        scs = [_dot_nt(qr, kr) for qr, kr in zip(qrs, krs)]
        kvs = [_dot_tn((kr.astype(F32) * dec_ref[3]).astype(BF16), v) for kr, v in zip(krs, vs)]
        sfs = [sf_ref[...]]
        for kv in kvs:
            sfs.append(sfs[-1] * cf + kv)
        sf_ref[...] = sfs[-1]
        intra = [_dot((sc * dec_ref[0]).astype(BF16), v) for sc, v in zip(scs, vs)]
        inter_f = [_dot(qr, s.astype(BF16)) for qr, s in zip(qrs, sfs)]
        inter_b = [_dot(qr, sbs_ref[ci].astype(BF16)) for qr, ci in zip(qrs, cis)]
        for r, o1, o2, o3 in zip(rows, intra, inter_f, inter_b):
            o = o1 + dec_ref[1] * o2 + dec_ref[2] * o3
            o_ref[r, :] = head_norm(o).astype(o_ref.dtype)
        return carry
    lax.fori_loop(0, n_steps, fwd, 0)


def _retention(proj, log_gamma, cos, sin, n_batch, seq, ctx_len):
    heads = RET_HEADS
    dk = proj.shape[1] // (4 * heads)
    d = heads * dk
    assert dk == RET_CHUNK and ctx_len == RET_CHUNK and seq % RET_CHUNK == 0
    ctx_row0 = (n_batch * seq) // ctx_len
    lat = lambda part: pl.BlockSpec((seq, dk), lambda b, h: (b, part * heads + h))
    ctx = lambda part: pl.BlockSpec((ctx_len, dk), lambda b, h: (ctx_row0 + b, part * heads + h))
    tab = pl.BlockSpec((seq, dk // 2), lambda b, h: (0, 0))
    n_chunks = seq // RET_CHUNK
    return pl.pallas_call(
        functools.partial(_ret_kernel, k_scale=dk ** -0.5),
        grid=(n_batch, heads),
        in_specs=[pl.BlockSpec(memory_space=pltpu.SMEM),
                  lat(0), lat(1), lat(2), ctx(0), ctx(1), ctx(2), tab, tab],
        out_specs=[pl.BlockSpec((seq, dk), lambda b, h: (b, h)),
                   pl.BlockSpec((ctx_len, dk), lambda b, h: (b, h))],
        out_shape=[jax.ShapeDtypeStruct((n_batch * seq, d), BF16),
                   jax.ShapeDtypeStruct((n_batch * ctx_len, d), BF16)],
        scratch_shapes=[
            pltpu.VMEM((5, RET_CHUNK, RET_CHUNK), F32),
            pltpu.VMEM((seq, dk), BF16),
            pltpu.VMEM((n_chunks, dk, dk), F32),
            pltpu.VMEM((dk, dk), F32),
            pltpu.VMEM((dk, dk), F32),
        ],
        compiler_params=_params("parallel", "arbitrary"),
        name="retention",
    )(log_gamma, proj, proj, proj, proj, proj, proj, cos, sin)


def _hg_gates(z, lb):
    f = lb + (1.0 - lb) * _sigmoid(z)
    return f, jnp.log2(f), 1.0 - f


def _hg_cumsum(tri_ref, g):
    dk = g.shape[1]
    hi = g.astype(BF16)
    lo = (g - hi.astype(F32)).astype(BF16)
    r = _dot(tri_ref[...], jnp.concatenate([hi, lo], axis=1))
    return r[:, :dk] + r[:, dk:]


def _tiles(x):
    return [x[j * V7X_SUBLANES:(j + 1) * V7X_SUBLANES] for j in range(x.shape[0] // V7X_SUBLANES)]


def _row_of_tile(tile, r):
    return jnp.broadcast_to(tile[r:r + 1, :], tile.shape)


def _hg_levels(up, lo, cv, cin):
    upt, lot, cvt, cint = _tiles(up), _tiles(lo), _tiles(cv), _tiles(cin)
    nt = len(cvt)
    zero = jnp.zeros_like(cvt[0])
    lastt = [_row_of_tile(t, V7X_SUBLANES - 1) for t in cint]
    out = []
    bt = nt
    while bt >= 2:
        ht = bt // 2
        hi_rows, lo_rows = [], []
        for j in range(nt):
            b0 = (j // bt) * bt
            ref = lastt[b0 + ht - 1]
            if j - b0 >= ht:
                hi_rows.append(jnp.exp2(cvt[j] - ref) * upt[j])
                lo_rows.append(zero)
            else:
                hi_rows.append(zero)
                lo_rows.append(jnp.exp2(ref - cvt[j]) * lot[j])
        out.append((jnp.concatenate(hi_rows, axis=0), jnp.concatenate(lo_rows, axis=0)))
        bt = ht
    sub = lax.broadcasted_iota(jnp.int32, zero.shape, 0)
    for size in (8, 4):
        upper = (sub & (size // 2)) != 0
        sign = jnp.where(upper, 1.0, -1.0)
        hi_rows, lo_rows = [], []
        for j in range(nt):
            if size == 8:
                ref = _row_of_tile(cint[j], 3)
            else:
                ref = jnp.where(sub < 4, _row_of_tile(cint[j], 1), _row_of_tile(cint[j], 5))
            z = jnp.exp2((cvt[j] - ref) * sign) * jnp.where(upper, upt[j], lot[j])
            hi_rows.append(jnp.where(upper, z, 0.0))
            lo_rows.append(jnp.where(upper, 0.0, z))
        out.append((jnp.concatenate(hi_rows, axis=0), jnp.concatenate(lo_rows, axis=0)))
    return out


def _hg_bwd_local(rows, ci, g_ref, key_ref, v_ref, tri_ref, cumb_ref, kv_ref):
    g = g_ref[rows, :]
    cum = _hg_cumsum(tri_ref, g)
    cumb_ref[rows, :] = cum
    k_dec = key_ref[rows, :].astype(F32) * jnp.exp2(cum - g)
    kv_ref[ci] = _dot_tn(v_ref[rows, :], k_dec.astype(BF16))


def _hg_step_out(rows_list, cis, q_ref, gf_ref, kf_ref, gb_ref, kb_ref, v_ref, tri_ref, cumb_ref,
                 sbs_ref, sf_ref, out_ref, pair_xor):
    n = len(cis)
    q = [q_ref[r, :].astype(F32) for r in rows_list]
    kf = [kf_ref[r, :].astype(F32) for r in rows_list]
    kb = [kb_ref[r, :].astype(F32) for r in rows_list]
    v = [v_ref[r, :] for r in rows_list]
    gf = [gf_ref[r, :] for r in rows_list]
    gb = [gb_ref[r, :] for r in rows_list]
    c = q[0].shape[0]

    cumf = [_hg_cumsum(tri_ref, g) for g in gf]
    cumb = [cumb_ref[r, :] for r in rows_list]
    cumxb = [cb - g for cb, g in zip(cumb, gb)]
    totf = [cf[c - 1:, :] for cf in cumf]
    totb = [cb[c - 1:, :] for cb in cumb]

    kvs = [_dot_tn(v[u], (kf[u] * jnp.exp2(totf[u] - cumf[u])).astype(BF16)) for u in range(n)]
    sf = [sf_ref[...]]
    for u in range(n):
        sf.append(sf[u] * jnp.exp2(totf[u]) + kvs[u])
    sf_ref[...] = sf[n]

    odd = (lax.broadcasted_iota(jnp.int32, q[0].shape, 0) & 1) != 0
    scores = []
    for u in range(n):
        lev_f = _hg_levels(q[u], kf[u], cumf[u], cumf[u])
        lev_b = _hg_levels(kb[u], q[u], cumxb[u], cumb[u])
        lev_f.append((jnp.where(odd, q[u] * jnp.exp2(gf[u]), 0.0), jnp.where(odd, 0.0, kf[u])))
        lev_b.append((jnp.where(odd, kb[u], 0.0), jnp.where(odd, 0.0, q[u] * jnp.exp2(gb[u]))))
        a = None
        size = c
        for (xf, yf), (yb, xb) in zip(lev_f, lev_b):
            x = jnp.concatenate([xf.astype(BF16), xb.astype(BF16)], axis=1)
            y = jnp.concatenate([yf.astype(BF16), yb.astype(BF16)], axis=1)
            p = _dot_nt(x, y)
            a = p if a is None else jnp.where(pair_xor < size, p, a)
            size //= 2
        scores.append(a.astype(BF16))

    inter = []
    for u in range(n):
        q_dec = jnp.concatenate([(q[u] * jnp.exp2(cumf[u])).astype(BF16),
                                 (q[u] * jnp.exp2(totb[u] - cumxb[u])).astype(BF16)], axis=1)
        states = jnp.concatenate([sf[u].astype(BF16), sbs_ref[cis[u]].astype(BF16)], axis=1)
        inter.append(_dot_nt(q_dec, states))

    for u in range(n):
        o = _dot(scores[u], v[u]) + inter[u]
        o += jnp.sum(q[u] * (kf[u] + kb[u]), axis=-1, keepdims=True) * v[u].astype(F32)
        out_ref[rows_list[u], :] = o.astype(out_ref.dtype)


def _hg_kernel(q_ref, gf_ref, kf_ref, gb_ref, kb_ref, v_ref,
               qc_ref, gfc_ref, kfc_ref, gbc_ref, kbc_ref, vc_ref,
               o_ref, oc_ref, tri_ref, cumb_ref, sbs_ref, sf_ref, sb_ref):
    c = HG_CHUNK
    n_i = lax.broadcasted_iota(jnp.int32, (c, c), 0)
    m_i = lax.broadcasted_iota(jnp.int32, (c, c), 1)
    pair_xor = n_i ^ m_i
    tri_ref[...] = jnp.where(m_i <= n_i, 1.0, 0.0).astype(BF16)

    def run(qr_ref, gfr_ref, kfr_ref, gbr_ref, kbr_ref, vr_ref, out_ref):
        n_chunks = qr_ref.shape[0] // c
        unroll = math.gcd(HG_UNROLL, n_chunks)
        n_steps = n_chunks // unroll

        def chunk_rows(ci):
            return pl.ds(pl.multiple_of(ci * c, c), c)

        local_unroll = math.gcd(HG_LOCAL_UNROLL, n_chunks)

        def bwd_local(i, carry):
            cis = [i * local_unroll + u for u in range(local_unroll)]
            gs = [gbr_ref[chunk_rows(ci), :] for ci in cis]
            cums = [_hg_cumsum(tri_ref, g) for g in gs]
            k_decs = [(kbr_ref[chunk_rows(ci), :].astype(F32) * jnp.exp2(cum - g)).astype(BF16)
                      for ci, g, cum in zip(cis, gs, cums)]
            kvs = [_dot_tn(vr_ref[chunk_rows(ci), :], k_dec) for ci, k_dec in zip(cis, k_decs)]
            for ci, cum, kv in zip(cis, cums, kvs):
                cumb_ref[chunk_rows(ci), :] = cum
                sbs_ref[ci] = kv
            return carry
        lax.fori_loop(0, n_chunks // local_unroll, bwd_local, 0)

        def bwd_state(i, carry):
            s = sb_ref[...]
            for u in range(unroll):
                ci = n_chunks - 1 - (i * unroll + u)
                kv = sbs_ref[ci]
                sbs_ref[ci] = s
                last_tile = cumb_ref[pl.ds(pl.multiple_of(ci * c + c - V7X_SUBLANES, V7X_SUBLANES),
                                           V7X_SUBLANES), :]
                s = s * jnp.exp2(last_tile[V7X_SUBLANES - 1:, :]) + kv
            sb_ref[...] = s
            return carry
        lax.fori_loop(0, n_steps, bwd_state, 0)

        def fwd(i, carry):
            cis = [i * unroll + u for u in range(unroll)]
            _hg_step_out([chunk_rows(ci) for ci in cis], cis, qr_ref, gfr_ref, kfr_ref, gbr_ref, kbr_ref,
                         vr_ref, tri_ref, cumb_ref, sbs_ref, sf_ref, out_ref, pair_xor)
            return carry
        lax.fori_loop(0, n_steps, fwd, 0)

    sf_ref[...] = jnp.zeros_like(sf_ref)
    sb_ref[...] = jnp.zeros_like(sb_ref)
    run(qc_ref, gfc_ref, kfc_ref, gbc_ref, kbc_ref, vc_ref, oc_ref)
    run(q_ref, gf_ref, kf_ref, gb_ref, kb_ref, v_ref, o_ref)


def _hgrn(qiv, log_f, key, n_batch, seq, ctx_len):
    dk = HG_EXPAND
    heads = log_f.shape[1] // (2 * dk)
    d = heads * dk
    assert seq % HG_CHUNK == 0 and ctx_len % HG_CHUNK == 0 and ctx_len <= seq
    ctx_row0 = (n_batch * seq) // ctx_len
    lat = lambda part: pl.BlockSpec((seq, dk), lambda b, h: (b, part * heads + h))
    ctx = lambda part: pl.BlockSpec((ctx_len, dk), lambda b, h: (ctx_row0 + b, part * heads + h))
    return pl.pallas_call(
        _hg_kernel,
        grid=(n_batch, heads),
        in_specs=[lat(0), lat(0), lat(0), lat(1), lat(1), lat(1),
                  ctx(0), ctx(0), ctx(0), ctx(1), ctx(1), ctx(1)],
        out_specs=[pl.BlockSpec((seq, dk), lambda b, h: (b, h)),
                   pl.BlockSpec((ctx_len, dk), lambda b, h: (b, h))],
        out_shape=[jax.ShapeDtypeStruct((n_batch * seq, d), BF16),
                   jax.ShapeDtypeStruct((n_batch * ctx_len, d), BF16)],
        scratch_shapes=[
            pltpu.VMEM((HG_CHUNK, HG_CHUNK), BF16),
            pltpu.VMEM((seq, dk), F32),
            pltpu.VMEM((seq // HG_CHUNK, dk, dk), F32),
            pltpu.VMEM((dk, dk), F32),
            pltpu.VMEM((dk, dk), F32),
        ],
        compiler_params=_params("parallel", "arbitrary"),
        name="hgrn",
    )(qiv, log_f, key, log_f, key, qiv, qiv, log_f, key, log_f, key, qiv)


def _rope_tables(seq, head_dim):
    quarter = head_dim // 4
    rows = jnp.repeat(jnp.arange(seq // GRID_W, dtype=F32), GRID_W)
    cols = jnp.tile(jnp.arange(GRID_W, dtype=F32), seq // GRID_W)
    inv_freq = ROPE_BASE ** (-jnp.arange(quarter, dtype=F32) / quarter)
    ang = jnp.concatenate([rows[:, None] * inv_freq, cols[:, None] * inv_freq], axis=-1)
    return jnp.cos(ang), jnp.sin(ang)


def kernel(x, c, ctx, c_ctx, ada_w, ada_b, norm1_g, norm2_g, ret_w_in, ret_w_out, ret_decay_logits,
           hg_w_in, hg_w_out, hg_norm_g, hg_lower_bounds, ffn_w_gate_up, ffn_w_down, final_norm_g):
    n_batch, seq, d = x.shape
    ctx_len = ctx.shape[1]
    depth = ada_w.shape[0]
    n_lat = n_batch * seq
    n_all = n_lat + n_batch * ctx_len
    assert seq % BIG_ROW_TILE == 0 and (n_batch * ctx_len) % BIG_ROW_TILE == 0

    cond_rows = -(-(n_batch + 1) // V7X_SUBLANES) * V7X_SUBLANES
    cond = jnp.zeros((cond_rows, d), F32).at[:n_batch].set(c).at[n_batch].set(c_ctx)
    mods = _ada_mod(cond, ada_w, ada_b).reshape(depth, cond_rows, 1, 6 * d)

    lb_p = jax.nn.softmax(hg_lower_bounds.astype(F32), axis=0)
    lower_bounds = jnp.cumsum(lb_p, axis=0) - lb_p[0]
    log_gamma = jax.nn.log_sigmoid(ret_decay_logits.astype(F32))
    cos, sin = _rope_tables(seq, d // RET_HEADS)

    ret_in, ret_out = ret_w_in, ret_w_out.astype(BF16)
    hg_in, hg_out = hg_w_in, hg_w_out.astype(BF16)
    ffn_gate_up, ffn_down = ffn_w_gate_up, ffn_w_down.astype(BF16)

    xs, h1 = _prenorm(x.reshape(n_lat, d), ctx.reshape(n_batch * ctx_len, d), norm1_g[0], mods[0],
                      seq, n_batch)
    for layer in range(depth):
        last = layer == depth - 1
        j = layer // N_MIXERS
        mod = mods[layer]
        n_rows = n_lat if last else n_all
        retention = layer % N_MIXERS == 0
        w_in, w_out = (ret_in, ret_out) if retention else (hg_in, hg_out)
        tiles_per_part = d // IN_COL_TILE
        if retention:
            proj = _proj(h1, w_in, j, (0, w_in.shape[2], 0),
                         [None] * (3 * tiles_per_part) + ["silu"] * tiles_per_part)
            o_lat, o_ctx = _retention(proj, log_gamma[j], cos, sin, n_batch, seq, ctx_len)
            norm_gain = jnp.ones((d,), F32)
        else:
            proj = _proj(h1, w_in, j, (0, tiles_per_part, 2 * tiles_per_part),
                         ["silu"] * tiles_per_part + [None] * tiles_per_part + ["sigmoid"] * tiles_per_part)
            log_f, key = _proj_gates(h1, w_in, j, lower_bounds[j], tiles_per_part, 2 * tiles_per_part)
            o_lat, o_ctx = _hgrn(proj, log_f, key, n_batch, seq, ctx_len)
            norm_gain = hg_norm_g[j]
        gate_block = proj.shape[1] // d - 1
        xs, h2 = _out_proj(o_lat, o_ctx, proj, gate_block, norm_gain, w_out, j, xs, mod, norm2_g[layer],
                           seq, n_batch, n_rows, not retention)
        next_gain, next_mod = (final_norm_g, None) if last else (norm1_g[layer + 1], mods[layer + 1])
        xs, h1 = _ffn(h2, ffn_gate_up, ffn_down, layer, xs, mod, next_gain, next_mod, seq, n_batch, n_rows)
    return xs.reshape(n_batch, seq, d)
```

```python
import functools
import math

import jax
import jax.numpy as jnp
from jax import lax
from jax.experimental import pallas as pl
from jax.experimental.pallas import tpu as pltpu

F32 = jnp.float32
BF16 = jnp.bfloat16

EPS = 1e-6
LOG2E = 1.4426950408889634
ROPE_BASE = 10000.0
GRID_W = 64
N_MIXERS = 2
RET_HEADS = 8
HG_EXPAND = 128

V7X_LANES = 128
V7X_SUBLANES = 8
V7X_VMEM_BYTES = 64 * 1024 * 1024

ROW_TILE = 512
BIG_ROW_TILE = 1024
IN_COL_TILE = 1024
FFN_HID_TILE = 512
MXU_COLS = 256
ADA_COL_TILE = 1024
RET_CHUNK = 256
RET_UNROLL = 8
HG_CHUNK = 128
HG_UNROLL = 16
HG_LOCAL_UNROLL = 16
ROW_STEP = 64
NORM_STEP = 128
CAST_STEP = 256
VMEM_LIMIT = 56 * 1024 * 1024


def _params(*semantics):
    return pltpu.CompilerParams(dimension_semantics=semantics, vmem_limit_bytes=VMEM_LIMIT)


def _sigmoid(x):
    return 1.0 / (1.0 + jnp.exp(-x))


def _gate_sigmoid(x):
    return 0.5 * jnp.tanh(0.5 * x) + 0.5


def _dot(a, b):
    return jnp.dot(a, b, preferred_element_type=F32)


def _dot_nt(a, b):
    return lax.dot_general(a, b, (((1,), (1,)), ((), ())), preferred_element_type=F32)


def _dot_tn(a, b):
    return lax.dot_general(a, b, (((0,), (0,)), ((), ())), preferred_element_type=F32)


def _ada_kernel(c_ref, w_ref, b_ref, o_ref):
    c = c_ref[...]
    a = (c * _sigmoid(c)).astype(BF16)
    o_ref[...] = _dot(a, w_ref[...].astype(BF16)) + b_ref[...]


def _ada_mod(cond, ada_w, ada_b):
    depth, d, n = ada_w.shape
    rows = cond.shape[0]
    return pl.pallas_call(
        _ada_kernel,
        grid=(depth, n // ADA_COL_TILE),
        in_specs=[
            pl.BlockSpec((rows, d), lambda l, j: (0, 0)),
            pl.BlockSpec((None, d, ADA_COL_TILE), lambda l, j: (l, 0, j)),
            pl.BlockSpec((None, 1, ADA_COL_TILE), lambda l, j: (l, 0, j)),
        ],
        out_specs=pl.BlockSpec((None, rows, ADA_COL_TILE), lambda l, j: (l, 0, j)),
        out_shape=jax.ShapeDtypeStruct((depth, rows, n), F32),
        compiler_params=_params("parallel", "parallel"),
        name="ada_mod",
    )(cond, ada_w, ada_b.reshape(depth, 1, n))


def _norm_mod(x, gain, shift, scale):
    return x * lax.rsqrt(jnp.mean(x * x, axis=-1, keepdims=True) + EPS) * (gain * (1.0 + scale)) + shift


def _row_loop(rows, body, step_rows=ROW_STEP):
    def step(i, carry):
        body(pl.ds(pl.multiple_of(i * step_rows, step_rows), step_rows))
        return carry
    lax.fori_loop(0, rows // step_rows, step, 0)


def _mod_spec(width, slot, row_tile, rows_per_batch, n_batch, col_axis=None):
    def index(*ids):
        col = slot if col_axis is None else slot + ids[col_axis]
        return (jnp.minimum((ids[0] * row_tile) // rows_per_batch, n_batch), 0, col)
    return pl.BlockSpec((None, 1, width), index)


def _prenorm_kernel(xl_ref, xc_ref, g_ref, sh_ref, sc_ref, xs_ref, h_ref, *, n_lat_tiles):
    def copy_norm(x_ref):
        def body(rows):
            x = x_ref[rows, :]
            xs_ref[rows, :] = x
            h_ref[rows, :] = _norm_mod(x, g_ref[...], sh_ref[...], sc_ref[...]).astype(BF16)
        _row_loop(ROW_TILE, body, NORM_STEP)

    tile = pl.program_id(0)
    pl.when(tile < n_lat_tiles)(functools.partial(copy_norm, xl_ref))
    pl.when(tile >= n_lat_tiles)(functools.partial(copy_norm, xc_ref))


def _prenorm(x_lat, x_ctx, gain, mod, rows_per_batch, n_batch):
    d = x_lat.shape[1]
    tm = ROW_TILE
    n_lat_tiles = x_lat.shape[0] // tm
    r = x_lat.shape[0] + x_ctx.shape[0]
    row = lambda i: (i, 0)
    return pl.pallas_call(
        functools.partial(_prenorm_kernel, n_lat_tiles=n_lat_tiles),
        grid=(r // tm,),
        in_specs=[
            pl.BlockSpec((tm, d), lambda i: (jnp.minimum(i, n_lat_tiles - 1), 0)),
            pl.BlockSpec((tm, d), lambda i: (jnp.maximum(i - n_lat_tiles, 0), 0)),
            pl.BlockSpec((1, d), lambda i: (0, 0)),
            _mod_spec(d, 0, tm, rows_per_batch, n_batch),
            _mod_spec(d, 1, tm, rows_per_batch, n_batch),
        ],
        out_specs=[pl.BlockSpec((tm, d), row), pl.BlockSpec((tm, d), row)],
        out_shape=[jax.ShapeDtypeStruct((r, d), F32), jax.ShapeDtypeStruct((r, d), BF16)],
        compiler_params=_params("parallel"),
        name="prenorm",
    )(x_lat, x_ctx, gain.reshape(1, d), mod, mod)


def _col_tiles():
    return [slice(t * MXU_COLS, (t + 1) * MXU_COLS) for t in range(IN_COL_TILE // MXU_COLS)]


def _cast_weight_tile(w_ref, wb_ref):
    @pl.when(pl.program_id(1) == 0)
    def _():
        def body(rows):
            wb_ref[rows, :] = w_ref[rows, :].astype(BF16)
        _row_loop(w_ref.shape[0], body, CAST_STEP)


def _proj_kernel(h_ref, w_ref, o_ref, wb_ref, *, acts):
    _cast_weight_tile(w_ref, wb_ref)
    h = h_ref[...]

    def run(act):
        for cols in _col_tiles():
            a = _dot(h, wb_ref[:, cols])
            if act == "silu":
                a = a * _gate_sigmoid(a)
            elif act == "sigmoid":
                a = _gate_sigmoid(a)
            o_ref[:, cols] = a.astype(o_ref.dtype)

    j = pl.program_id(0)
    for act in sorted(set(acts), key=str):
        tiles = [t for t, a in enumerate(acts) if a == act]
        lo, hi = tiles[0], tiles[-1] + 1
        assert tiles == list(range(lo, hi))
        pl.when(jnp.logical_and(j >= lo, j < hi))(functools.partial(run, act))


def _proj(h, w, layer, col_tiles, acts):
    r, d = h.shape
    tm, tn = BIG_ROW_TILE, IN_COL_TILE
    first, skip_from, skip = col_tiles
    n_tiles = w.shape[2] // tn - first - skip
    assert len(acts) == n_tiles
    wcol = lambda j, i: (layer, 0, first + j + jnp.where(j >= skip_from, skip, 0))
    return pl.pallas_call(
        functools.partial(_proj_kernel, acts=tuple(acts)),
        grid=(n_tiles, r // tm),
        in_specs=[pl.BlockSpec((tm, d), lambda j, i: (i, 0)), pl.BlockSpec((None, d, tn), wcol)],
        out_specs=pl.BlockSpec((tm, tn), lambda j, i: (i, j)),
        out_shape=jax.ShapeDtypeStruct((r, n_tiles * tn), BF16),
        scratch_shapes=[pltpu.VMEM((d, tn), BF16)],
        compiler_params=_params("arbitrary", "arbitrary"),
        name="in_proj",
    )(h, w)


def _proj_gates_kernel(h_ref, w_ref, lb_ref, g_ref, key_ref, wb_ref):
    _cast_weight_tile(w_ref, wb_ref)
    h = h_ref[...]
    for cols in _col_tiles():
        _, g, key = _hg_gates(_dot(h, wb_ref[:, cols]), lb_ref[:, cols])
        g_ref[:, cols] = g
        key_ref[:, cols] = key.astype(key_ref.dtype)


def _proj_gates(h, w, layer, lower_bound, first_tile, n_tiles):
    r, d = h.shape
    tm, tn = BIG_ROW_TILE, IN_COL_TILE
    lb_tiles = lower_bound.shape[0] // tn
    out = pl.BlockSpec((tm, tn), lambda j, i: (i, j))
    return pl.pallas_call(
        _proj_gates_kernel,
        grid=(n_tiles, r // tm),
        in_specs=[pl.BlockSpec((tm, d), lambda j, i: (i, 0)),
                  pl.BlockSpec((None, d, tn), lambda j, i: (layer, 0, first_tile + j)),
                  pl.BlockSpec((1, tn), lambda j, i: (0, j % lb_tiles))],
        out_specs=[out, out],
        out_shape=[jax.ShapeDtypeStruct((r, n_tiles * tn), F32),
                   jax.ShapeDtypeStruct((r, n_tiles * tn), BF16)],
        scratch_shapes=[pltpu.VMEM((d, tn), BF16)],
        compiler_params=_params("arbitrary", "arbitrary"),
        name="in_proj_gates",
    )(h, w, lower_bound.reshape(1, -1))


def _outproj_kernel(ol_ref, oc_ref, g_ref, ng_ref, w_ref, x_ref, gt_ref, n2_ref, sh_ref, sc_ref,
                    xo_ref, h_ref, y_ref, *, hgrn, n_lat_tiles):
    def gate_from(o_ref):
        def gate(rows):
            o = o_ref[rows, :].astype(F32)
            if hgrn:
                o = o * lax.rsqrt(jnp.mean(o * o, axis=-1, keepdims=True) + EPS) * ng_ref[...]
            y_ref[rows, :] = (o * g_ref[rows, :].astype(F32)).astype(BF16)
        _row_loop(ROW_TILE, gate)

    tile = pl.program_id(0)
    pl.when(tile < n_lat_tiles)(functools.partial(gate_from, ol_ref))
    pl.when(tile >= n_lat_tiles)(functools.partial(gate_from, oc_ref))

    y = y_ref[...]
    for t in range(w_ref.shape[1] // MXU_COLS):
        cols = slice(t * MXU_COLS, (t + 1) * MXU_COLS)
        xo_ref[:, cols] = x_ref[:, cols] + gt_ref[:, cols] * _dot(y, w_ref[:, cols])

    def norm(rows):
        h_ref[rows, :] = _norm_mod(xo_ref[rows, :], n2_ref[...], sh_ref[...], sc_ref[...]).astype(BF16)
    _row_loop(ROW_TILE, norm, NORM_STEP)


def _out_proj(o_lat, o_ctx, proj, gate_block, norm_gain, w, layer, x, mod, gain2, rows_per_batch, n_batch,
              n_rows, hgrn):
    d = x.shape[1]
    tm = ROW_TILE
    n_lat_tiles = o_lat.shape[0] // tm
    row = lambda i: (i, 0)
    const = lambda i: (0, 0)
    return pl.pallas_call(
        functools.partial(_outproj_kernel, hgrn=hgrn, n_lat_tiles=n_lat_tiles),
        grid=(n_rows // tm,),
        in_specs=[
            pl.BlockSpec((tm, d), lambda i: (jnp.minimum(i, n_lat_tiles - 1), 0)),
            pl.BlockSpec((tm, d), lambda i: (jnp.maximum(i - n_lat_tiles, 0), 0)),
            pl.BlockSpec((tm, d), lambda i: (i, gate_block)),
            pl.BlockSpec((1, d), const),
            pl.BlockSpec((None, d, d), lambda i: (layer, 0, 0)),
            pl.BlockSpec((tm, d), row),
            _mod_spec(d, 2, tm, rows_per_batch, n_batch),
            pl.BlockSpec((1, d), const),
            _mod_spec(d, 3, tm, rows_per_batch, n_batch),
            _mod_spec(d, 4, tm, rows_per_batch, n_batch),
        ],
        out_specs=[pl.BlockSpec((tm, d), row), pl.BlockSpec((tm, d), row)],
        out_shape=[jax.ShapeDtypeStruct((n_rows, d), F32), jax.ShapeDtypeStruct((n_rows, d), BF16)],
        scratch_shapes=[pltpu.VMEM((tm, d), BF16)],
        compiler_params=_params("parallel"),
        name="out_proj",
    )(o_lat, o_ctx, proj, norm_gain.reshape(1, d), w, x, mod, gain2.reshape(1, d), mod, mod)


def _gateup_kernel(h_ref, wg_ref, wu_ref, p_ref, wgb_ref, wub_ref):
    _cast_weight_tile(wg_ref, wgb_ref)
    _cast_weight_tile(wu_ref, wub_ref)
    h = h_ref[...]
    for t in range(FFN_HID_TILE // MXU_COLS):
        cols = slice(t * MXU_COLS, (t + 1) * MXU_COLS)
        a = _dot(h, wgb_ref[:, cols])
        b = _dot(h, wub_ref[:, cols])
        p_ref[:, cols] = (a * _gate_sigmoid(a) * b).astype(BF16)


def _down_kernel(p_ref, wd_ref, x_ref, gt_ref, ng_ref, sh_ref, sc_ref, *out_refs, final):
    xo_ref, h_ref = (out_refs[1], out_refs[0]) if final else out_refs
    p = p_ref[...]
    for t in range(wd_ref.shape[1] // MXU_COLS):
        cols = slice(t * MXU_COLS, (t + 1) * MXU_COLS)
        xo_ref[:, cols] = x_ref[:, cols] + gt_ref[:, cols] * _dot(p, wd_ref[:, cols])

    def norm(rows):
        x = xo_ref[rows, :]
        if final:
            h_ref[rows, :] = x * lax.rsqrt(jnp.mean(x * x, axis=-1, keepdims=True) + EPS) * ng_ref[...]
        else:
            h_ref[rows, :] = _norm_mod(x, ng_ref[...], sh_ref[...], sc_ref[...]).astype(BF16)
    _row_loop(ROW_TILE, norm, NORM_STEP)


def _ffn(h2, w_gate_up, w_down, layer, x, mod, next_gain, next_mod, rows_per_batch, n_batch, n_rows):
    d = x.shape[1]
    hidden = w_down.shape[1]
    tm = BIG_ROW_TILE
    n_hid = hidden // FFN_HID_TILE
    p = pl.pallas_call(
        _gateup_kernel,
        grid=(n_hid, n_rows // tm),
        in_specs=[
            pl.BlockSpec((tm, d), lambda j, i: (i, 0)),
            pl.BlockSpec((None, d, FFN_HID_TILE), lambda j, i: (layer, 0, j)),
            pl.BlockSpec((None, d, FFN_HID_TILE), lambda j, i: (layer, 0, j + n_hid)),
        ],
        out_specs=pl.BlockSpec((tm, FFN_HID_TILE), lambda j, i: (i, j)),
        out_shape=jax.ShapeDtypeStruct((n_rows, hidden), BF16),
        scratch_shapes=[pltpu.VMEM((d, FFN_HID_TILE), BF16), pltpu.VMEM((d, FFN_HID_TILE), BF16)],
        compiler_params=_params("arbitrary", "arbitrary"),
        name="ffn_gate_up",
    )(h2, w_gate_up, w_gate_up)
    tm = ROW_TILE
    final = next_mod is None
    row = lambda i: (i, 0)
    const = lambda i: (0, 0)
    norm_mod = mod if final else next_mod
    x_out = jax.ShapeDtypeStruct((n_rows, d), F32)
    outs = pl.pallas_call(
        functools.partial(_down_kernel, final=final),
        grid=(n_rows // tm,),
        in_specs=[
            pl.BlockSpec((tm, hidden), row),
            pl.BlockSpec((None, hidden, d), lambda i: (layer, 0, 0)),
            pl.BlockSpec((tm, d), row),
            _mod_spec(d, 5, tm, rows_per_batch, n_batch),
            pl.BlockSpec((1, d), const),
            _mod_spec(d, 0, tm, rows_per_batch, n_batch),
            _mod_spec(d, 1, tm, rows_per_batch, n_batch),
        ],
        out_specs=[pl.BlockSpec((tm, d), row)] * (1 if final else 2),
        out_shape=[x_out] if final else [x_out, jax.ShapeDtypeStruct((n_rows, d), BF16)],
        scratch_shapes=[pltpu.VMEM((tm, d), F32)] if final else [],
        compiler_params=_params("parallel"),
        name="ffn_down",
    )(p, w_down, x, mod, next_gain.reshape(1, d), norm_mod, norm_mod)
    return (outs[0], None) if final else outs


def _ret_kernel(lg_ref, q_ref, k_ref, v_ref, qc_ref, kc_ref, vc_ref, cos_ref, sin_ref,
                o_ref, oc_ref, dec_ref, kr_ref, sbs_ref, sf_ref, sb_ref, *, k_scale):
    c = RET_CHUNK
    dk = q_ref.shape[1]
    half = dk // 2
    n_chunks = q_ref.shape[0] // c
    head = pl.program_id(1)
    lgf = lg_ref[0, head]
    lgb = lg_ref[1, head]

    n_i = lax.broadcasted_iota(jnp.int32, (c, c), 0).astype(F32)
    m_i = lax.broadcasted_iota(jnp.int32, (c, c), 1).astype(F32)
    diff = n_i - m_i
    dec_ref[0] = (jnp.where(diff >= 0, jnp.exp(lgf * jnp.maximum(diff, 0.0)), 0.0)
                  + jnp.where(diff <= 0, jnp.exp(lgb * jnp.maximum(-diff, 0.0)), 0.0))
    t_i = lax.broadcasted_iota(jnp.int32, (c, dk), 0).astype(F32)
    dec_ref[1] = jnp.exp(lgf * (t_i + 1.0))
    dec_ref[2] = jnp.exp(lgb * (c - t_i))
    dec_ref[3] = jnp.exp(lgf * (c - 1.0 - t_i))
    dec_ref[4] = jnp.exp(lgb * t_i)
    cf = jnp.exp(lgf * c)
    cb = jnp.exp(lgb * c)

    def head_norm(o):
        return o * lax.rsqrt(jnp.mean(o * o, axis=-1, keepdims=True) + EPS)

    def rope(t, rows):
        cos = cos_ref[rows, :]
        sin = sin_ref[rows, :]
        t1 = t[:, :half]
        t2 = t[:, half:]
        return jnp.concatenate([t1 * cos - t2 * sin, t1 * sin + t2 * cos], axis=-1)

    qc = qc_ref[...]
    kc = (kc_ref[...].astype(F32) * k_scale)
    vc = vc_ref[...]
    sc = _dot_nt(qc, kc.astype(BF16)) * dec_ref[0]
    oc_ref[...] = head_norm(_dot(sc.astype(BF16), vc)).astype(oc_ref.dtype)
    sf_ref[...] = _dot_tn((kc * dec_ref[3]).astype(BF16), vc)
    sb_ref[...] = _dot_tn((kc * dec_ref[4]).astype(BF16), vc)

    unroll = math.gcd(RET_UNROLL, n_chunks)
    n_steps = n_chunks // unroll

    def chunk_rows(ci):
        return pl.ds(pl.multiple_of(ci * c, c), c)

    def bwd_local(i, carry):
        cis = [i * unroll + u for u in range(unroll)]
        rows = [chunk_rows(ci) for ci in cis]
        krs = [rope(k_ref[r, :].astype(F32), r) * k_scale for r in rows]
        kvs = [_dot_tn((kr * dec_ref[4]).astype(BF16), v_ref[r, :]) for kr, r in zip(krs, rows)]
        for ci, r, kr, kv in zip(cis, rows, krs, kvs):
            kr_ref[r, :] = kr.astype(BF16)
            sbs_ref[ci] = kv
        return carry
    lax.fori_loop(0, n_steps, bwd_local, 0)

    def bwd_state(i, carry):
        ci = n_chunks - 1 - i
        s = sb_ref[...]
        kv = sbs_ref[ci]
        sbs_ref[ci] = s
        sb_ref[...] = s * cb + kv
        return carry
    lax.fori_loop(0, n_chunks, bwd_state, 0)

    def fwd(i, carry):
        cis = [i * unroll + u for u in range(unroll)]
        rows = [chunk_rows(ci) for ci in cis]
        qrs = [rope(q_ref[r, :].astype(F32), r).astype(BF16) for r in rows]
        krs = [kr_ref[r, :] for r in rows]
        vs = [v_ref[r, :] for r in rows]
        scs = [_dot_nt(qr, kr) for qr, kr in zip(qrs, krs)]
        kvs = [_dot_tn((kr.astype(F32) * dec_ref[3]).astype(BF16), v) for kr, v in zip(krs, vs)]
        sfs = [sf_ref[...]]
        for kv in kvs:
            sfs.append(sfs[-1] * cf + kv)
        sf_ref[...] = sfs[-1]
        intra = [_dot((sc * dec_ref[0]).astype(BF16), v) for sc, v in zip(scs, vs)]
        inter_f = [_dot(qr, s.astype(BF16)) for qr, s in zip(qrs, sfs)]
        inter_b = [_dot(qr, sbs_ref[ci].astype(BF16)) for qr, ci in zip(qrs, cis)]
        for r, o1, o2, o3 in zip(rows, intra, inter_f, inter_b):
            o = o1 + dec_ref[1] * o2 + dec_ref[2] * o3
            o_ref[r, :] = head_norm(o).astype(o_ref.dtype)
        return carry
    lax.fori_loop(0, n_steps, fwd, 0)


def _retention(proj, log_gamma, cos, sin, n_batch, seq, ctx_len):
    heads = RET_HEADS
    dk = proj.shape[1] // (4 * heads)
    d = heads * dk
    assert dk == RET_CHUNK and ctx_len == RET_CHUNK and seq % RET_CHUNK == 0
    ctx_row0 = (n_batch * seq) // ctx_len
    lat = lambda part: pl.BlockSpec((seq, dk), lambda b, h: (b, part * heads + h))
    ctx = lambda part: pl.BlockSpec((ctx_len, dk), lambda b, h: (ctx_row0 + b, part * heads + h))
    tab = pl.BlockSpec((seq, dk // 2), lambda b, h: (0, 0))
    n_chunks = seq // RET_CHUNK
    return pl.pallas_call(
        functools.partial(_ret_kernel, k_scale=dk ** -0.5),
        grid=(n_batch, heads),
        in_specs=[pl.BlockSpec(memory_space=pltpu.SMEM),
                  lat(0), lat(1), lat(2), ctx(0), ctx(1), ctx(2), tab, tab],
        out_specs=[pl.BlockSpec((seq, dk), lambda b, h: (b, h)),
                   pl.BlockSpec((ctx_len, dk), lambda b, h: (b, h))],
        out_shape=[jax.ShapeDtypeStruct((n_batch * seq, d), BF16),
                   jax.ShapeDtypeStruct((n_batch * ctx_len, d), BF16)],
        scratch_shapes=[
            pltpu.VMEM((5, RET_CHUNK, RET_CHUNK), F32),
            pltpu.VMEM((seq, dk), BF16),
            pltpu.VMEM((n_chunks, dk, dk), F32),
            pltpu.VMEM((dk, dk), F32),
            pltpu.VMEM((dk, dk), F32),
        ],
        compiler_params=_params("parallel", "arbitrary"),
        name="retention",
    )(log_gamma, proj, proj, proj, proj, proj, proj, cos, sin)


def _hg_gates(z, lb):
    f = lb + (1.0 - lb) * _sigmoid(z)
    return f, jnp.log2(f), 1.0 - f


def _hg_cumsum(tri_ref, g):
    dk = g.shape[1]
    hi = g.astype(BF16)
    lo = (g - hi.astype(F32)).astype(BF16)
    r = _dot(tri_ref[...], jnp.concatenate([hi, lo], axis=1))
    return r[:, :dk] + r[:, dk:]


def _tiles(x):
    return [x[j * V7X_SUBLANES:(j + 1) * V7X_SUBLANES] for j in range(x.shape[0] // V7X_SUBLANES)]


def _row_of_tile(tile, r):
    return jnp.broadcast_to(tile[r:r + 1, :], tile.shape)


def _hg_levels(up, lo, cv, cin):
    upt, lot, cvt, cint = _tiles(up), _tiles(lo), _tiles(cv), _tiles(cin)
    nt = len(cvt)
    zero = jnp.zeros_like(cvt[0])
    lastt = [_row_of_tile(t, V7X_SUBLANES - 1) for t in cint]
    out = []
    bt = nt
    while bt >= 2:
        ht = bt // 2
        hi_rows, lo_rows = [], []
        for j in range(nt):
            b0 = (j // bt) * bt
            ref = lastt[b0 + ht - 1]
            if j - b0 >= ht:
                hi_rows.append(jnp.exp2(cvt[j] - ref) * upt[j])
                lo_rows.append(zero)
            else:
                hi_rows.append(zero)
                lo_rows.append(jnp.exp2(ref - cvt[j]) * lot[j])
        out.append((jnp.concatenate(hi_rows, axis=0), jnp.concatenate(lo_rows, axis=0)))
        bt = ht
    sub = lax.broadcasted_iota(jnp.int32, zero.shape, 0)
    for size in (8, 4):
        upper = (sub & (size // 2)) != 0
        sign = jnp.where(upper, 1.0, -1.0)
        hi_rows, lo_rows = [], []
        for j in range(nt):
            if size == 8:
                ref = _row_of_tile(cint[j], 3)
            else:
                ref = jnp.where(sub < 4, _row_of_tile(cint[j], 1), _row_of_tile(cint[j], 5))
            z = jnp.exp2((cvt[j] - ref) * sign) * jnp.where(upper, upt[j], lot[j])
            hi_rows.append(jnp.where(upper, z, 0.0))
            lo_rows.append(jnp.where(upper, 0.0, z))
        out.append((jnp.concatenate(hi_rows, axis=0), jnp.concatenate(lo_rows, axis=0)))
    return out


def _hg_bwd_local(rows, ci, g_ref, key_ref, v_ref, tri_ref, cumb_ref, kv_ref):
    g = g_ref[rows, :]
    cum = _hg_cumsum(tri_ref, g)
    cumb_ref[rows, :] = cum
    k_dec = key_ref[rows, :].astype(F32) * jnp.exp2(cum - g)
    kv_ref[ci] = _dot_tn(v_ref[rows, :], k_dec.astype(BF16))


def _hg_step_out(rows_list, cis, q_ref, gf_ref, kf_ref, gb_ref, kb_ref, v_ref, tri_ref, cumb_ref,
                 sbs_ref, sf_ref, out_ref, pair_xor):
    n = len(cis)
    q = [q_ref[r, :].astype(F32) for r in rows_list]
    kf = [kf_ref[r, :].astype(F32) for r in rows_list]
    kb = [kb_ref[r, :].astype(F32) for r in rows_list]
    v = [v_ref[r, :] for r in rows_list]
    gf = [gf_ref[r, :] for r in rows_list]
    gb = [gb_ref[r, :] for r in rows_list]
    c = q[0].shape[0]

    cumf = [_hg_cumsum(tri_ref, g) for g in gf]
    cumb = [cumb_ref[r, :] for r in rows_list]
    cumxb = [cb - g for cb, g in zip(cumb, gb)]
    totf = [cf[c - 1:, :] for cf in cumf]
    totb = [cb[c - 1:, :] for cb in cumb]

    kvs = [_dot_tn(v[u], (kf[u] * jnp.exp2(totf[u] - cumf[u])).astype(BF16)) for u in range(n)]
    sf = [sf_ref[...]]
    for u in range(n):
        sf.append(sf[u] * jnp.exp2(totf[u]) + kvs[u])
    sf_ref[...] = sf[n]

    odd = (lax.broadcasted_iota(jnp.int32, q[0].shape, 0) & 1) != 0
    scores = []
    for u in range(n):
        lev_f = _hg_levels(q[u], kf[u], cumf[u], cumf[u])
        lev_b = _hg_levels(kb[u], q[u], cumxb[u], cumb[u])
        lev_f.append((jnp.where(odd, q[u] * jnp.exp2(gf[u]), 0.0), jnp.where(odd, 0.0, kf[u])))
        lev_b.append((jnp.where(odd, kb[u], 0.0), jnp.where(odd, 0.0, q[u] * jnp.exp2(gb[u]))))
        a = None
        size = c
        for (xf, yf), (yb, xb) in zip(lev_f, lev_b):
            x = jnp.concatenate([xf.astype(BF16), xb.astype(BF16)], axis=1)
            y = jnp.concatenate([yf.astype(BF16), yb.astype(BF16)], axis=1)
            p = _dot_nt(x, y)
            a = p if a is None else jnp.where(pair_xor < size, p, a)
            size //= 2
        scores.append(a.astype(BF16))

    inter = []
    for u in range(n):
        q_dec = jnp.concatenate([(q[u] * jnp.exp2(cumf[u])).astype(BF16),
                                 (q[u] * jnp.exp2(totb[u] - cumxb[u])).astype(BF16)], axis=1)
        states = jnp.concatenate([sf[u].astype(BF16), sbs_ref[cis[u]].astype(BF16)], axis=1)
        inter.append(_dot_nt(q_dec, states))

    for u in range(n):
        o = _dot(scores[u], v[u]) + inter[u]
        o += jnp.sum(q[u] * (kf[u] + kb[u]), axis=-1, keepdims=True) * v[u].astype(F32)
        out_ref[rows_list[u], :] = o.astype(out_ref.dtype)


def _hg_kernel(q_ref, gf_ref, kf_ref, gb_ref, kb_ref, v_ref,
               qc_ref, gfc_ref, kfc_ref, gbc_ref, kbc_ref, vc_ref,
               o_ref, oc_ref, tri_ref, cumb_ref, sbs_ref, sf_ref, sb_ref):
    c = HG_CHUNK
    n_i = lax.broadcasted_iota(jnp.int32, (c, c), 0)
    m_i = lax.broadcasted_iota(jnp.int32, (c, c), 1)
    pair_xor = n_i ^ m_i
    tri_ref[...] = jnp.where(m_i <= n_i, 1.0, 0.0).astype(BF16)

    def run(qr_ref, gfr_ref, kfr_ref, gbr_ref, kbr_ref, vr_ref, out_ref):
        n_chunks = qr_ref.shape[0] // c
        unroll = math.gcd(HG_UNROLL, n_chunks)
        n_steps = n_chunks // unroll

        def chunk_rows(ci):
            return pl.ds(pl.multiple_of(ci * c, c), c)

        local_unroll = math.gcd(HG_LOCAL_UNROLL, n_chunks)

        def bwd_local(i, carry):
            cis = [i * local_unroll + u for u in range(local_unroll)]
            gs = [gbr_ref[chunk_rows(ci), :] for ci in cis]
            cums = [_hg_cumsum(tri_ref, g) for g in gs]
            k_decs = [(kbr_ref[chunk_rows(ci), :].astype(F32) * jnp.exp2(cum - g)).astype(BF16)
                      for ci, g, cum in zip(cis, gs, cums)]
            kvs = [_dot_tn(vr_ref[chunk_rows(ci), :], k_dec) for ci, k_dec in zip(cis, k_decs)]
            for ci, cum, kv in zip(cis, cums, kvs):
                cumb_ref[chunk_rows(ci), :] = cum
                sbs_ref[ci] = kv
            return carry
        lax.fori_loop(0, n_chunks // local_unroll, bwd_local, 0)

        def bwd_state(i, carry):
            s = sb_ref[...]
            for u in range(unroll):
                ci = n_chunks - 1 - (i * unroll + u)
                kv = sbs_ref[ci]
                sbs_ref[ci] = s
                last_tile = cumb_ref[pl.ds(pl.multiple_of(ci * c + c - V7X_SUBLANES, V7X_SUBLANES),
                                           V7X_SUBLANES), :]
                s = s * jnp.exp2(last_tile[V7X_SUBLANES - 1:, :]) + kv
            sb_ref[...] = s
            return carry
        lax.fori_loop(0, n_steps, bwd_state, 0)

        def fwd(i, carry):
            cis = [i * unroll + u for u in range(unroll)]
            _hg_step_out([chunk_rows(ci) for ci in cis], cis, qr_ref, gfr_ref, kfr_ref, gbr_ref, kbr_ref,
                         vr_ref, tri_ref, cumb_ref, sbs_ref, sf_ref, out_ref, pair_xor)
            return carry
        lax.fori_loop(0, n_steps, fwd, 0)

    sf_ref[...] = jnp.zeros_like(sf_ref)
    sb_ref[...] = jnp.zeros_like(sb_ref)
    run(qc_ref, gfc_ref, kfc_ref, gbc_ref, kbc_ref, vc_ref, oc_ref)
    run(q_ref, gf_ref, kf_ref, gb_ref, kb_ref, v_ref, o_ref)


def _hgrn(qiv, log_f, key, n_batch, seq, ctx_len):
    dk = HG_EXPAND
    heads = log_f.shape[1] // (2 * dk)
    d = heads * dk
    assert seq % HG_CHUNK == 0 and ctx_len % HG_CHUNK == 0 and ctx_len <= seq
    ctx_row0 = (n_batch * seq) // ctx_len
    lat = lambda part: pl.BlockSpec((seq, dk), lambda b, h: (b, part * heads + h))
    ctx = lambda part: pl.BlockSpec((ctx_len, dk), lambda b, h: (ctx_row0 + b, part * heads + h))
    return pl.pallas_call(
        _hg_kernel,
        grid=(n_batch, heads),
        in_specs=[lat(0), lat(0), lat(0), lat(1), lat(1), lat(1),
                  ctx(0), ctx(0), ctx(0), ctx(1), ctx(1), ctx(1)],
        out_specs=[pl.BlockSpec((seq, dk), lambda b, h: (b, h)),
                   pl.BlockSpec((ctx_len, dk), lambda b, h: (b, h))],
        out_shape=[jax.ShapeDtypeStruct((n_batch * seq, d), BF16),
                   jax.ShapeDtypeStruct((n_batch * ctx_len, d), BF16)],
        scratch_shapes=[
            pltpu.VMEM((HG_CHUNK, HG_CHUNK), BF16),
            pltpu.VMEM((seq, dk), F32),
            pltpu.VMEM((seq // HG_CHUNK, dk, dk), F32),
            pltpu.VMEM((dk, dk), F32),
            pltpu.VMEM((dk, dk), F32),
        ],
        compiler_params=_params("parallel", "arbitrary"),
        name="hgrn",
    )(qiv, log_f, key, log_f, key, qiv, qiv, log_f, key, log_f, key, qiv)


def _rope_tables(seq, head_dim):
    quarter = head_dim // 4
    rows = jnp.repeat(jnp.arange(seq // GRID_W, dtype=F32), GRID_W)
    cols = jnp.tile(jnp.arange(GRID_W, dtype=F32), seq // GRID_W)
    inv_freq = ROPE_BASE ** (-jnp.arange(quarter, dtype=F32) / quarter)
    ang = jnp.concatenate([rows[:, None] * inv_freq, cols[:, None] * inv_freq], axis=-1)
    return jnp.cos(ang), jnp.sin(ang)


def kernel(x, c, ctx, c_ctx, ada_w, ada_b, norm1_g, norm2_g, ret_w_in, ret_w_out, ret_decay_logits,
           hg_w_in, hg_w_out, hg_norm_g, hg_lower_bounds, ffn_w_gate_up, ffn_w_down, final_norm_g):
    n_batch, seq, d = x.shape
    ctx_len = ctx.shape[1]
    depth = ada_w.shape[0]
    n_lat = n_batch * seq
    n_all = n_lat + n_batch * ctx_len
    assert seq % BIG_ROW_TILE == 0 and (n_batch * ctx_len) % BIG_ROW_TILE == 0

    cond_rows = -(-(n_batch + 1) // V7X_SUBLANES) * V7X_SUBLANES
    cond = jnp.zeros((cond_rows, d), F32).at[:n_batch].set(c).at[n_batch].set(c_ctx)
    mods = _ada_mod(cond, ada_w, ada_b).reshape(depth, cond_rows, 1, 6 * d)

    lb_p = jax.nn.softmax(hg_lower_bounds.astype(F32), axis=0)
    lower_bounds = jnp.cumsum(lb_p, axis=0) - lb_p[0]
    log_gamma = jax.nn.log_sigmoid(ret_decay_logits.astype(F32))
    cos, sin = _rope_tables(seq, d // RET_HEADS)

    ret_in, ret_out = ret_w_in, ret_w_out.astype(BF16)
    hg_in, hg_out = hg_w_in, hg_w_out.astype(BF16)
    ffn_gate_up, ffn_down = ffn_w_gate_up, ffn_w_down.astype(BF16)

    xs, h1 = _prenorm(x.reshape(n_lat, d), ctx.reshape(n_batch * ctx_len, d), norm1_g[0], mods[0],
                      seq, n_batch)
    for layer in range(depth):
        last = layer == depth - 1
        j = layer // N_MIXERS
        mod = mods[layer]
        n_rows = n_lat if last else n_all
        retention = layer % N_MIXERS == 0
        w_in, w_out = (ret_in, ret_out) if retention else (hg_in, hg_out)
        tiles_per_part = d // IN_COL_TILE
        if retention:
            proj = _proj(h1, w_in, j, (0, w_in.shape[2], 0),
                         [None] * (3 * tiles_per_part) + ["silu"] * tiles_per_part)
            o_lat, o_ctx = _retention(proj, log_gamma[j], cos, sin, n_batch, seq, ctx_len)
            norm_gain = jnp.ones((d,), F32)
        else:
            proj = _proj(h1, w_in, j, (0, tiles_per_part, 2 * tiles_per_part),
                         ["silu"] * tiles_per_part + [None] * tiles_per_part + ["sigmoid"] * tiles_per_part)
            log_f, key = _proj_gates(h1, w_in, j, lower_bounds[j], tiles_per_part, 2 * tiles_per_part)
            o_lat, o_ctx = _hgrn(proj, log_f, key, n_batch, seq, ctx_len)
            norm_gain = hg_norm_g[j]
        gate_block = proj.shape[1] // d - 1
        xs, h2 = _out_proj(o_lat, o_ctx, proj, gate_block, norm_gain, w_out, j, xs, mod, norm2_g[layer],
                           seq, n_batch, n_rows, not retention)
        next_gain, next_mod = (final_norm_g, None) if last else (norm1_g[layer + 1], mods[layer + 1])
        xs, h1 = _ffn(h2, ffn_gate_up, ffn_down, layer, xs, mod, next_gain, next_mod, seq, n_batch, n_rows)
    return xs.reshape(n_batch, seq, d)
```

```python
import functools
import math

import jax
import jax.numpy as jnp
from jax import lax
from jax.experimental import pallas as pl
from jax.experimental.pallas import tpu as pltpu

F32 = jnp.float32
BF16 = jnp.bfloat16

EPS = 1e-6
LOG2E = 1.4426950408889634
ROPE_BASE = 10000.0
GRID_W = 64
N_MIXERS = 2
RET_HEADS = 8
HG_EXPAND = 128

V7X_LANES = 128
V7X_SUBLANES = 8
V7X_VMEM_BYTES = 64 * 1024 * 1024

ROW_TILE = 512
BIG_ROW_TILE = 1024
IN_COL_TILE = 1024
FFN_HID_TILE = 512
MXU_COLS = 256
ADA_COL_TILE = 1024
RET_CHUNK = 256
RET_UNROLL = 8
HG_CHUNK = 128
HG_UNROLL = 32
HG_LOCAL_UNROLL = 16
ROW_STEP = 64
NORM_STEP = 128
CAST_STEP = 256
VMEM_LIMIT = 56 * 1024 * 1024


def _params(*semantics):
    return pltpu.CompilerParams(dimension_semantics=semantics, vmem_limit_bytes=VMEM_LIMIT)


def _sigmoid(x):
    return 1.0 / (1.0 + jnp.exp(-x))


def _gate_sigmoid(x):
    return 0.5 * jnp.tanh(0.5 * x) + 0.5


def _dot(a, b):
    return jnp.dot(a, b, preferred_element_type=F32)


def _dot_nt(a, b):
    return lax.dot_general(a, b, (((1,), (1,)), ((), ())), preferred_element_type=F32)


def _dot_tn(a, b):
    return lax.dot_general(a, b, (((0,), (0,)), ((), ())), preferred_element_type=F32)


def _ada_kernel(c_ref, w_ref, b_ref, o_ref):
    c = c_ref[...]
    a = (c * _sigmoid(c)).astype(BF16)
    o_ref[...] = _dot(a, w_ref[...].astype(BF16)) + b_ref[...]


def _ada_mod(cond, ada_w, ada_b):
    depth, d, n = ada_w.shape
    rows = cond.shape[0]
    return pl.pallas_call(
        _ada_kernel,
        grid=(depth, n // ADA_COL_TILE),
        in_specs=[
            pl.BlockSpec((rows, d), lambda l, j: (0, 0)),
            pl.BlockSpec((None, d, ADA_COL_TILE), lambda l, j: (l, 0, j)),
            pl.BlockSpec((None, 1, ADA_COL_TILE), lambda l, j: (l, 0, j)),
        ],
        out_specs=pl.BlockSpec((None, rows, ADA_COL_TILE), lambda l, j: (l, 0, j)),
        out_shape=jax.ShapeDtypeStruct((depth, rows, n), F32),
        compiler_params=_params("parallel", "parallel"),
        name="ada_mod",
    )(cond, ada_w, ada_b.reshape(depth, 1, n))


def _norm_mod(x, gain, shift, scale):
    return x * lax.rsqrt(jnp.mean(x * x, axis=-1, keepdims=True) + EPS) * (gain * (1.0 + scale)) + shift


def _row_loop(rows, body, step_rows=ROW_STEP):
    def step(i, carry):
        body(pl.ds(pl.multiple_of(i * step_rows, step_rows), step_rows))
        return carry
    lax.fori_loop(0, rows // step_rows, step, 0)


def _mod_spec(width, slot, row_tile, rows_per_batch, n_batch, col_axis=None):
    def index(*ids):
        col = slot if col_axis is None else slot + ids[col_axis]
        return (jnp.minimum((ids[0] * row_tile) // rows_per_batch, n_batch), 0, col)
    return pl.BlockSpec((None, 1, width), index)


def _prenorm_kernel(xl_ref, xc_ref, g_ref, sh_ref, sc_ref, xs_ref, h_ref, *, n_lat_tiles):
    def copy_norm(x_ref):
        def body(rows):
            x = x_ref[rows, :]
            xs_ref[rows, :] = x
            h_ref[rows, :] = _norm_mod(x, g_ref[...], sh_ref[...], sc_ref[...]).astype(BF16)
        _row_loop(ROW_TILE, body, NORM_STEP)

    tile = pl.program_id(0)
    pl.when(tile < n_lat_tiles)(functools.partial(copy_norm, xl_ref))
    pl.when(tile >= n_lat_tiles)(functools.partial(copy_norm, xc_ref))


def _prenorm(x_lat, x_ctx, gain, mod, rows_per_batch, n_batch):
    d = x_lat.shape[1]
    tm = ROW_TILE
    n_lat_tiles = x_lat.shape[0] // tm
    r = x_lat.shape[0] + x_ctx.shape[0]
    row = lambda i: (i, 0)
    return pl.pallas_call(
        functools.partial(_prenorm_kernel, n_lat_tiles=n_lat_tiles),
        grid=(r // tm,),
        in_specs=[
            pl.BlockSpec((tm, d), lambda i: (jnp.minimum(i, n_lat_tiles - 1), 0)),
            pl.BlockSpec((tm, d), lambda i: (jnp.maximum(i - n_lat_tiles, 0), 0)),
            pl.BlockSpec((1, d), lambda i: (0, 0)),
            _mod_spec(d, 0, tm, rows_per_batch, n_batch),
            _mod_spec(d, 1, tm, rows_per_batch, n_batch),
        ],
        out_specs=[pl.BlockSpec((tm, d), row), pl.BlockSpec((tm, d), row)],
        out_shape=[jax.ShapeDtypeStruct((r, d), F32), jax.ShapeDtypeStruct((r, d), BF16)],
        compiler_params=_params("parallel"),
        name="prenorm",
    )(x_lat, x_ctx, gain.reshape(1, d), mod, mod)


def _col_tiles():
    return [slice(t * MXU_COLS, (t + 1) * MXU_COLS) for t in range(IN_COL_TILE // MXU_COLS)]


def _cast_weight_tile(w_ref, wb_ref):
    @pl.when(pl.program_id(1) == 0)
    def _():
        def body(rows):
            wb_ref[rows, :] = w_ref[rows, :].astype(BF16)
        _row_loop(w_ref.shape[0], body, CAST_STEP)


def _proj_kernel(h_ref, w_ref, o_ref, wb_ref, *, acts):
    _cast_weight_tile(w_ref, wb_ref)
    h = h_ref[...]

    def run(act):
        for cols in _col_tiles():
            a = _dot(h, wb_ref[:, cols])
            if act == "silu":
                a = a * _gate_sigmoid(a)
            elif act == "sigmoid":
                a = _gate_sigmoid(a)
            o_ref[:, cols] = a.astype(o_ref.dtype)

    j = pl.program_id(0)
    for act in sorted(set(acts), key=str):
        tiles = [t for t, a in enumerate(acts) if a == act]
        lo, hi = tiles[0], tiles[-1] + 1
        assert tiles == list(range(lo, hi))
        pl.when(jnp.logical_and(j >= lo, j < hi))(functools.partial(run, act))


def _proj(h, w, layer, col_tiles, acts):
    r, d = h.shape
    tm, tn = BIG_ROW_TILE, IN_COL_TILE
    first, skip_from, skip = col_tiles
    n_tiles = w.shape[2] // tn - first - skip
    assert len(acts) == n_tiles
    wcol = lambda j, i: (layer, 0, first + j + jnp.where(j >= skip_from, skip, 0))
    return pl.pallas_call(
        functools.partial(_proj_kernel, acts=tuple(acts)),
        grid=(n_tiles, r // tm),
        in_specs=[pl.BlockSpec((tm, d), lambda j, i: (i, 0)), pl.BlockSpec((None, d, tn), wcol)],
        out_specs=pl.BlockSpec((tm, tn), lambda j, i: (i, j)),
        out_shape=jax.ShapeDtypeStruct((r, n_tiles * tn), BF16),
        scratch_shapes=[pltpu.VMEM((d, tn), BF16)],
        compiler_params=_params("arbitrary", "arbitrary"),
        name="in_proj",
    )(h, w)


def _proj_gates_kernel(h_ref, w_ref, lb_ref, g_ref, key_ref, wb_ref):
    _cast_weight_tile(w_ref, wb_ref)
    h = h_ref[...]
    for cols in _col_tiles():
        _, g, key = _hg_gates(_dot(h, wb_ref[:, cols]), lb_ref[:, cols])
        g_ref[:, cols] = g
        key_ref[:, cols] = key.astype(key_ref.dtype)


def _proj_gates(h, w, layer, lower_bound, first_tile, n_tiles):
    r, d = h.shape
    tm, tn = BIG_ROW_TILE, IN_COL_TILE
    lb_tiles = lower_bound.shape[0] // tn
    out = pl.BlockSpec((tm, tn), lambda j, i: (i, j))
    return pl.pallas_call(
        _proj_gates_kernel,
        grid=(n_tiles, r // tm),
        in_specs=[pl.BlockSpec((tm, d), lambda j, i: (i, 0)),
                  pl.BlockSpec((None, d, tn), lambda j, i: (layer, 0, first_tile + j)),
                  pl.BlockSpec((1, tn), lambda j, i: (0, j % lb_tiles))],
        out_specs=[out, out],
        out_shape=[jax.ShapeDtypeStruct((r, n_tiles * tn), F32),
                   jax.ShapeDtypeStruct((r, n_tiles * tn), BF16)],
        scratch_shapes=[pltpu.VMEM((d, tn), BF16)],
        compiler_params=_params("arbitrary", "arbitrary"),
        name="in_proj_gates",
    )(h, w, lower_bound.reshape(1, -1))


def _outproj_kernel(ol_ref, oc_ref, g_ref, ng_ref, w_ref, x_ref, gt_ref, n2_ref, sh_ref, sc_ref,
                    xo_ref, h_ref, y_ref, *, hgrn, n_lat_tiles):
    def gate_from(o_ref):
        def gate(rows):
            o = o_ref[rows, :].astype(F32)
            if hgrn:
                o = o * lax.rsqrt(jnp.mean(o * o, axis=-1, keepdims=True) + EPS) * ng_ref[...]
            y_ref[rows, :] = (o * g_ref[rows, :].astype(F32)).astype(BF16)
        _row_loop(ROW_TILE, gate)

    tile = pl.program_id(0)
    pl.when(tile < n_lat_tiles)(functools.partial(gate_from, ol_ref))
    pl.when(tile >= n_lat_tiles)(functools.partial(gate_from, oc_ref))

    y = y_ref[...]
    for t in range(w_ref.shape[1] // MXU_COLS):
        cols = slice(t * MXU_COLS, (t + 1) * MXU_COLS)
        xo_ref[:, cols] = x_ref[:, cols] + gt_ref[:, cols] * _dot(y, w_ref[:, cols])

    def norm(rows):
        h_ref[rows, :] = _norm_mod(xo_ref[rows, :], n2_ref[...], sh_ref[...], sc_ref[...]).astype(BF16)
    _row_loop(ROW_TILE, norm, NORM_STEP)


def _out_proj(o_lat, o_ctx, proj, gate_block, norm_gain, w, layer, x, mod, gain2, rows_per_batch, n_batch,
              n_rows, hgrn):
    d = x.shape[1]
    tm = ROW_TILE
    n_lat_tiles = o_lat.shape[0] // tm
    row = lambda i: (i, 0)
    const = lambda i: (0, 0)
    return pl.pallas_call(
        functools.partial(_outproj_kernel, hgrn=hgrn, n_lat_tiles=n_lat_tiles),
        grid=(n_rows // tm,),
        in_specs=[
            pl.BlockSpec((tm, d), lambda i: (jnp.minimum(i, n_lat_tiles - 1), 0)),
            pl.BlockSpec((tm, d), lambda i: (jnp.maximum(i - n_lat_tiles, 0), 0)),
            pl.BlockSpec((tm, d), lambda i: (i, gate_block)),
            pl.BlockSpec((1, d), const),
            pl.BlockSpec((None, d, d), lambda i: (layer, 0, 0)),
            pl.BlockSpec((tm, d), row),
            _mod_spec(d, 2, tm, rows_per_batch, n_batch),
            pl.BlockSpec((1, d), const),
            _mod_spec(d, 3, tm, rows_per_batch, n_batch),
            _mod_spec(d, 4, tm, rows_per_batch, n_batch),
        ],
        out_specs=[pl.BlockSpec((tm, d), row), pl.BlockSpec((tm, d), row)],
        out_shape=[jax.ShapeDtypeStruct((n_rows, d), F32), jax.ShapeDtypeStruct((n_rows, d), BF16)],
        scratch_shapes=[pltpu.VMEM((tm, d), BF16)],
        compiler_params=_params("parallel"),
        name="out_proj",
    )(o_lat, o_ctx, proj, norm_gain.reshape(1, d), w, x, mod, gain2.reshape(1, d), mod, mod)


def _gateup_kernel(h_ref, wg_ref, wu_ref, p_ref, wgb_ref, wub_ref):
    _cast_weight_tile(wg_ref, wgb_ref)
    _cast_weight_tile(wu_ref, wub_ref)
    h = h_ref[...]
    for t in range(FFN_HID_TILE // MXU_COLS):
        cols = slice(t * MXU_COLS, (t + 1) * MXU_COLS)
        a = _dot(h, wgb_ref[:, cols])
        b = _dot(h, wub_ref[:, cols])
        p_ref[:, cols] = (a * _gate_sigmoid(a) * b).astype(BF16)


def _down_kernel(p_ref, wd_ref, x_ref, gt_ref, ng_ref, sh_ref, sc_ref, *out_refs, final):
    xo_ref, h_ref = (out_refs[1], out_refs[0]) if final else out_refs
    p = p_ref[...]
    for t in range(wd_ref.shape[1] // MXU_COLS):
        cols = slice(t * MXU_COLS, (t + 1) * MXU_COLS)
        xo_ref[:, cols] = x_ref[:, cols] + gt_ref[:, cols] * _dot(p, wd_ref[:, cols])

    def norm(rows):
        x = xo_ref[rows, :]
        if final:
            h_ref[rows, :] = x * lax.rsqrt(jnp.mean(x * x, axis=-1, keepdims=True) + EPS) * ng_ref[...]
        else:
            h_ref[rows, :] = _norm_mod(x, ng_ref[...], sh_ref[...], sc_ref[...]).astype(BF16)
    _row_loop(ROW_TILE, norm, NORM_STEP)


def _ffn(h2, w_gate_up, w_down, layer, x, mod, next_gain, next_mod, rows_per_batch, n_batch, n_rows):
    d = x.shape[1]
    hidden = w_down.shape[1]
    tm = BIG_ROW_TILE
    n_hid = hidden // FFN_HID_TILE
    p = pl.pallas_call(
        _gateup_kernel,
        grid=(n_hid, n_rows // tm),
        in_specs=[
            pl.BlockSpec((tm, d), lambda j, i: (i, 0)),
            pl.BlockSpec((None, d, FFN_HID_TILE), lambda j, i: (layer, 0, j)),
            pl.BlockSpec((None, d, FFN_HID_TILE), lambda j, i: (layer, 0, j + n_hid)),
        ],
        out_specs=pl.BlockSpec((tm, FFN_HID_TILE), lambda j, i: (i, j)),
        out_shape=jax.ShapeDtypeStruct((n_rows, hidden), BF16),
        scratch_shapes=[pltpu.VMEM((d, FFN_HID_TILE), BF16), pltpu.VMEM((d, FFN_HID_TILE), BF16)],
        compiler_params=_params("arbitrary", "arbitrary"),
        name="ffn_gate_up",
    )(h2, w_gate_up, w_gate_up)
    tm = ROW_TILE
    final = next_mod is None
    row = lambda i: (i, 0)
    const = lambda i: (0, 0)
    norm_mod = mod if final else next_mod
    x_out = jax.ShapeDtypeStruct((n_rows, d), F32)
    outs = pl.pallas_call(
        functools.partial(_down_kernel, final=final),
        grid=(n_rows // tm,),
        in_specs=[
            pl.BlockSpec((tm, hidden), row),
            pl.BlockSpec((None, hidden, d), lambda i: (layer, 0, 0)),
            pl.BlockSpec((tm, d), row),
            _mod_spec(d, 5, tm, rows_per_batch, n_batch),
            pl.BlockSpec((1, d), const),
            _mod_spec(d, 0, tm, rows_per_batch, n_batch),
            _mod_spec(d, 1, tm, rows_per_batch, n_batch),
        ],
        out_specs=[pl.BlockSpec((tm, d), row)] * (1 if final else 2),
        out_shape=[x_out] if final else [x_out, jax.ShapeDtypeStruct((n_rows, d), BF16)],
        scratch_shapes=[pltpu.VMEM((tm, d), F32)] if final else [],
        compiler_params=_params("parallel"),
        name="ffn_down",
    )(p, w_down, x, mod, next_gain.reshape(1, d), norm_mod, norm_mod)
    return (outs[0], None) if final else outs


def _ret_kernel(lg_ref, q_ref, k_ref, v_ref, qc_ref, kc_ref, vc_ref, cos_ref, sin_ref,
                o_ref, oc_ref, dec_ref, kr_ref, sbs_ref, sf_ref, sb_ref, *, k_scale):
    c = RET_CHUNK
    dk = q_ref.shape[1]
    half = dk // 2
    n_chunks = q_ref.shape[0] // c
    head = pl.program_id(1)
    lgf = lg_ref[0, head]
    lgb = lg_ref[1, head]

    n_i = lax.broadcasted_iota(jnp.int32, (c, c), 0).astype(F32)
    m_i = lax.broadcasted_iota(jnp.int32, (c, c), 1).astype(F32)
    diff = n_i - m_i
    dec_ref[0] = (jnp.where(diff >= 0, jnp.exp(lgf * jnp.maximum(diff, 0.0)), 0.0)
                  + jnp.where(diff <= 0, jnp.exp(lgb * jnp.maximum(-diff, 0.0)), 0.0))
    t_i = lax.broadcasted_iota(jnp.int32, (c, dk), 0).astype(F32)
    dec_ref[1] = jnp.exp(lgf * (t_i + 1.0))
    dec_ref[2] = jnp.exp(lgb * (c - t_i))
    dec_ref[3] = jnp.exp(lgf * (c - 1.0 - t_i))
    dec_ref[4] = jnp.exp(lgb * t_i)
    cf = jnp.exp(lgf * c)
    cb = jnp.exp(lgb * c)

    def head_norm(o):
        return o * lax.rsqrt(jnp.mean(o * o, axis=-1, keepdims=True) + EPS)

    def rope(t, rows):
        cos = cos_ref[rows, :]
        sin = sin_ref[rows, :]
        t1 = t[:, :half]
        t2 = t[:, half:]
        return jnp.concatenate([t1 * cos - t2 * sin, t1 * sin + t2 * cos], axis=-1)

    qc = qc_ref[...]
    kc = (kc_ref[...].astype(F32) * k_scale)
    vc = vc_ref[...]
    sc = _dot_nt(qc, kc.astype(BF16)) * dec_ref[0]
    oc_ref[...] = head_norm(_dot(sc.astype(BF16), vc)).astype(oc_ref.dtype)
    sf_ref[...] = _dot_tn((kc * dec_ref[3]).astype(BF16), vc)
    sb_ref[...] = _dot_tn((kc * dec_ref[4]).astype(BF16), vc)

    unroll = math.gcd(RET_UNROLL, n_chunks)
    n_steps = n_chunks // unroll

    def chunk_rows(ci):
        return pl.ds(pl.multiple_of(ci * c, c), c)

    def bwd_local(i, carry):
        cis = [i * unroll + u for u in range(unroll)]
        rows = [chunk_rows(ci) for ci in cis]
        krs = [rope(k_ref[r, :].astype(F32), r) * k_scale for r in rows]
        kvs = [_dot_tn((kr * dec_ref[4]).astype(BF16), v_ref[r, :]) for kr, r in zip(krs, rows)]
        for ci, r, kr, kv in zip(cis, rows, krs, kvs):
            kr_ref[r, :] = kr.astype(BF16)
            sbs_ref[ci] = kv
        return carry
    lax.fori_loop(0, n_steps, bwd_local, 0)

    def bwd_state(i, carry):
        ci = n_chunks - 1 - i
        s = sb_ref[...]
        kv = sbs_ref[ci]
        sbs_ref[ci] = s
        sb_ref[...] = s * cb + kv
        return carry
    lax.fori_loop(0, n_chunks, bwd_state, 0)

    def fwd(i, carry):
        cis = [i * unroll + u for u in range(unroll)]
        rows = [chunk_rows(ci) for ci in cis]
        qrs = [rope(q_ref[r, :].astype(F32), r).astype(BF16) for r in rows]
        krs = [kr_ref[r, :] for r in rows]
        vs = [v_ref[r, :] for r in rows]
        scs = [_dot_nt(qr, kr) for qr, kr in zip(qrs, krs)]
        kvs = [_dot_tn((kr.astype(F32) * dec_ref[3]).astype(BF16), v) for kr, v in zip(krs, vs)]
        sfs = [sf_ref[...]]
        for kv in kvs:
            sfs.append(sfs[-1] * cf + kv)
        sf_ref[...] = sfs[-1]
        intra = [_dot((sc * dec_ref[0]).astype(BF16), v) for sc, v in zip(scs, vs)]
        inter_f = [_dot(qr, s.astype(BF16)) for qr, s in zip(qrs, sfs)]
        inter_b = [_dot(qr, sbs_ref[ci].astype(BF16)) for qr, ci in zip(qrs, cis)]
        for r, o1, o2, o3 in zip(rows, intra, inter_f, inter_b):
            o = o1 + dec_ref[1] * o2 + dec_ref[2] * o3
            o_ref[r, :] = head_norm(o).astype(o_ref.dtype)
        return carry
    lax.fori_loop(0, n_steps, fwd, 0)


def _retention(proj, log_gamma, cos, sin, n_batch, seq, ctx_len):
    heads = RET_HEADS
    dk = proj.shape[1] // (4 * heads)
    d = heads * dk
    assert dk == RET_CHUNK and ctx_len == RET_CHUNK and seq % RET_CHUNK == 0
    ctx_row0 = (n_batch * seq) // ctx_len
    lat = lambda part: pl.BlockSpec((seq, dk), lambda b, h: (b, part * heads + h))
    ctx = lambda part: pl.BlockSpec((ctx_len, dk), lambda b, h: (ctx_row0 + b, part * heads + h))
    tab = pl.BlockSpec((seq, dk // 2), lambda b, h: (0, 0))
    n_chunks = seq // RET_CHUNK
    return pl.pallas_call(
        functools.partial(_ret_kernel, k_scale=dk ** -0.5),
        grid=(n_batch, heads),
        in_specs=[pl.BlockSpec(memory_space=pltpu.SMEM),
                  lat(0), lat(1), lat(2), ctx(0), ctx(1), ctx(2), tab, tab],
        out_specs=[pl.BlockSpec((seq, dk), lambda b, h: (b, h)),
                   pl.BlockSpec((ctx_len, dk), lambda b, h: (b, h))],
        out_shape=[jax.ShapeDtypeStruct((n_batch * seq, d), BF16),
                   jax.ShapeDtypeStruct((n_batch * ctx_len, d), BF16)],
        scratch_shapes=[
            pltpu.VMEM((5, RET_CHUNK, RET_CHUNK), F32),
            pltpu.VMEM((seq, dk), BF16),
            pltpu.VMEM((n_chunks, dk, dk), F32),
            pltpu.VMEM((dk, dk), F32),
            pltpu.VMEM((dk, dk), F32),
        ],
        compiler_params=_params("parallel", "arbitrary"),
        name="retention",
    )(log_gamma, proj, proj, proj, proj, proj, proj, cos, sin)


def _hg_gates(z, lb):
    f = lb + (1.0 - lb) * _sigmoid(z)
    return f, jnp.log2(f), 1.0 - f


def _hg_cumsum(tri_ref, g):
    dk = g.shape[1]
    hi = g.astype(BF16)
    lo = (g - hi.astype(F32)).astype(BF16)
    r = _dot(tri_ref[...], jnp.concatenate([hi, lo], axis=1))
    return r[:, :dk] + r[:, dk:]


def _tiles(x):
    return [x[j * V7X_SUBLANES:(j + 1) * V7X_SUBLANES] for j in range(x.shape[0] // V7X_SUBLANES)]


def _row_of_tile(tile, r):
    return jnp.broadcast_to(tile[r:r + 1, :], tile.shape)


def _hg_levels(up, lo, cv, cin):
    upt, lot, cvt, cint = _tiles(up), _tiles(lo), _tiles(cv), _tiles(cin)
    nt = len(cvt)
    zero = jnp.zeros_like(cvt[0])
    lastt = [_row_of_tile(t, V7X_SUBLANES - 1) for t in cint]
    out = []
    bt = nt
    while bt >= 2:
        ht = bt // 2
        hi_rows, lo_rows = [], []
        for j in range(nt):
            b0 = (j // bt) * bt
            ref = lastt[b0 + ht - 1]
            if j - b0 >= ht:
                hi_rows.append(jnp.exp2(cvt[j] - ref) * upt[j])
                lo_rows.append(zero)
            else:
                hi_rows.append(zero)
                lo_rows.append(jnp.exp2(ref - cvt[j]) * lot[j])
        out.append((jnp.concatenate(hi_rows, axis=0), jnp.concatenate(lo_rows, axis=0)))
        bt = ht
    sub = lax.broadcasted_iota(jnp.int32, zero.shape, 0)
    for size in (8, 4):
        upper = (sub & (size // 2)) != 0
        sign = jnp.where(upper, 1.0, -1.0)
        hi_rows, lo_rows = [], []
        for j in range(nt):
            if size == 8:
                ref = _row_of_tile(cint[j], 3)
            else:
                ref = jnp.where(sub < 4, _row_of_tile(cint[j], 1), _row_of_tile(cint[j], 5))
            z = jnp.exp2((cvt[j] - ref) * sign) * jnp.where(upper, upt[j], lot[j])
            hi_rows.append(jnp.where(upper, z, 0.0))
            lo_rows.append(jnp.where(upper, 0.0, z))
        out.append((jnp.concatenate(hi_rows, axis=0), jnp.concatenate(lo_rows, axis=0)))
    return out


def _hg_bwd_local(rows, ci, g_ref, key_ref, v_ref, tri_ref, cumb_ref, kv_ref):
    g = g_ref[rows, :]
    cum = _hg_cumsum(tri_ref, g)
    cumb_ref[rows, :] = cum
    k_dec = key_ref[rows, :].astype(F32) * jnp.exp2(cum - g)
    kv_ref[ci] = _dot_tn(v_ref[rows, :], k_dec.astype(BF16))


def _hg_step_out(rows_list, cis, q_ref, gf_ref, kf_ref, gb_ref, kb_ref, v_ref, tri_ref, cumb_ref,
                 sbs_ref, sf_ref, out_ref, pair_xor):
    n = len(cis)
    q = [q_ref[r, :].astype(F32) for r in rows_list]
    kf = [kf_ref[r, :].astype(F32) for r in rows_list]
    kb = [kb_ref[r, :].astype(F32) for r in rows_list]
    v = [v_ref[r, :] for r in rows_list]
    gf = [gf_ref[r, :] for r in rows_list]
    gb = [gb_ref[r, :] for r in rows_list]
    c = q[0].shape[0]

    cumf = [_hg_cumsum(tri_ref, g) for g in gf]
    cumb = [cumb_ref[r, :] for r in rows_list]
    cumxb = [cb - g for cb, g in zip(cumb, gb)]
    totf = [cf[c - 1:, :] for cf in cumf]
    totb = [cb[c - 1:, :] for cb in cumb]

    kvs = [_dot_tn(v[u], (kf[u] * jnp.exp2(totf[u] - cumf[u])).astype(BF16)) for u in range(n)]
    sf = [sf_ref[...]]
    for u in range(n):
        sf.append(sf[u] * jnp.exp2(totf[u]) + kvs[u])
    sf_ref[...] = sf[n]

    odd = (lax.broadcasted_iota(jnp.int32, q[0].shape, 0) & 1) != 0
    scores = []
    for u in range(n):
        lev_f = _hg_levels(q[u], kf[u], cumf[u], cumf[u])
        lev_b = _hg_levels(kb[u], q[u], cumxb[u], cumb[u])
        lev_f.append((jnp.where(odd, q[u] * jnp.exp2(gf[u]), 0.0), jnp.where(odd, 0.0, kf[u])))
        lev_b.append((jnp.where(odd, kb[u], 0.0), jnp.where(odd, 0.0, q[u] * jnp.exp2(gb[u]))))
        a = None
        size = c
        for (xf, yf), (yb, xb) in zip(lev_f, lev_b):
            x = jnp.concatenate([xf.astype(BF16), xb.astype(BF16)], axis=1)
            y = jnp.concatenate([yf.astype(BF16), yb.astype(BF16)], axis=1)
            p = _dot_nt(x, y)
            a = p if a is None else jnp.where(pair_xor < size, p, a)
            size //= 2
        scores.append(a.astype(BF16))

    inter = []
    for u in range(n):
        q_dec = jnp.concatenate([(q[u] * jnp.exp2(cumf[u])).astype(BF16),
                                 (q[u] * jnp.exp2(totb[u] - cumxb[u])).astype(BF16)], axis=1)
        states = jnp.concatenate([sf[u].astype(BF16), sbs_ref[cis[u]].astype(BF16)], axis=1)
        inter.append(_dot_nt(q_dec, states))

    for u in range(n):
        o = _dot(scores[u], v[u]) + inter[u]
        o += jnp.sum(q[u] * (kf[u] + kb[u]), axis=-1, keepdims=True) * v[u].astype(F32)
        out_ref[rows_list[u], :] = o.astype(out_ref.dtype)


def _hg_kernel(q_ref, gf_ref, kf_ref, gb_ref, kb_ref, v_ref,
               qc_ref, gfc_ref, kfc_ref, gbc_ref, kbc_ref, vc_ref,
               o_ref, oc_ref, tri_ref, cumb_ref, sbs_ref, sf_ref, sb_ref):
    c = HG_CHUNK
    n_i = lax.broadcasted_iota(jnp.int32, (c, c), 0)
    m_i = lax.broadcasted_iota(jnp.int32, (c, c), 1)
    pair_xor = n_i ^ m_i
    tri_ref[...] = jnp.where(m_i <= n_i, 1.0, 0.0).astype(BF16)

    def run(qr_ref, gfr_ref, kfr_ref, gbr_ref, kbr_ref, vr_ref, out_ref):
        n_chunks = qr_ref.shape[0] // c
        unroll = math.gcd(HG_UNROLL, n_chunks)
        n_steps = n_chunks // unroll

        def chunk_rows(ci):
            return pl.ds(pl.multiple_of(ci * c, c), c)

        local_unroll = math.gcd(HG_LOCAL_UNROLL, n_chunks)

        def bwd_local(i, carry):
            cis = [i * local_unroll + u for u in range(local_unroll)]
            gs = [gbr_ref[chunk_rows(ci), :] for ci in cis]
            cums = [_hg_cumsum(tri_ref, g) for g in gs]
            k_decs = [(kbr_ref[chunk_rows(ci), :].astype(F32) * jnp.exp2(cum - g)).astype(BF16)
                      for ci, g, cum in zip(cis, gs, cums)]
            kvs = [_dot_tn(vr_ref[chunk_rows(ci), :], k_dec) for ci, k_dec in zip(cis, k_decs)]
            for ci, cum, kv in zip(cis, cums, kvs):
                cumb_ref[chunk_rows(ci), :] = cum
                sbs_ref[ci] = kv
            return carry
        lax.fori_loop(0, n_chunks // local_unroll, bwd_local, 0)

        def bwd_state(i, carry):
            s = sb_ref[...]
            for u in range(unroll):
                ci = n_chunks - 1 - (i * unroll + u)
                kv = sbs_ref[ci]
                sbs_ref[ci] = s
                last_tile = cumb_ref[pl.ds(pl.multiple_of(ci * c + c - V7X_SUBLANES, V7X_SUBLANES),
                                           V7X_SUBLANES), :]
                s = s * jnp.exp2(last_tile[V7X_SUBLANES - 1:, :]) + kv
            sb_ref[...] = s
            return carry
        lax.fori_loop(0, n_steps, bwd_state, 0)

        def fwd(i, carry):
            cis = [i * unroll + u for u in range(unroll)]
            _hg_step_out([chunk_rows(ci) for ci in cis], cis, qr_ref, gfr_ref, kfr_ref, gbr_ref, kbr_ref,
                         vr_ref, tri_ref, cumb_ref, sbs_ref, sf_ref, out_ref, pair_xor)
            return carry
        lax.fori_loop(0, n_steps, fwd, 0)

    sf_ref[...] = jnp.zeros_like(sf_ref)
    sb_ref[...] = jnp.zeros_like(sb_ref)
    run(qc_ref, gfc_ref, kfc_ref, gbc_ref, kbc_ref, vc_ref, oc_ref)
    run(q_ref, gf_ref, kf_ref, gb_ref, kb_ref, v_ref, o_ref)


def _hgrn(qiv, log_f, key, n_batch, seq, ctx_len):
    dk = HG_EXPAND
    heads = log_f.shape[1] // (2 * dk)
    d = heads * dk
    assert seq % HG_CHUNK == 0 and ctx_len % HG_CHUNK == 0 and ctx_len <= seq
    ctx_row0 = (n_batch * seq) // ctx_len
    lat = lambda part: pl.BlockSpec((seq, dk), lambda b, h: (b, part * heads + h))
    ctx = lambda part: pl.BlockSpec((ctx_len, dk), lambda b, h: (ctx_row0 + b, part * heads + h))
    return pl.pallas_call(
        _hg_kernel,
        grid=(n_batch, heads),
        in_specs=[lat(0), lat(0), lat(0), lat(1), lat(1), lat(1),
                  ctx(0), ctx(0), ctx(0), ctx(1), ctx(1), ctx(1)],
        out_specs=[pl.BlockSpec((seq, dk), lambda b, h: (b, h)),
                   pl.BlockSpec((ctx_len, dk), lambda b, h: (b, h))],
        out_shape=[jax.ShapeDtypeStruct((n_batch * seq, d), BF16),
                   jax.ShapeDtypeStruct((n_batch * ctx_len, d), BF16)],
        scratch_shapes=[
            pltpu.VMEM((HG_CHUNK, HG_CHUNK), BF16),
            pltpu.VMEM((seq, dk), F32),
            pltpu.VMEM((seq // HG_CHUNK, dk, dk), F32),
            pltpu.VMEM((dk, dk), F32),
            pltpu.VMEM((dk, dk), F32),
        ],
        compiler_params=_params("parallel", "arbitrary"),
        name="hgrn",
    )(qiv, log_f, key, log_f, key, qiv, qiv, log_f, key, log_f, key, qiv)


def _rope_tables(seq, head_dim):
    quarter = head_dim // 4
    rows = jnp.repeat(jnp.arange(seq // GRID_W, dtype=F32), GRID_W)
    cols = jnp.tile(jnp.arange(GRID_W, dtype=F32), seq // GRID_W)
    inv_freq = ROPE_BASE ** (-jnp.arange(quarter, dtype=F32) / quarter)
    ang = jnp.concatenate([rows[:, None] * inv_freq, cols[:, None] * inv_freq], axis=-1)
    return jnp.cos(ang), jnp.sin(ang)


def kernel(x, c, ctx, c_ctx, ada_w, ada_b, norm1_g, norm2_g, ret_w_in, ret_w_out, ret_decay_logits,
           hg_w_in, hg_w_out, hg_norm_g, hg_lower_bounds, ffn_w_gate_up, ffn_w_down, final_norm_g):
    n_batch, seq, d = x.shape
    ctx_len = ctx.shape[1]
    depth = ada_w.shape[0]
    n_lat = n_batch * seq
    n_all = n_lat + n_batch * ctx_len
    assert seq % BIG_ROW_TILE == 0 and (n_batch * ctx_len) % BIG_ROW_TILE == 0

    cond_rows = -(-(n_batch + 1) // V7X_SUBLANES) * V7X_SUBLANES
    cond = jnp.zeros((cond_rows, d), F32).at[:n_batch].set(c).at[n_batch].set(c_ctx)
    mods = _ada_mod(cond, ada_w, ada_b).reshape(depth, cond_rows, 1, 6 * d)

    lb_p = jax.nn.softmax(hg_lower_bounds.astype(F32), axis=0)
    lower_bounds = jnp.cumsum(lb_p, axis=0) - lb_p[0]
    log_gamma = jax.nn.log_sigmoid(ret_decay_logits.astype(F32))
    cos, sin = _rope_tables(seq, d // RET_HEADS)

    ret_in, ret_out = ret_w_in, ret_w_out.astype(BF16)
    hg_in, hg_out = hg_w_in, hg_w_out.astype(BF16)
    ffn_gate_up, ffn_down = ffn_w_gate_up, ffn_w_down.astype(BF16)

    xs, h1 = _prenorm(x.reshape(n_lat, d), ctx.reshape(n_batch * ctx_len, d), norm1_g[0], mods[0],
                      seq, n_batch)
    for layer in range(depth):
        last = layer == depth - 1
        j = layer // N_MIXERS
        mod = mods[layer]
        n_rows = n_lat if last else n_all
        retention = layer % N_MIXERS == 0
        w_in, w_out = (ret_in, ret_out) if retention else (hg_in, hg_out)
        tiles_per_part = d // IN_COL_TILE
        if retention:
            proj = _proj(h1, w_in, j, (0, w_in.shape[2], 0),
                         [None] * (3 * tiles_per_part) + ["silu"] * tiles_per_part)
            o_lat, o_ctx = _retention(proj, log_gamma[j], cos, sin, n_batch, seq, ctx_len)
            norm_gain = jnp.ones((d,), F32)
        else:
            proj = _proj(h1, w_in, j, (0, tiles_per_part, 2 * tiles_per_part),
                         ["silu"] * tiles_per_part + [None] * tiles_per_part + ["sigmoid"] * tiles_per_part)
            log_f, key = _proj_gates(h1, w_in, j, lower_bounds[j], tiles_per_part, 2 * tiles_per_part)
            o_lat, o_ctx = _hgrn(proj, log_f, key, n_batch, seq, ctx_len)
            norm_gain = hg_norm_g[j]
        gate_block = proj.shape[1] // d - 1
        xs, h2 = _out_proj(o_lat, o_ctx, proj, gate_block, norm_gain, w_out, j, xs, mod, norm2_g[layer],
                           seq, n_batch, n_rows, not retention)
        next_gain, next_mod = (final_norm_g, None) if last else (norm1_g[layer + 1], mods[layer + 1])
        xs, h1 = _ffn(h2, ffn_gate_up, ffn_down, layer, xs, mod, next_gain, next_mod, seq, n_batch, n_rows)
    return xs.reshape(n_batch, seq, d)
```
